```python
import jax, jax.numpy as jnp
from jax import lax
import numpy as np

D_MODEL = 1024
BATCH = 8
SEQ = 4096
DEPTH = 2

HEAD_DIM = 64
POOL_WIDTH = D_MODEL // 4
POOL_WINDOWS = (2, 4, 8, 16)
POOL_GROUPS = len(POOL_WINDOWS)
POOL_GROUP_WIDTH = POOL_WIDTH // POOL_GROUPS
RET_WIDTH = 3 * D_MODEL // 8
RET_HEADS = RET_WIDTH // HEAD_DIM
RET_CHUNK = 128
NSA_WIDTH = D_MODEL - POOL_WIDTH - RET_WIDTH
NSA_HEADS = NSA_WIDTH // HEAD_DIM
NSA_KV_HEADS = 2
NSA_KV_WIDTH = NSA_KV_HEADS * HEAD_DIM
CMP_BLOCK = 32
CMP_STRIDE = 16
CMP_HIDDEN = 128
SLC_BLOCK = 64
SLC_TOP = 16
SLC_QCHUNK = 64
WINDOW = 512
WIN_QBLOCK = 128
N_BRANCH = 3
FORCE_SCORE = 1e6
MIX_WIDTH = POOL_WIDTH + RET_WIDTH + NSA_WIDTH
D_FF = 2816
CONV_WIDTH = 3
ROPE_THETA = 10000.0
LN_EPS = 1e-5
GN_EPS = 1e-5
DEEPNORM_ALPHA = (2 * DEPTH) ** 0.25
DEEPNORM_BETA = (8 * DEPTH) ** -0.25
IN_SPLITS = (POOL_WIDTH, RET_WIDTH, RET_WIDTH, RET_WIDTH, RET_WIDTH, NSA_WIDTH,
             NSA_KV_WIDTH, NSA_KV_WIDTH, NSA_KV_WIDTH, NSA_KV_WIDTH, NSA_KV_WIDTH, NSA_KV_WIDTH,
             NSA_HEADS * N_BRANCH)
IN_IS_VALUE = (True, False, False, True, False, False, False, True, False, True, False, True, False)
IN_WIDTH = sum(IN_SPLITS)

kernel_name = 'hybrid_pool_retention_nsa_deepnorm'


def _layernorm(x, g, b):
    xf = x.astype(jnp.float32)
    mu = jnp.mean(xf, axis=-1, keepdims=True)
    var = jnp.mean(jnp.square(xf - mu), axis=-1, keepdims=True)
    return ((xf - mu) * lax.rsqrt(var + LN_EPS) * g + b).astype(x.dtype)


def _rope(x, pos):
    dh = x.shape[-1]
    inv = ROPE_THETA ** (-jnp.arange(0, dh, 2, dtype=jnp.float32) / dh)
    ang = pos.astype(jnp.float32)[..., None] * inv
    cos = jnp.cos(ang)[:, :, None, :]
    sin = jnp.sin(ang)[:, :, None, :]
    xf = x.astype(jnp.float32)
    x1, x2 = xf[..., : dh // 2], xf[..., dh // 2:]
    return jnp.concatenate([x1 * cos - x2 * sin, x2 * cos + x1 * sin], axis=-1).astype(x.dtype)


def _masked_softmax(s, mask):
    s = jnp.where(mask, s, -1e30)
    p = jax.nn.softmax(s, axis=-1)
    return jnp.where(mask, p, 0.0)


def _pool_mixer(v, pool_w, pool_scale):
    b, t, _ = v.shape
    vg = v.reshape(b, t, POOL_GROUPS, POOL_GROUP_WIDTH).astype(jnp.float32)
    cs = jnp.pad(jnp.cumsum(vg, axis=1), ((0, 0), (1, 0), (0, 0), (0, 0)))
    tpos = jnp.arange(t)
    pooled = []
    for gi, w in enumerate(POOL_WINDOWS):
        lo = jnp.maximum(tpos + 1 - w, 0)
        cnt = (tpos + 1 - lo).astype(jnp.float32)[None, :, None]
        pooled.append((cs[:, 1:, gi] - cs[:, lo, gi]) / cnt)
    mixed = (jnp.stack(pooled, axis=2) - vg).astype(v.dtype)
    y = jnp.einsum('btgc,gcd->btgd', mixed, pool_w).reshape(b, t, POOL_WIDTH)
    return y * pool_scale


def _retention(q, k, v, g, gn_g, pos):
    b, t, _ = q.shape
    h, dh, c = RET_HEADS, HEAD_DIM, RET_CHUNK
    n = t // c
    f32 = jnp.float32
    qr = _rope(q.reshape(b, t, h, dh), pos).astype(f32)
    kr = _rope(k.reshape(b, t, h, dh), pos).astype(f32) * (dh ** -0.5)
    qc = qr.reshape(b, n, c, h, dh)
    kc = kr.reshape(b, n, c, h, dh)
    vc = v.reshape(b, n, c, h, dh).astype(f32)
    log_gamma = jnp.log1p(-jnp.power(2.0, -5.0 - jnp.arange(h, dtype=f32)))
    i = jnp.arange(c, dtype=f32)
    diff = i[:, None] - i[None, :]
    dmask = jnp.where(diff >= 0, jnp.exp(log_gamma[:, None, None] * jnp.maximum(diff, 0.0)), 0.0)
    xi = jnp.exp(log_gamma[:, None] * (i + 1.0))
    zeta = jnp.exp(log_gamma[:, None] * (c - 1.0 - i))
    gamma_c = jnp.exp(log_gamma * c)
    s = jnp.einsum('bnihd,bnjhd->bnhij', qc, kc) * dmask
    inner = jnp.einsum('bnhij,bnjhd->bnihd', s, vc)
    kv = jnp.einsum('bnjhd,bnjhe,hj->bnhde', kc, vc, zeta)

    def step(state, kv_i):
        return gamma_c[None, :, None, None] * state + kv_i, state

    _, r_prev = lax.scan(step, jnp.zeros((b, h, dh, dh), f32), jnp.moveaxis(kv, 1, 0))
    cross = jnp.einsum('bnihd,nbhde,hi->bnihe', qc, r_prev, xi)
    o = (inner + cross).reshape(b, t, h, dh)
    mu = jnp.mean(o, axis=-1, keepdims=True)
    var = jnp.mean(jnp.square(o - mu), axis=-1, keepdims=True)
    on = ((o - mu) * lax.rsqrt(var + GN_EPS)).reshape(b, t, RET_WIDTH) * gn_g
    return (jax.nn.silu(g.astype(f32)) * on).astype(q.dtype)


def _compress(kv, pos_emb, w1, b1, w2):
    b, t, _ = kv.shape
    n_cmp = (t - CMP_BLOCK) // CMP_STRIDE + 1
    blk_idx = (jnp.arange(n_cmp) * CMP_STRIDE)[:, None] + jnp.arange(CMP_BLOCK)[None, :]
    blocks = kv.reshape(b, t, NSA_KV_HEADS, HEAD_DIM)[:, blk_idx] + pos_emb[None, None, :, None, :]
    flat = jnp.moveaxis(blocks, 3, 2).reshape(b, n_cmp, NSA_KV_HEADS, CMP_BLOCK * HEAD_DIM)
    return jax.nn.gelu(flat @ w1 + b1) @ w2


def _nsa(q, k_cmp, v_cmp, k_slc, v_slc, k_win, v_win, gate, pos,
         cmp_pos_k, cmp_w1_k, cmp_b1_k, cmp_w2_k, cmp_pos_v, cmp_w1_v, cmp_b1_v, cmp_w2_v):
    b, t, _ = q.shape
    hq, gk, dh = NSA_HEADS, NSA_KV_HEADS, HEAD_DIM
    rep = hq // gk
    f32 = jnp.float32
    tq = jnp.arange(t)
    qg = (_rope(q.reshape(b, t, hq, dh), pos).astype(f32) * (dh ** -0.5)).reshape(b, t, gk, rep, dh)

    n_cmp = (t - CMP_BLOCK) // CMP_STRIDE + 1
    ends = jnp.arange(n_cmp) * CMP_STRIDE + CMP_BLOCK - 1
    kc = _rope(_compress(k_cmp, cmp_pos_k, cmp_w1_k, cmp_b1_k, cmp_w2_k), pos[:, ends]).astype(f32)
    vc = _compress(v_cmp, cmp_pos_v, cmp_w1_v, cmp_b1_v, cmp_w2_v).astype(f32)
    s_c = jnp.einsum('btgrd,bngd->bgrtn', qg, kc)
    p_c = _masked_softmax(s_c, ends[None, :] <= tq[:, None])
    o_cmp = jnp.einsum('bgrtn,bngd->btgrd', p_c, vc)

    n_slc = t // SLC_BLOCK
    ci = jnp.arange(n_cmp)[:, None]
    sj = jnp.arange(n_slc)[None, :]
    overlap = jnp.clip(jnp.minimum(ci * CMP_STRIDE + CMP_BLOCK, (sj + 1) * SLC_BLOCK)
                       - jnp.maximum(ci * CMP_STRIDE, sj * SLC_BLOCK), 0, None).astype(f32) / CMP_STRIDE
    imp = jnp.sum(p_c, axis=2) @ overlap
    blk = jnp.arange(n_slc)[None, :]
    cur = (tq // SLC_BLOCK)[:, None]
    forced = (blk == 0) | (blk == cur) | (blk == cur - 1)
    valid = blk * SLC_BLOCK <= tq[:, None]
    score = jnp.where(valid, jnp.where(forced, FORCE_SCORE, imp), -1.0)
    _, sel = lax.top_k(score, min(SLC_TOP, n_slc))

    ks = _rope(k_slc.reshape(b, t, gk, dh), pos)
    ksb = jnp.moveaxis(ks.reshape(b, n_slc, SLC_BLOCK, gk, dh), 3, 1)
    vsb = jnp.moveaxis(v_slc.reshape(b, n_slc, SLC_BLOCK, gk, dh), 3, 1)
    nq = t // SLC_QCHUNK
    q_ch = jnp.moveaxis(qg.reshape(b, nq, SLC_QCHUNK, gk, rep, dh), 1, 0)
    sel_ch = jnp.moveaxis(jnp.transpose(sel, (0, 2, 1, 3)).reshape(b, nq, SLC_QCHUNK, gk, -1), 1, 0)
    t_ch = tq.reshape(nq, SLC_QCHUNK)
    bi = jnp.arange(b)[:, None, None, None]
    gi = jnp.arange(gk)[None, None, :, None]

    def slc_chunk(args):
        qx, sx, tx = args
        kx = ksb[bi, gi, sx].astype(f32)
        vx = vsb[bi, gi, sx].astype(f32)
        qc_ = qx.shape[1]
        kpos = sx[..., None] * SLC_BLOCK + jnp.arange(SLC_BLOCK)
        mask = jnp.transpose((kpos <= tx[None, :, None, None, None]).reshape(b, qc_, gk, -1), (0, 2, 1, 3))[:, :, None]
        s = jnp.einsum('bqgrd,bqgkld->bgrqkl', qx, kx).reshape(b, gk, rep, qc_, -1)
        p = _masked_softmax(s, mask)
        return jnp.einsum('bgrqm,bqgmd->bqgrd', p, vx.reshape(b, qc_, gk, -1, dh))

    o_slc = jnp.moveaxis(lax.map(slc_chunk, (q_ch, sel_ch, t_ch)), 0, 1).reshape(b, t, gk, rep, dh)

    nw = t // WIN_QBLOCK
    span = WINDOW + WIN_QBLOCK
    kw = _rope(k_win.reshape(b, t, gk, dh), pos)
    kp = jnp.pad(kw, ((0, 0), (WINDOW, 0), (0, 0), (0, 0)))
    vp = jnp.pad(v_win.reshape(b, t, gk, dh), ((0, 0), (WINDOW, 0), (0, 0), (0, 0)))
    qw_ch = jnp.moveaxis(qg.reshape(b, nw, WIN_QBLOCK, gk, rep, dh), 1, 0)

    def win_block(args):
        i, qx = args
        kx = lax.dynamic_slice_in_dim(kp, i * WIN_QBLOCK, span, axis=1).astype(f32)
        vx = lax.dynamic_slice_in_dim(vp, i * WIN_QBLOCK, span, axis=1).astype(f32)
        qpos = i * WIN_QBLOCK + jnp.arange(WIN_QBLOCK)
        kpos = i * WIN_QBLOCK - WINDOW + jnp.arange(span)
        d = qpos[:, None] - kpos[None, :]
        mask = (d >= 0) & (d < WINDOW) & (kpos[None, :] >= 0)
        s = jnp.einsum('bqgrd,bkgd->bgrqk', qx, kx)
        p = _masked_softmax(s, mask)
        return jnp.einsum('bgrqk,bkgd->bqgrd', p, vx)

    o_win = jnp.moveaxis(lax.map(win_block, (jnp.arange(nw), qw_ch)), 0, 1).reshape(b, t, gk, rep, dh)

    gts = jax.nn.sigmoid(gate.astype(f32)).reshape(b, t, gk, rep, N_BRANCH)
    o = gts[..., 0:1] * o_cmp + gts[..., 1:2] * o_slc + gts[..., 2:3] * o_win
    return o.reshape(b, t, NSA_WIDTH).astype(q.dtype)


def _conv_ffn(x, w_gate, w_up, conv_w, conv_b, w_down):
    t = x.shape[1]
    hg = x @ w_gate
    hp = jnp.pad(hg, ((0, 0), (CONV_WIDTH - 1, 0), (0, 0)))
    hc = conv_b
    for kk in range(CONV_WIDTH):
        hc = hc + hp[:, kk:kk + t] * conv_w[kk]
    return (jax.nn.gelu(hc) * (x @ w_up)) @ w_down


def setup_inputs(seed: int = 0) -> dict:
    key = jax.random.key(seed)
    ks = jax.random.split(key, 26)
    nrm = jax.random.normal
    col_scale = jnp.asarray(np.concatenate(
        [np.full((s,), DEEPNORM_BETA if isv else 1.0, np.float32) for s, isv in zip(IN_SPLITS, IN_IS_VALUE)]))
    cf = CMP_BLOCK * HEAD_DIM
    return {
        'x': nrm(ks[0], (BATCH, SEQ, D_MODEL), jnp.float32),
        'positions': (jnp.arange(SEQ, dtype=jnp.int32)[None, :]
                      + jax.random.randint(ks[1], (BATCH, 1), 0, 512, dtype=jnp.int32)),
        'w_in': nrm(ks[2], (DEPTH, D_MODEL, IN_WIDTH), jnp.float32) * (D_MODEL ** -0.5) * col_scale,
        'w_out': nrm(ks[3], (DEPTH, MIX_WIDTH, D_MODEL), jnp.float32) * (MIX_WIDTH ** -0.5) * DEEPNORM_BETA,
        'pool_w': nrm(ks[4], (DEPTH, POOL_GROUPS, POOL_GROUP_WIDTH, POOL_GROUP_WIDTH), jnp.float32) * (POOL_GROUP_WIDTH ** -0.5),
        'pool_scale': 1.0 + 0.1 * nrm(ks[5], (DEPTH, POOL_WIDTH), jnp.float32),
        'ret_gn_g': 1.0 + 0.1 * nrm(ks[6], (DEPTH, RET_WIDTH), jnp.float32),
        'cmp_pos_k': 0.1 * nrm(ks[7], (DEPTH, CMP_BLOCK, HEAD_DIM), jnp.float32),
        'cmp_w1_k': nrm(ks[8], (DEPTH, cf, CMP_HIDDEN), jnp.float32) * (cf ** -0.5),
        'cmp_b1_k': 0.01 * nrm(ks[9], (DEPTH, CMP_HIDDEN), jnp.float32),
        'cmp_w2_k': nrm(ks[10], (DEPTH, CMP_HIDDEN, HEAD_DIM), jnp.float32) * (CMP_HIDDEN ** -0.5),
        'cmp_pos_v': 0.1 * nrm(ks[11], (DEPTH, CMP_BLOCK, HEAD_DIM), jnp.float32),
        'cmp_w1_v': nrm(ks[12], (DEPTH, cf, CMP_HIDDEN), jnp.float32) * (cf ** -0.5),
        'cmp_b1_v': 0.01 * nrm(ks[13], (DEPTH, CMP_HIDDEN), jnp.float32),
        'cmp_w2_v': nrm(ks[14], (DEPTH, CMP_HIDDEN, HEAD_DIM), jnp.float32) * (CMP_HIDDEN ** -0.5),
        'ffn_w_gate': nrm(ks[15], (DEPTH, D_MODEL, D_FF), jnp.float32) * (D_MODEL ** -0.5),
        'ffn_w_up': nrm(ks[16], (DEPTH, D_MODEL, D_FF), jnp.float32) * (D_MODEL ** -0.5),
        'ffn_conv_w': nrm(ks[17], (DEPTH, CONV_WIDTH, D_FF), jnp.float32) * (CONV_WIDTH ** -0.5),
        'ffn_conv_b': 0.01 * nrm(ks[18], (DEPTH, D_FF), jnp.float32),
        'ffn_w_down': nrm(ks[19], (DEPTH, D_FF, D_MODEL), jnp.float32) * (D_FF ** -0.5) * DEEPNORM_BETA,
        'ln1_g': 1.0 + 0.05 * nrm(ks[20], (DEPTH, D_MODEL), jnp.float32),
        'ln1_b': 0.01 * nrm(ks[21], (DEPTH, D_MODEL), jnp.float32),
        'ln2_g': 1.0 + 0.05 * nrm(ks[22], (DEPTH, D_MODEL), jnp.float32),
        'ln2_b': 0.01 * nrm(ks[23], (DEPTH, D_MODEL), jnp.float32),
    }


def reference(x, positions, w_in, w_out, pool_w, pool_scale, ret_gn_g,
              cmp_pos_k, cmp_w1_k, cmp_b1_k, cmp_w2_k, cmp_pos_v, cmp_w1_v, cmp_b1_v, cmp_w2_v,
              ffn_w_gate, ffn_w_up, ffn_conv_w, ffn_conv_b, ffn_w_down,
              ln1_g, ln1_b, ln2_g, ln2_b):
    split_at = [int(s) for s in np.cumsum(IN_SPLITS)[:-1]]
    for l in range(DEPTH):
        h = x @ w_in[l]
        (v_pool, q_ret, k_ret, v_ret, g_ret, q_nsa, k_cmp, v_cmp,
         k_slc, v_slc, k_win, v_win, gate_nsa) = jnp.split(h, split_at, axis=-1)
        y_a = _pool_mixer(v_pool, pool_w[l], pool_scale[l])
        y_b = _retention(q_ret, k_ret, v_ret, g_ret, ret_gn_g[l], positions)
        y_c = _nsa(q_nsa, k_cmp, v_cmp, k_slc, v_slc, k_win, v_win, gate_nsa, positions,
                   cmp_pos_k[l], cmp_w1_k[l], cmp_b1_k[l], cmp_w2_k[l],
                   cmp_pos_v[l], cmp_w1_v[l], cmp_b1_v[l], cmp_w2_v[l])
        mix = jnp.concatenate([y_a, y_b, y_c], axis=-1) @ w_out[l]
        x = _layernorm(DEEPNORM_ALPHA * x + mix, ln1_g[l], ln1_b[l])
        f = _conv_ffn(x, ffn_w_gate[l], ffn_w_up[l], ffn_conv_w[l], ffn_conv_b[l], ffn_w_down[l])
        x = _layernorm(DEEPNORM_ALPHA * x + f, ln2_g[l], ln2_b[l])
    return x
```

```python
import functools
import math

import numpy as np
import jax
import jax.numpy as jnp
from jax import lax
from jax.experimental import pallas as pl
from jax.experimental.pallas import tpu as pltpu

F32 = jnp.float32
BF16 = jnp.bfloat16

HEAD_DIM = 64
POOL_WINDOWS = (2, 4, 8, 16)
POOL_GROUP_WIDTH = 64
POOL_WIDTH = POOL_GROUP_WIDTH * len(POOL_WINDOWS)
POOL_HALO = 16
RET_HEADS = 6
RET_WIDTH = RET_HEADS * HEAD_DIM
RET_CHUNK = 128
NSA_HEADS = 6
NSA_WIDTH = NSA_HEADS * HEAD_DIM
NSA_KV_HEADS = 2
NSA_REP = NSA_HEADS // NSA_KV_HEADS
NSA_KV_WIDTH = NSA_KV_HEADS * HEAD_DIM
CMP_BLOCK = 32
CMP_STRIDE = 16
CMP_HIDDEN = 128
SLC_BLOCK = 64
SLC_TOP = 16
WINDOW = 512
N_BRANCH = 3
FORCE_SCORE = 1e6
CONV_WIDTH = 3
CONV_HALO = 16
ROPE_THETA = 10000.0
LN_EPS = 1e-5
GN_EPS = 1e-5
NEG_INF = -1e30

LANES = 128
VMEM_LIMIT = 56 * 1024 * 1024

OFF_POOL = 0
OFF_RET = OFF_POOL + POOL_WIDTH
OFF_QNSA = OFF_RET + 4 * RET_WIDTH
OFF_KCMP = OFF_QNSA + NSA_WIDTH
OFF_VCMP = OFF_KCMP + NSA_KV_WIDTH
OFF_KV4 = OFF_VCMP + NSA_KV_WIDTH
OFF_GATE = OFF_KV4 + 4 * NSA_KV_WIDTH
IN_WIDTH = OFF_GATE + NSA_HEADS * N_BRANCH
IN_PAD = OFF_GATE + LANES


def _cparams(sem):
    return pltpu.CompilerParams(dimension_semantics=sem, vmem_limit_bytes=VMEM_LIMIT)


def _rope(x, cos, sin_signed):
    lane = lax.broadcasted_iota(jnp.int32, x.shape, 1)
    first = (lane & (HEAD_DIM - 1)) < HEAD_DIM // 2
    partner = jnp.where(first, pltpu.roll(x, LANES - HEAD_DIM // 2, 1), pltpu.roll(x, HEAD_DIM // 2, 1))
    return x * cos + partner * sin_signed


def _gelu_tanh(x):
    return 0.5 * x * (1.0 + jnp.tanh(math.sqrt(2.0 / math.pi) * (x + 0.044715 * (x * x * x))))


def _sigmoid(x):
    return 1.0 / (1.0 + jnp.exp(-x))


def _layernorm(y, g, b):
    mu = jnp.mean(y, axis=-1, keepdims=True)
    d = y - mu
    var = jnp.mean(d * d, axis=-1, keepdims=True)
    return d * lax.rsqrt(var + LN_EPS) * g + b


def _dot_nt(a, b):
    return lax.dot_general(a, b, (((1,), (1,)), ((), ())), preferred_element_type=F32)


def _rope_table_kernel(pos_ref, inv_ref, cos_ref, sin_ref):
    ang = pos_ref[0].astype(F32) * inv_ref[...]
    lane = lax.broadcasted_iota(jnp.int32, ang.shape, 1)
    first = (lane & (HEAD_DIM - 1)) < HEAD_DIM // 2
    s = jnp.sin(ang)
    cos_ref[0] = jnp.cos(ang)
    sin_ref[0] = jnp.where(first, -s, s)


def _rope_tables(pos, inv):
    b, n = pos.shape
    tt = min(n, 512)
    shp = jax.ShapeDtypeStruct((b, n, LANES), F32)
    return pl.pallas_call(
        _rope_table_kernel,
        grid=(b, n // tt),
        in_specs=[pl.BlockSpec((1, tt, 1), lambda i, j: (i, j, 0)),
                  pl.BlockSpec((1, LANES), lambda i, j: (0, 0))],
        out_specs=[pl.BlockSpec((1, tt, LANES), lambda i, j: (i, j, 0))] * 2,
        out_shape=[shp, shp],
        compiler_params=_cparams(("parallel", "parallel")),
        name="rope_tables",
    )(pos.reshape(b, n, 1), inv)


_IN_OUTS = ((OFF_POOL, POOL_WIDTH), (OFF_RET, 4 * RET_WIDTH), (OFF_QNSA, NSA_WIDTH),
            (OFF_KCMP, NSA_KV_WIDTH), (OFF_VCMP, NSA_KV_WIDTH), (OFF_KV4, 4 * NSA_KV_WIDTH),
            (OFF_GATE, LANES))


def _inproj_kernel(x_ref, w_ref, *o_refs):
    xb = x_ref[...].astype(BF16)
    for o_ref, (off, width) in zip(o_refs, _IN_OUTS):
        step = 2 * LANES if width % (2 * LANES) == 0 else LANES
        for c in range(0, width, step):
            o_ref[:, c:c + step] = jnp.dot(xb, w_ref[:, off + c:off + c + step], preferred_element_type=F32)


def _inproj(x2, w):
    n, d = x2.shape
    tm = 512
    return pl.pallas_call(
        _inproj_kernel,
        grid=(n // tm,),
        in_specs=[pl.BlockSpec((tm, d), lambda i: (i, 0)),
                  pl.BlockSpec((d, IN_PAD), lambda i: (0, 0))],
        out_specs=[pl.BlockSpec((tm, wd), lambda i: (i, 0)) for _, wd in _IN_OUTS],
        out_shape=[jax.ShapeDtypeStruct((n, wd), F32) for _, wd in _IN_OUTS],
        compiler_params=_cparams(("parallel",)),
        name="in_proj",
    )(x2, w)


def _pool_kernel(cur_ref, prev_ref, w_ref, scale_ref, o_ref, *, tt):
    i = pl.program_id(1)
    cur = cur_ref[0]
    prev = jnp.where(i > 0, prev_ref[0], 0.0)
    cat = jnp.concatenate([prev, cur], axis=0)
    s2 = cat + pltpu.roll(cat, 1, 0)
    s4 = s2 + pltpu.roll(s2, 2, 0)
    s8 = s4 + pltpu.roll(s4, 4, 0)
    s16 = s8 + pltpu.roll(s8, 8, 0)
    grp = lax.broadcasted_iota(jnp.int32, cur.shape, 1) // POOL_GROUP_WIDTH
    t = i * tt + lax.broadcasted_iota(jnp.int32, cur.shape, 0)

    def pick(a, b, c, d):
        return jnp.where(grp == 0, a, jnp.where(grp == 1, b, jnp.where(grp == 2, c, d)))

    h = POOL_HALO
    wsum = pick(s2[h:], s4[h:], s8[h:], s16[h:])
    width = pick(*[float(w) for w in POOL_WINDOWS])
    cnt = jnp.minimum((t + 1).astype(F32), width)
    mixed = wsum / cnt - cur
    y = jnp.dot(mixed.astype(BF16), w_ref[...], preferred_element_type=F32)
    o_ref[0] = y * scale_ref[...]


def _pool(v, wbd, scale):
    b, t, c = v.shape
    tt = min(t, 512)
    r = tt // POOL_HALO
    return pl.pallas_call(
        functools.partial(_pool_kernel, tt=tt),
        grid=(b, t // tt),
        in_specs=[pl.BlockSpec((1, tt, c), lambda i, j: (i, j, 0)),
                  pl.BlockSpec((1, POOL_HALO, c), lambda i, j: (i, jnp.maximum(j * r - 1, 0), 0)),
                  pl.BlockSpec((c, c), lambda i, j: (0, 0)),
                  pl.BlockSpec((1, c), lambda i, j: (0, 0))],
        out_specs=pl.BlockSpec((1, tt, c), lambda i, j: (i, j, 0)),
        out_shape=jax.ShapeDtypeStruct((b, t, c), F32),
        compiler_params=_cparams(("parallel", "parallel")),
        name="pool_mixer",
    )(v, v, wbd, scale)


def _ret_consts():
    h, c = RET_HEADS, RET_CHUNK
    lg = np.log1p(-np.power(2.0, -5.0 - np.arange(h, dtype=np.float64)))
    i = np.arange(c, dtype=np.float64)
    diff = i[:, None] - i[None, :]
    dmask = np.where(diff >= 0, np.exp(lg[:, None, None] * np.maximum(diff, 0.0)), 0.0)
    xi = np.repeat(np.exp(lg[:, None] * (i + 1.0)).T, HEAD_DIM, axis=1)
    zeta = np.repeat(np.exp(lg[:, None] * (c - 1.0 - i)).T, HEAD_DIM, axis=1)
    gc = np.repeat(np.exp(lg * c), HEAD_DIM)[None, :]
    return (jnp.asarray(dmask, F32), jnp.asarray(xi, F32), jnp.asarray(zeta, F32), jnp.asarray(gc, F32))


def _ret_kernel(h_ref, cos_ref, sin_ref, dmask_ref, xi_ref, zeta_ref, gc_ref, gn_ref, o_ref, state_ref):
    @pl.when(pl.program_id(1) == 0)
    def _():
        state_ref[...] = jnp.zeros_like(state_ref)

    cos = cos_ref[0]
    sin = sin_ref[0]
    c = RET_CHUNK
    lane = lax.broadcasted_iota(jnp.int32, (c, LANES), 1)
    row = lax.broadcasted_iota(jnp.int32, (LANES, LANES), 0)
    col = lax.broadcasted_iota(jnp.int32, (LANES, LANES), 1)
    lo = lane < HEAD_DIM
    same_head = (row < HEAD_DIM) == (col < HEAD_DIM)
    for j in range(RET_WIDTH // LANES):
        sl = slice(j * LANES, (j + 1) * LANES)
        q = _rope(h_ref[0, :, j * LANES:(j + 1) * LANES], cos, sin)
        k = _rope(h_ref[0, :, RET_WIDTH + j * LANES:RET_WIDTH + (j + 1) * LANES], cos, sin) * (HEAD_DIM ** -0.5)
        v = h_ref[0, :, 2 * RET_WIDTH + j * LANES:2 * RET_WIDTH + (j + 1) * LANES]
        g = h_ref[0, :, 3 * RET_WIDTH + j * LANES:3 * RET_WIDTH + (j + 1) * LANES]
        kb = k.astype(BF16)
        vb = v.astype(BF16)
        state = state_ref[j]
        o = jnp.dot(q.astype(BF16), state.astype(BF16), preferred_element_type=F32) * xi_ref[:, sl]
        for hh in range(2):
            m = lo if hh == 0 else jnp.logical_not(lo)
            qm = jnp.where(m, q, 0.0).astype(BF16)
            s = _dot_nt(qm, kb) * dmask_ref[2 * j + hh]
            pv = jnp.dot(s.astype(BF16), vb, preferred_element_type=F32)
            o = o + jnp.where(m, pv, 0.0)
        kz = (k * zeta_ref[:, sl]).astype(BF16)
        kv = lax.dot_general(kz, vb, (((0,), (0,)), ((), ())), preferred_element_type=F32)
        state_ref[j] = gc_ref[:, sl] * state + jnp.where(same_head, kv, 0.0)
        s_lo = jnp.sum(jnp.where(lo, o, 0.0), axis=-1, keepdims=True)
        s_hi = jnp.sum(jnp.where(lo, 0.0, o), axis=-1, keepdims=True)
        d = o - jnp.where(lo, s_lo, s_hi) * (1.0 / HEAD_DIM)
        d2 = d * d
        v_lo = jnp.sum(jnp.where(lo, d2, 0.0), axis=-1, keepdims=True)
        v_hi = jnp.sum(jnp.where(lo, 0.0, d2), axis=-1, keepdims=True)
        var = jnp.where(lo, v_lo, v_hi) * (1.0 / HEAD_DIM)
        on = d * lax.rsqrt(var + GN_EPS) * gn_ref[:, sl]
        o_ref[0, :, sl] = g * _sigmoid(g) * on


def _retention(hret, cos, sin, gn_g):
    b, t, _ = hret.shape
    c = RET_CHUNK
    dmask, xi, zeta, gc = _ret_consts()
    const = lambda shape: pl.BlockSpec(shape, lambda i, j: (0,) * len(shape))
    return pl.pallas_call(
        _ret_kernel,
        grid=(b, t // c),
        in_specs=[pl.BlockSpec((1, c, 4 * RET_WIDTH), lambda i, j: (i, j, 0)),
                  pl.BlockSpec((1, c, LANES), lambda i, j: (i, j, 0)),
                  pl.BlockSpec((1, c, LANES), lambda i, j: (i, j, 0)),
                  const((RET_HEADS, c, c)), const((c, RET_WIDTH)), const((c, RET_WIDTH)),
                  const((1, RET_WIDTH)), const((1, RET_WIDTH))],
        out_specs=pl.BlockSpec((1, c, RET_WIDTH), lambda i, j: (i, j, 0)),
        out_shape=jax.ShapeDtypeStruct((b, t, RET_WIDTH), F32),
        scratch_shapes=[pltpu.VMEM((RET_WIDTH // LANES, LANES, LANES), F32)],
        compiler_params=_cparams(("parallel", "arbitrary")),
        name="retention",
    )(hret, cos, sin, dmask, xi, zeta, gc, gn_g)


def _compress_kernel(k_ref, v_ref, ptk, pbk, wtk, wbk, b1k, w2k, ptv, pbv, wtv, wbv, b1v, w2v,
                     cos_ref, sin_ref, kc_ref, vc_ref):
    def comp(x_ref, pt, pb, wt, wb, b1, w2):
        x = x_ref[0]
        top = jnp.dot((x + pt[...]).astype(BF16), wt[...], preferred_element_type=F32)
        bot = jnp.dot((x + pb[...]).astype(BF16), wb[...], preferred_element_type=F32)
        rows = bot.shape[0]
        pre = top + pltpu.roll(bot, rows - 1, 0) + b1[...]
        return jnp.dot(_gelu_tanh(pre).astype(BF16), w2[...], preferred_element_type=F32)

    kc = comp(k_ref, ptk, pbk, wtk, wbk, b1k, w2k)
    kc_ref[0] = _rope(kc, cos_ref[0], sin_ref[0])
    vc_ref[0] = comp(v_ref, ptv, pbv, wtv, wbv, b1v, w2v)


def _compress_weights(pos, w1, b1, w2):
    half = CMP_BLOCK // 2
    g = NSA_KV_HEADS
    w1r = w1.reshape(CMP_BLOCK, HEAD_DIM, CMP_HIDDEN)
    eye = jnp.eye(g, dtype=F32)

    def lay(wh):
        return jnp.einsum('ldh,ge->lgdeh', wh, eye).reshape(half * g * HEAD_DIM, g * CMP_HIDDEN)

    def tile_pos(p):
        return jnp.broadcast_to(p[:, None, :], (half, g, HEAD_DIM)).reshape(1, half * g * HEAD_DIM)

    w2bd = jnp.einsum('hd,ge->ghed', w2, eye).reshape(g * CMP_HIDDEN, g * HEAD_DIM)
    return (tile_pos(pos[:half]), tile_pos(pos[half:]), lay(w1r[:half]).astype(BF16), lay(w1r[half:]).astype(BF16),
            jnp.tile(b1, g)[None, :], w2bd.astype(BF16))


def _compress(kc_in, vc_in, wk, wv, cos_e, sin_e):
    b, nc, width = kc_in.shape
    const = lambda a: pl.BlockSpec(a.shape, lambda i: (0,) * a.ndim)
    shp = jax.ShapeDtypeStruct((b, nc, NSA_KV_WIDTH), F32)
    blk = lambda w: pl.BlockSpec((1, nc, w), lambda i: (i, 0, 0))
    return pl.pallas_call(
        _compress_kernel,
        grid=(b,),
        in_specs=[blk(width), blk(width)] + [const(a) for a in wk] + [const(a) for a in wv]
                 + [blk(LANES), blk(LANES)],
        out_specs=[blk(NSA_KV_WIDTH), blk(NSA_KV_WIDTH)],
        out_shape=[shp, shp],
        compiler_params=_cparams(("parallel",)),
        name="nsa_compress",
    )(kc_in, vc_in, *wk, *wv, cos_e, sin_e)


def _krope_kernel(kv_ref, cos_ref, sin_ref, ks_ref, vs_ref, kw_ref, vw_ref):
    cos = cos_ref[0]
    sin = sin_ref[0]
    w = NSA_KV_WIDTH
    ks_ref[0] = _rope(kv_ref[0, :, 0:w], cos, sin).astype(BF16)
    vs_ref[0] = kv_ref[0, :, w:2 * w].astype(BF16)
    kw_ref[0] = _rope(kv_ref[0, :, 2 * w:3 * w], cos, sin).astype(BF16)
    vw_ref[0] = kv_ref[0, :, 3 * w:4 * w].astype(BF16)


def _krope(kv4, cos, sin):
    b, t, _ = kv4.shape
    tt = min(t, 512)
    shp = jax.ShapeDtypeStruct((b, t, NSA_KV_WIDTH), BF16)
    blk = lambda w: pl.BlockSpec((1, tt, w), lambda i, j: (i, j, 0))
    return pl.pallas_call(
        _krope_kernel,
        grid=(b, t // tt),
        in_specs=[blk(4 * NSA_KV_WIDTH), blk(LANES), blk(LANES)],
        out_specs=[blk(NSA_KV_WIDTH)] * 4,
        out_shape=[shp] * 4,
        compiler_params=_cparams(("parallel", "parallel")),
        name="nsa_kv_rope",
    )(kv4, cos, sin)


def _cmpattn_kernel(q_ref, cos_ref, sin_ref, kc_ref, vc_ref, ovt_ref, qo_ref, o_ref, sel_ref, *, tq, n_slc):
    i = pl.program_id(1)
    cos = cos_ref[0]
    sin = sin_ref[0]
    scale = HEAD_DIM ** -0.5
    qr = jnp.concatenate([_rope(q_ref[0, :, j * LANES:(j + 1) * LANES], cos, sin) * scale
                          for j in range(NSA_WIDTH // LANES)], axis=1)
    qo_ref[0] = qr
    kc = kc_ref[0].astype(BF16)
    vc = vc_ref[0].astype(BF16)
    nc = kc.shape[0]
    t = i * tq + lax.broadcasted_iota(jnp.int32, (tq, nc), 0)
    n = lax.broadcasted_iota(jnp.int32, (tq, nc), 1)
    visible = n * CMP_STRIDE + (CMP_BLOCK - 1) <= t
    blk = lax.broadcasted_iota(jnp.int32, (LANES, tq), 0)
    tl = i * tq + lax.broadcasted_iota(jnp.int32, (LANES, tq), 1)
    cur = tl // SLC_BLOCK
    forced = (blk == 0) | (blk == cur) | (blk == cur - 1)
    valid = (blk * SLC_BLOCK <= tl) & (blk < n_slc)
    outs = []
    for g in range(NSA_KV_HEADS):
        kg = kc[:, g * HEAD_DIM:(g + 1) * HEAD_DIM]
        vg = vc[:, g * HEAD_DIM:(g + 1) * HEAD_DIM]
        psum = jnp.zeros((tq, nc), F32)
        for r in range(NSA_REP):
            hq = g * NSA_REP + r
            qh = qr[:, hq * HEAD_DIM:(hq + 1) * HEAD_DIM].astype(BF16)
            s = jnp.where(visible, _dot_nt(qh, kg), NEG_INF)
            e = jnp.exp(s - jnp.max(s, axis=-1, keepdims=True))
            p = jnp.where(visible, e / jnp.sum(e, axis=-1, keepdims=True), 0.0)
            psum = psum + p
            outs.append(jnp.dot(p.astype(BF16), vg, preferred_element_type=F32))
        imp_t = lax.dot_general(ovt_ref[...], psum, (((1,), (1,)), ((), ())),
                                precision=lax.Precision.HIGHEST, preferred_element_type=F32)
        score = jnp.where(valid, jnp.where(forced, FORCE_SCORE, imp_t), -1.0)
        rank = jnp.zeros((LANES, tq), F32)
        for jp in range(n_slc):
            rowv = score[jp:jp + 1, :]
            ahead = jnp.where(rowv > score, 1.0, jnp.where((rowv == score) & (blk > jp), 1.0, 0.0))
            rank = rank + ahead
        chosen = jnp.where((rank < float(min(SLC_TOP, n_slc))) & (blk < n_slc), 1.0, 0.0)
        sel_ref[0, g] = jnp.transpose(chosen)
    o_ref[0] = jnp.concatenate(outs, axis=1)


def _cmpattn(q, cos, sin, kc, vc, ovt):
    b, t, _ = q.shape
    nc = kc.shape[1]
    tq = 128
    n_slc = t // SLC_BLOCK
    blk = lambda w: pl.BlockSpec((1, tq, w), lambda i, j: (i, j, 0))
    full = lambda w: pl.BlockSpec((1, nc, w), lambda i, j: (i, 0, 0))
    shp = jax.ShapeDtypeStruct((b, t, NSA_WIDTH), F32)
    return pl.pallas_call(
        functools.partial(_cmpattn_kernel, tq=tq, n_slc=n_slc),
        grid=(b, t // tq),
        in_specs=[blk(NSA_WIDTH), blk(LANES), blk(LANES), full(NSA_KV_WIDTH), full(NSA_KV_WIDTH),
                  pl.BlockSpec((LANES, nc), lambda i, j: (0, 0))],
        out_specs=[blk(NSA_WIDTH), blk(NSA_WIDTH),
                   pl.BlockSpec((1, NSA_KV_HEADS, tq, LANES), lambda i, j: (i, 0, j, 0))],
        out_shape=[shp, shp, jax.ShapeDtypeStruct((b, NSA_KV_HEADS, t, LANES), F32)],
        compiler_params=_cparams(("parallel", "parallel")),
        name="nsa_compressed_attn_select",
    )(q, cos, sin, kc, vc, ovt)


def _slc_kernel(q_ref, sel_ref, ks_ref, vs_ref, e_ref, o_ref, *, tq, kc):
    i = pl.program_id(1)
    n_chunks = (i * tq + tq + kc - 1) // kc
    tpos = i * tq + lax.broadcasted_iota(jnp.int32, (tq, kc), 0)
    kio = lax.broadcasted_iota(jnp.int32, (tq, kc), 1)
    outs = []
    for g in range(NSA_KV_HEADS):
        q3 = jnp.concatenate([q_ref[0, :, (g * NSA_REP + r) * HEAD_DIM:(g * NSA_REP + r + 1) * HEAD_DIM]
                              for r in range(NSA_REP)], axis=0).astype(BF16)
        selg = sel_ref[0, g].astype(BF16)

        def body(c, carry, g=g, q3=q3, selg=selg):
            m, l, acc = carry
            start = pl.multiple_of(c * kc, kc)
            k = ks_ref[0, pl.ds(start, kc), :][:, g * HEAD_DIM:(g + 1) * HEAD_DIM]
            v = vs_ref[0, pl.ds(start, kc), :][:, g * HEAD_DIM:(g + 1) * HEAD_DIM]
            s = _dot_nt(q3, k)
            picked = jnp.dot(selg, e_ref[c], preferred_element_type=F32)
            keep = (picked > 0.5) & (c * kc + kio <= tpos)
            keep3 = jnp.concatenate([keep] * NSA_REP, axis=0)
            s = jnp.where(keep3, s, NEG_INF)
            m_new = jnp.maximum(m, jnp.max(s, axis=-1, keepdims=True))
            a = jnp.exp(m - m_new)
            p = jnp.where(keep3, jnp.exp(s - m_new), 0.0)
            l = a * l + jnp.sum(p, axis=-1, keepdims=True)
            acc = a * acc + jnp.dot(p.astype(BF16), v, preferred_element_type=F32)
            return m_new, l, acc

        m0 = jnp.full((NSA_REP * tq, 1), NEG_INF, F32)
        l0 = jnp.zeros((NSA_REP * tq, 1), F32)
        a0 = jnp.zeros((NSA_REP * tq, HEAD_DIM), F32)
        _, l, acc = lax.fori_loop(0, n_chunks, body, (m0, l0, a0))
        o = acc / l
        outs.extend(o[r * tq:(r + 1) * tq] for r in range(NSA_REP))
    o_ref[0] = jnp.concatenate(outs, axis=1)


def _expand_table(t, kc):
    key = np.arange(t).reshape(t // kc, 1, kc)
    j = np.arange(LANES).reshape(1, LANES, 1)
    return jnp.asarray((key // SLC_BLOCK == j).astype(np.float32), BF16)


def _slc(q, sel, ks, vs):
    b, t, _ = q.shape
    tq = 128
    kc = min(t, 256)
    e = _expand_table(t, kc)
    return pl.pallas_call(
        functools.partial(_slc_kernel, tq=tq, kc=kc),
        grid=(b, t // tq),
        in_specs=[pl.BlockSpec((1, tq, NSA_WIDTH), lambda i, j: (i, j, 0)),
                  pl.BlockSpec((1, NSA_KV_HEADS, tq, LANES), lambda i, j: (i, 0, j, 0)),
                  pl.BlockSpec((1, t, NSA_KV_WIDTH), lambda i, j: (i, 0, 0)),
                  pl.BlockSpec((1, t, NSA_KV_WIDTH), lambda i, j: (i, 0, 0)),
                  pl.BlockSpec((t // kc, LANES, kc), lambda i, j: (0, 0, 0))],
        out_specs=pl.BlockSpec((1, tq, NSA_WIDTH), lambda i, j: (i, j, 0)),
        out_shape=jax.ShapeDtypeStruct((b, t, NSA_WIDTH), F32),
        compiler_params=_cparams(("parallel", "parallel")),
        name="nsa_selected_attn",
    )(q, sel, ks, vs, e)


def _win_kernel(q_ref, kw_ref, vw_ref, o_ref, *, tq, span, t_total):
    i = pl.program_id(1)
    start = pl.multiple_of(jnp.clip(i * tq - WINDOW, 0, t_total - span), tq)
    k = kw_ref[0, pl.ds(start, span), :]
    v = vw_ref[0, pl.ds(start, span), :]
    d = (i * tq + lax.broadcasted_iota(jnp.int32, (tq, span), 0)) - (start + lax.broadcasted_iota(jnp.int32, (tq, span), 1))
    keep = (d >= 0) & (d < WINDOW)
    outs = []
    for hq in range(NSA_HEADS):
        g = hq // NSA_REP
        qh = q_ref[0, :, hq * HEAD_DIM:(hq + 1) * HEAD_DIM].astype(BF16)
        s = jnp.where(keep, _dot_nt(qh, k[:, g * HEAD_DIM:(g + 1) * HEAD_DIM]), NEG_INF)
        e = jnp.where(keep, jnp.exp(s - jnp.max(s, axis=-1, keepdims=True)), 0.0)
        p = e / jnp.sum(e, axis=-1, keepdims=True)
        outs.append(jnp.dot(p.astype(BF16), v[:, g * HEAD_DIM:(g + 1) * HEAD_DIM], preferred_element_type=F32))
    o_ref[0] = jnp.concatenate(outs, axis=1)


def _win(q, kw, vw):
    b, t, _ = q.shape
    tq = 128
    span = min(t, WINDOW + tq)
    return pl.pallas_call(
        functools.partial(_win_kernel, tq=tq, span=span, t_total=t),
        grid=(b, t // tq),
        in_specs=[pl.BlockSpec((1, tq, NSA_WIDTH), lambda i, j: (i, j, 0)),
                  pl.BlockSpec((1, t, NSA_KV_WIDTH), lambda i, j: (i, 0, 0)),
                  pl.BlockSpec((1, t, NSA_KV_WIDTH), lambda i, j: (i, 0, 0))],
        out_specs=pl.BlockSpec((1, tq, NSA_WIDTH), lambda i, j: (i, j, 0)),
        out_shape=jax.ShapeDtypeStruct((b, t, NSA_WIDTH), F32),
        compiler_params=_cparams(("parallel", "parallel")),
        name="nsa_window_attn",
    )(q, kw, vw)


def _outproj_kernel(x_ref, ya_ref, yb_ref, oc_ref, os_ref, ow_ref, gate_ref, w_ref, g_ref, b_ref, o_ref, *, alpha):
    sg = _sigmoid(gate_ref[...])
    lane = lax.broadcasted_iota(jnp.int32, oc_ref.shape, 1) // HEAD_DIM
    yc = jnp.zeros(oc_ref.shape, F32)
    for br, ref in enumerate((oc_ref, os_ref, ow_ref)):
        wgt = jnp.zeros(oc_ref.shape, F32)
        for hq in range(NSA_HEADS):
            col = hq * N_BRANCH + br
            wgt = jnp.where(lane == hq, sg[:, col:col + 1], wgt)
        yc = yc + wgt * ref[...]
    y = jnp.concatenate([ya_ref[...], yb_ref[...], yc], axis=1).astype(BF16)
    mix = jnp.dot(y, w_ref[...], preferred_element_type=F32)
    o_ref[...] = _layernorm(alpha * x_ref[...] + mix, g_ref[...], b_ref[...])


def _outproj(x2, ya, yb, oc, osl, ow, gate, w, g, bta, alpha):
    n, d = x2.shape
    tm = 256
    row = lambda w_: pl.BlockSpec((tm, w_), lambda i: (i, 0))
    const = lambda shape: pl.BlockSpec(shape, lambda i: (0, 0))
    return pl.pallas_call(
        functools.partial(_outproj_kernel, alpha=alpha),
        grid=(n // tm,),
        in_specs=[row(d), row(POOL_WIDTH), row(RET_WIDTH), row(NSA_WIDTH), row(NSA_WIDTH), row(NSA_WIDTH),
                  row(LANES), const((d, d)), const((1, d)), const((1, d))],
        out_specs=row(d),
        out_shape=jax.ShapeDtypeStruct((n, d), F32),
        compiler_params=_cparams(("parallel",)),
        name="out_proj_ln",
    )(x2, ya, yb, oc, osl, ow, gate, w, g, bta)


def _ffn_kernel(x_ref, xh_ref, wg_ref, wu_ref, cw_ref, cb_ref, wd_ref, g_ref, b_ref, o_ref, xb_ref, acc_ref,
                *, alpha, tm):
    j = pl.program_id(1)
    c = pl.program_id(2)
    h = CONV_HALO

    @pl.when(c == 0)
    def _():
        xb_ref[0:h, :] = jnp.where(j > 0, xh_ref[0], 0.0).astype(BF16)
        xb_ref[h:, :] = x_ref[0].astype(BF16)
        acc_ref[...] = jnp.zeros_like(acc_ref)

    xb = xb_ref[...]
    hg = jnp.dot(xb, wg_ref[...], preferred_element_type=F32)
    up = jnp.dot(xb[h:], wu_ref[...], preferred_element_type=F32)
    hc = (cb_ref[...] + pltpu.roll(hg, 2, 0)[h:] * cw_ref[0:1, :] + pltpu.roll(hg, 1, 0)[h:] * cw_ref[1:2, :]
          + hg[h:] * cw_ref[2:3, :])
    act = (_gelu_tanh(hc) * up).astype(BF16)
    acc_ref[...] += jnp.dot(act, wd_ref[...], preferred_element_type=F32)

    @pl.when(c == pl.num_programs(2) - 1)
    def _():
        o_ref[0] = _layernorm(alpha * x_ref[0] + acc_ref[...], g_ref[...], b_ref[...])


def _ffn(x, wg, wu, cw, cb, wd, g, bta, alpha):
    b, t, d = x.shape
    dff = wg.shape[1]
    tm = min(t, 512)
    fc = 256
    r = tm // CONV_HALO
    return pl.pallas_call(
        functools.partial(_ffn_kernel, alpha=alpha, tm=tm),
        grid=(b, t // tm, dff // fc),
        in_specs=[pl.BlockSpec((1, tm, d), lambda i, j, c: (i, j, 0)),
                  pl.BlockSpec((1, CONV_HALO, d), lambda i, j, c: (i, jnp.maximum(j * r - 1, 0), 0)),
                  pl.BlockSpec((d, fc), lambda i, j, c: (0, c)),
                  pl.BlockSpec((d, fc), lambda i, j, c: (0, c)),
                  pl.BlockSpec((CONV_WIDTH, fc), lambda i, j, c: (0, c)),
                  pl.BlockSpec((1, fc), lambda i, j, c: (0, c)),
                  pl.BlockSpec((fc, d), lambda i, j, c: (c, 0)),
                  pl.BlockSpec((1, d), lambda i, j, c: (0, 0)),
                  pl.BlockSpec((1, d), lambda i, j, c: (0, 0))],
        out_specs=pl.BlockSpec((1, tm, d), lambda i, j, c: (i, j, 0)),
        out_shape=jax.ShapeDtypeStruct((b, t, d), F32),
        scratch_shapes=[pltpu.VMEM((CONV_HALO + tm, d), BF16), pltpu.VMEM((tm, d), F32)],
        compiler_params=_cparams(("parallel", "parallel", "arbitrary")),
        name="conv_ffn_ln",
    )(x, x, wg, wu, cw, cb, wd, g, bta)


def _overlap_t(nc, n_slc):
    ci = np.arange(nc)[None, :]
    sj = np.arange(LANES)[:, None]
    ov = np.clip(np.minimum(ci * CMP_STRIDE + CMP_BLOCK, (sj + 1) * SLC_BLOCK)
                 - np.maximum(ci * CMP_STRIDE, sj * SLC_BLOCK), 0, None).astype(np.float32) / CMP_STRIDE
    ov[n_slc:] = 0.0
    ov[:, nc - 1] = 0.0
    return jnp.asarray(ov)


def kernel(x, positions, w_in, w_out, pool_w, pool_scale, ret_gn_g, cmp_pos_k, cmp_w1_k, cmp_b1_k, cmp_w2_k,
           cmp_pos_v, cmp_w1_v, cmp_b1_v, cmp_w2_v, ffn_w_gate, ffn_w_up, ffn_conv_w, ffn_conv_b, ffn_w_down,
           ln1_g, ln1_b, ln2_g, ln2_b):
    b, t, d = x.shape
    depth = w_in.shape[0]
    alpha = float((2 * depth) ** 0.25)
    n = b * t
    nc = t // CMP_STRIDE
    n_slc = t // SLC_BLOCK

    inv = ROPE_THETA ** (-jnp.arange(0, HEAD_DIM, 2, dtype=F32) / HEAD_DIM)
    inv = jnp.tile(inv, LANES // (HEAD_DIM // 2))[None, :]
    cos, sin = _rope_tables(positions, inv)
    ends = jnp.minimum(jnp.arange(nc) * CMP_STRIDE + CMP_BLOCK - 1, t - 1)
    cos_e, sin_e = _rope_tables(positions[:, ends], inv)
    ovt = _overlap_t(nc, n_slc)
    eye_g = jnp.eye(len(POOL_WINDOWS), dtype=F32)

    for l in range(depth):
        w_in_p = jnp.pad(w_in[l], ((0, 0), (0, IN_PAD - IN_WIDTH))).astype(BF16)
        x2 = x.reshape(n, d)
        v_pool, h_ret, q_nsa, k_cmp, v_cmp, kv4, gate = _inproj(x2, w_in_p)

        wbd = jnp.einsum('gcd,ge->gced', pool_w[l], eye_g).reshape(POOL_WIDTH, POOL_WIDTH).astype(BF16)
        y_a = _pool(v_pool.reshape(b, t, POOL_WIDTH), wbd, pool_scale[l][None, :])

        y_b = _retention(h_ret.reshape(b, t, 4 * RET_WIDTH), cos, sin, ret_gn_g[l][None, :])

        wk = _compress_weights(cmp_pos_k[l], cmp_w1_k[l], cmp_b1_k[l], cmp_w2_k[l])
        wv = _compress_weights(cmp_pos_v[l], cmp_w1_v[l], cmp_b1_v[l], cmp_w2_v[l])
        kc, vc = _compress(k_cmp.reshape(b, nc, CMP_STRIDE * NSA_KV_WIDTH),
                           v_cmp.reshape(b, nc, CMP_STRIDE * NSA_KV_WIDTH), wk, wv, cos_e, sin_e)
        ks, vs, kw, vw = _krope(kv4.reshape(b, t, 4 * NSA_KV_WIDTH), cos, sin)
        qr, o_cmp, sel = _cmpattn(q_nsa.reshape(b, t, NSA_WIDTH), cos, sin, kc, vc, ovt)
        o_slc = _slc(qr, sel, ks, vs)
        o_win = _win(qr, kw, vw)

        x2 = _outproj(x2, y_a.reshape(n, POOL_WIDTH), y_b.reshape(n, RET_WIDTH), o_cmp.reshape(n, NSA_WIDTH),
                      o_slc.reshape(n, NSA_WIDTH), o_win.reshape(n, NSA_WIDTH), gate,
                      w_out[l].astype(BF16), ln1_g[l][None, :], ln1_b[l][None, :], alpha)
        x = _ffn(x2.reshape(b, t, d), ffn_w_gate[l].astype(BF16), ffn_w_up[l].astype(BF16), ffn_conv_w[l],
                 ffn_conv_b[l][None, :], ffn_w_down[l].astype(BF16), ln2_g[l][None, :], ln2_b[l][None, :], alpha)
    return x
```

```python
import functools
import math

import jax
import jax.numpy as jnp
import numpy as np
from jax import lax
from jax.experimental import pallas as pl
from jax.experimental.pallas import tpu as pltpu

F32 = jnp.float32
BF16 = jnp.bfloat16

HEAD_DIM = 64
POOL_WINDOWS = (2, 4, 8, 16)
POOL_GROUP_WIDTH = 64
POOL_WIDTH = POOL_GROUP_WIDTH * len(POOL_WINDOWS)
POOL_HALO = 16
RET_HEADS = 6
RET_WIDTH = RET_HEADS * HEAD_DIM
RET_CHUNK = 128
NSA_HEADS = 6
NSA_WIDTH = NSA_HEADS * HEAD_DIM
NSA_KV_HEADS = 2
NSA_REP = NSA_HEADS // NSA_KV_HEADS
NSA_KV_WIDTH = NSA_KV_HEADS * HEAD_DIM
CMP_BLOCK = 32
CMP_STRIDE = 16
CMP_HIDDEN = 128
SLC_BLOCK = 64
SLC_TOP = 16
WINDOW = 512
N_BRANCH = 3
FORCE_SCORE = 1e6
CONV_WIDTH = 3
CONV_HALO = 16
ROPE_THETA = 10000.0
LN_EPS = 1e-5
GN_EPS = 1e-5
NEG_INF = -1e30
Q_SCALE = HEAD_DIM ** -0.5 * math.log2(math.e)

LANES = 128
SUBLANES = 8
VMEM_LIMIT = 56 * 1024 * 1024

ATT_TQ = 128
SEL_TQ = 256
SEL_CHUNK = 512
SEL_ROWS = LANES - HEAD_DIM

OFF_POOL = 0
OFF_RET = OFF_POOL + POOL_WIDTH
OFF_QNSA = OFF_RET + 4 * RET_WIDTH
OFF_KCMP = OFF_QNSA + NSA_WIDTH
OFF_VCMP = OFF_KCMP + NSA_KV_WIDTH
OFF_KV4 = OFF_VCMP + NSA_KV_WIDTH
OFF_GATE = OFF_KV4 + 4 * NSA_KV_WIDTH
IN_WIDTH = OFF_GATE + NSA_HEADS * N_BRANCH
IN_PAD = OFF_GATE + LANES


def _cparams(sem):
    return pltpu.CompilerParams(dimension_semantics=sem, vmem_limit_bytes=VMEM_LIMIT)


def _rope(x, cos, sin_signed):
    lane = lax.broadcasted_iota(jnp.int32, x.shape, 1)
    first = (lane & (HEAD_DIM - 1)) < HEAD_DIM // 2
    partner = jnp.where(first, pltpu.roll(x, LANES - HEAD_DIM // 2, 1), pltpu.roll(x, HEAD_DIM // 2, 1))
    return x * cos + partner * sin_signed


def _gelu_tanh(x):
    return 0.5 * x * (1.0 + jnp.tanh(math.sqrt(2.0 / math.pi) * (x + 0.044715 * (x * x * x))))


def _sigmoid(x):
    return 1.0 / (1.0 + jnp.exp(-x))


def _layernorm(y, g, b):
    mu = jnp.mean(y, axis=-1, keepdims=True)
    d = y - mu
    var = jnp.mean(d * d, axis=-1, keepdims=True)
    return d * lax.rsqrt(var + LN_EPS) * g + b


def _dot(a, b):
    return jnp.dot(a, b, preferred_element_type=F32)


def _dot_nt(a, b):
    return lax.dot_general(a, b, (((1,), (1,)), ((), ())), preferred_element_type=F32)


def _rope_table_kernel(pos_ref, inv_ref, cos_ref, sin_ref):
    ang = pos_ref[0].astype(F32) * inv_ref[...]
    lane = lax.broadcasted_iota(jnp.int32, ang.shape, 1)
    first = (lane & (HEAD_DIM - 1)) < HEAD_DIM // 2
    s = jnp.sin(ang)
    cos_ref[0] = jnp.cos(ang)
    sin_ref[0] = jnp.where(first, -s, s)


def _rope_tables(pos, inv):
    b, n = pos.shape
    tt = min(n, 512)
    shp = jax.ShapeDtypeStruct((b, n, LANES), F32)
    return pl.pallas_call(
        _rope_table_kernel,
        grid=(b, n // tt),
        in_specs=[pl.BlockSpec((1, tt, 1), lambda i, j: (i, j, 0)),
                  pl.BlockSpec((1, LANES), lambda i, j: (0, 0))],
        out_specs=[pl.BlockSpec((1, tt, LANES), lambda i, j: (i, j, 0))] * 2,
        out_shape=[shp, shp],
        compiler_params=_cparams(("parallel", "parallel")),
        name="rope_tables",
    )(pos.reshape(b, n, 1), inv)


_IN_OUTS = ((OFF_POOL, POOL_WIDTH), (OFF_RET, 4 * RET_WIDTH), (OFF_QNSA, NSA_WIDTH),
            (OFF_KCMP, NSA_KV_WIDTH), (OFF_VCMP, NSA_KV_WIDTH), (OFF_KV4, 4 * NSA_KV_WIDTH),
            (OFF_GATE, LANES))


def _inproj_kernel(x_ref, w_ref, *o_refs):
    xb = x_ref[...].astype(BF16)
    for o_ref, (off, width) in zip(o_refs, _IN_OUTS):
        step = 2 * LANES if width % (2 * LANES) == 0 else LANES
        for c in range(0, width, step):
            o_ref[:, c:c + step] = _dot(xb, w_ref[:, off + c:off + c + step])


def _inproj(x2, w):
    n, d = x2.shape
    tm = 512
    return pl.pallas_call(
        _inproj_kernel,
        grid=(n // tm,),
        in_specs=[pl.BlockSpec((tm, d), lambda i: (i, 0)),
                  pl.BlockSpec((d, IN_PAD), lambda i: (0, 0))],
        out_specs=[pl.BlockSpec((tm, wd), lambda i: (i, 0)) for _, wd in _IN_OUTS],
        out_shape=[jax.ShapeDtypeStruct((n, wd), F32) for _, wd in _IN_OUTS],
        compiler_params=_cparams(("parallel",)),
        name="in_proj",
    )(x2, w)


def _pool_kernel(cur_ref, prev_ref, w_ref, scale_ref, o_ref, *, tt):
    i = pl.program_id(1)
    cur = cur_ref[0]
    prev = jnp.where(i > 0, prev_ref[0], 0.0)
    cat = jnp.concatenate([prev, cur], axis=0)
    s2 = cat + pltpu.roll(cat, 1, 0)
    s4 = s2 + pltpu.roll(s2, 2, 0)
    s8 = s4 + pltpu.roll(s4, 4, 0)
    s16 = s8 + pltpu.roll(s8, 8, 0)
    grp = lax.broadcasted_iota(jnp.int32, cur.shape, 1) // POOL_GROUP_WIDTH
    t = i * tt + lax.broadcasted_iota(jnp.int32, cur.shape, 0)

    def pick(a, b, c, d):
        return jnp.where(grp == 0, a, jnp.where(grp == 1, b, jnp.where(grp == 2, c, d)))

    h = POOL_HALO
    wsum = pick(s2[h:], s4[h:], s8[h:], s16[h:])
    width = pick(*[float(w) for w in POOL_WINDOWS])
    cnt = jnp.minimum((t + 1).astype(F32), width)
    mixed = wsum / cnt - cur
    o_ref[0] = _dot(mixed.astype(BF16), w_ref[...]) * scale_ref[...]


def _pool(v, wbd, scale):
    b, t, c = v.shape
    tt = min(t, 512)
    r = tt // POOL_HALO
    return pl.pallas_call(
        functools.partial(_pool_kernel, tt=tt),
        grid=(b, t // tt),
        in_specs=[pl.BlockSpec((1, tt, c), lambda i, j: (i, j, 0)),
                  pl.BlockSpec((1, POOL_HALO, c), lambda i, j: (i, jnp.maximum(j * r - 1, 0), 0)),
                  pl.BlockSpec((c, c), lambda i, j: (0, 0)),
                  pl.BlockSpec((1, c), lambda i, j: (0, 0))],
        out_specs=pl.BlockSpec((1, tt, c), lambda i, j: (i, j, 0)),
        out_shape=jax.ShapeDtypeStruct((b, t, c), F32),
        compiler_params=_cparams(("parallel", "parallel")),
        name="pool_mixer",
    )(v, v, wbd, scale)


def _ret_consts():
    h, c = RET_HEADS, RET_CHUNK
    lg = np.log1p(-np.power(2.0, -5.0 - np.arange(h, dtype=np.float64)))
    i = np.arange(c, dtype=np.float64)
    diff = i[:, None] - i[None, :]
    dmask = np.where(diff >= 0, np.exp(lg[:, None, None] * np.maximum(diff, 0.0)), 0.0)
    xi = np.repeat(np.exp(lg[:, None] * (i + 1.0)).T, HEAD_DIM, axis=1)
    zeta = np.repeat(np.exp(lg[:, None] * (c - 1.0 - i)).T, HEAD_DIM, axis=1)
    gc = np.repeat(np.exp(lg * c), HEAD_DIM)[None, :]
    return (jnp.asarray(dmask, F32), jnp.asarray(xi, F32), jnp.asarray(zeta, F32), jnp.asarray(gc, F32))


def _ret_kernel(h_ref, cos_ref, sin_ref, dmask_ref, xi_ref, zeta_ref, gc_ref, gn_ref, o_ref, state_ref):
    @pl.when(pl.program_id(1) == 0)
    def _():
        state_ref[...] = jnp.zeros_like(state_ref)

    cos = cos_ref[0]
    sin = sin_ref[0]
    c = RET_CHUNK
    lane = lax.broadcasted_iota(jnp.int32, (c, LANES), 1)
    row = lax.broadcasted_iota(jnp.int32, (LANES, LANES), 0)
    col = lax.broadcasted_iota(jnp.int32, (LANES, LANES), 1)
    lo = lane < HEAD_DIM
    same_head = (row < HEAD_DIM) == (col < HEAD_DIM)
    for j in range(RET_WIDTH // LANES):
        sl = slice(j * LANES, (j + 1) * LANES)
        q = _rope(h_ref[0, :, j * LANES:(j + 1) * LANES], cos, sin)
        k = _rope(h_ref[0, :, RET_WIDTH + j * LANES:RET_WIDTH + (j + 1) * LANES], cos, sin) * (HEAD_DIM ** -0.5)
        v = h_ref[0, :, 2 * RET_WIDTH + j * LANES:2 * RET_WIDTH + (j + 1) * LANES]
        g = h_ref[0, :, 3 * RET_WIDTH + j * LANES:3 * RET_WIDTH + (j + 1) * LANES]
        kb = k.astype(BF16)
        vb = v.astype(BF16)
        state = state_ref[j]
        o = _dot(q.astype(BF16), state.astype(BF16)) * xi_ref[:, sl]
        for hh in range(2):
            m = lo if hh == 0 else jnp.logical_not(lo)
            qm = jnp.where(m, q, 0.0).astype(BF16)
            s = _dot_nt(qm, kb) * dmask_ref[2 * j + hh]
            o = o + jnp.where(m, _dot(s.astype(BF16), vb), 0.0)
        kz = (k * zeta_ref[:, sl]).astype(BF16)
        kv = lax.dot_general(kz, vb, (((0,), (0,)), ((), ())), preferred_element_type=F32)
        state_ref[j] = gc_ref[:, sl] * state + jnp.where(same_head, kv, 0.0)
        s_lo = jnp.sum(jnp.where(lo, o, 0.0), axis=-1, keepdims=True)
        s_hi = jnp.sum(jnp.where(lo, 0.0, o), axis=-1, keepdims=True)
        d = o - jnp.where(lo, s_lo, s_hi) * (1.0 / HEAD_DIM)
        d2 = d * d
        v_lo = jnp.sum(jnp.where(lo, d2, 0.0), axis=-1, keepdims=True)
        v_hi = jnp.sum(jnp.where(lo, 0.0, d2), axis=-1, keepdims=True)
        var = jnp.where(lo, v_lo, v_hi) * (1.0 / HEAD_DIM)
        on = d * lax.rsqrt(var + GN_EPS) * gn_ref[:, sl]
        o_ref[0, :, sl] = g * _sigmoid(g) * on


def _retention(hret, cos, sin, gn_g):
    b, t, _ = hret.shape
    c = RET_CHUNK
    dmask, xi, zeta, gc = _ret_consts()
    const = lambda shape: pl.BlockSpec(shape, lambda i, j: (0,) * len(shape))
    return pl.pallas_call(
        _ret_kernel,
        grid=(b, t // c),
        in_specs=[pl.BlockSpec((1, c, 4 * RET_WIDTH), lambda i, j: (i, j, 0)),
                  pl.BlockSpec((1, c, LANES), lambda i, j: (i, j, 0)),
                  pl.BlockSpec((1, c, LANES), lambda i, j: (i, j, 0)),
                  const((RET_HEADS, c, c)), const((c, RET_WIDTH)), const((c, RET_WIDTH)),
                  const((1, RET_WIDTH)), const((1, RET_WIDTH))],
        out_specs=pl.BlockSpec((1, c, RET_WIDTH), lambda i, j: (i, j, 0)),
        out_shape=jax.ShapeDtypeStruct((b, t, RET_WIDTH), F32),
        scratch_shapes=[pltpu.VMEM((RET_WIDTH // LANES, LANES, LANES), F32)],
        compiler_params=_cparams(("parallel", "arbitrary")),
        name="retention",
    )(hret, cos, sin, dmask, xi, zeta, gc, gn_g)


def _compress_kernel(k_ref, v_ref, ptk, pbk, wtk, wbk, b1k, w2k, ptv, pbv, wtv, wbv, b1v, w2v,
                     cos_ref, sin_ref, kc_ref, vct_ref):
    def comp(x_ref, pt, pb, wt, wb, b1, w2):
        x = x_ref[0]
        top = _dot((x + pt[...]).astype(BF16), wt[...])
        bot = _dot((x + pb[...]).astype(BF16), wb[...])
        rows = bot.shape[0]
        pre = top + pltpu.roll(bot, rows - 1, 0) + b1[...]
        return _dot(_gelu_tanh(pre).astype(BF16), w2[...])

    kc = comp(k_ref, ptk, pbk, wtk, wbk, b1k, w2k)
    kc_ref[0] = _rope(kc, cos_ref[0], sin_ref[0]).astype(BF16)
    vct_ref[0] = jnp.transpose(comp(v_ref, ptv, pbv, wtv, wbv, b1v, w2v)).astype(BF16)


def _compress_weights(pos, w1, b1, w2):
    half = CMP_BLOCK // 2
    g = NSA_KV_HEADS
    w1r = w1.reshape(CMP_BLOCK, HEAD_DIM, CMP_HIDDEN)
    eye = jnp.eye(g, dtype=F32)

    def lay(wh):
        return jnp.einsum('ldh,ge->lgdeh', wh, eye).reshape(half * g * HEAD_DIM, g * CMP_HIDDEN)

    def tile_pos(p):
        return jnp.broadcast_to(p[:, None, :], (half, g, HEAD_DIM)).reshape(1, half * g * HEAD_DIM)

    w2bd = jnp.einsum('hd,ge->ghed', w2, eye).reshape(g * CMP_HIDDEN, g * HEAD_DIM)
    return (tile_pos(pos[:half]), tile_pos(pos[half:]), lay(w1r[:half]).astype(BF16), lay(w1r[half:]).astype(BF16),
            jnp.tile(b1, g)[None, :], w2bd.astype(BF16))


def _compress(kc_in, vc_in, wk, wv, cos_e, sin_e):
    b, nc, width = kc_in.shape
    const = lambda a: pl.BlockSpec(a.shape, lambda i: (0,) * a.ndim)
    blk = lambda w: pl.BlockSpec((1, nc, w), lambda i: (i, 0, 0))
    return pl.pallas_call(
        _compress_kernel,
        grid=(b,),
        in_specs=[blk(width), blk(width)] + [const(a) for a in wk] + [const(a) for a in wv]
                 + [blk(LANES), blk(LANES)],
        out_specs=[blk(NSA_KV_WIDTH), pl.BlockSpec((1, NSA_KV_WIDTH, nc), lambda i: (i, 0, 0))],
        out_shape=[jax.ShapeDtypeStruct((b, nc, NSA_KV_WIDTH), BF16),
                   jax.ShapeDtypeStruct((b, NSA_KV_WIDTH, nc), BF16)],
        compiler_params=_cparams(("parallel",)),
        name="nsa_compress",
    )(kc_in, vc_in, *wk, *wv, cos_e, sin_e)


def _kv_prep_kernel(kv_ref, cos_ref, sin_ref, ksa_ref, vst_ref, kw_ref, vwt_ref, *, tt):
    j = pl.program_id(1)
    cos = cos_ref[0]
    sin = sin_ref[0]
    w = NSA_KV_WIDTH
    lane = lax.broadcasted_iota(jnp.int32, (tt, LANES), 1)
    tpos = j * tt + lax.broadcasted_iota(jnp.int32, (tt, LANES), 0)
    onehot = jnp.where(lane - HEAD_DIM == tpos // SLC_BLOCK, 1.0, 0.0)
    lo = lane < HEAD_DIM
    ks = _rope(kv_ref[0, :, 0:w], cos, sin)
    ksa_ref[0, 0] = jnp.where(lo, ks, onehot).astype(BF16)
    ksa_ref[0, 1] = jnp.where(lo, pltpu.roll(ks, HEAD_DIM, 1), onehot).astype(BF16)
    kw_ref[0] = _rope(kv_ref[0, :, 2 * w:3 * w], cos, sin).astype(BF16)
    top = lax.broadcasted_iota(jnp.int32, (LANES, tt), 0) < HEAD_DIM

    def store_t(v, ref, chunk):
        vt = jnp.transpose(v)
        per_head = (jnp.where(top, vt, 1.0), jnp.where(top, pltpu.roll(vt, HEAD_DIM, 0), 1.0))
        for g, vg in enumerate(per_head):
            for u in range(tt // chunk):
                ref[0, g, u] = vg[:, u * chunk:(u + 1) * chunk].astype(BF16)

    store_t(kv_ref[0, :, w:2 * w], vst_ref, SEL_CHUNK)
    store_t(kv_ref[0, :, 3 * w:4 * w], vwt_ref, ATT_TQ)


def _kv_prep(kv4, cos, sin):
    b, t, _ = kv4.shape
    tt = min(t, 512)
    g = NSA_KV_HEADS
    blk = lambda w: pl.BlockSpec((1, tt, w), lambda i, j: (i, j, 0))
    return pl.pallas_call(
        functools.partial(_kv_prep_kernel, tt=tt),
        grid=(b, t // tt),
        in_specs=[blk(4 * NSA_KV_WIDTH), blk(LANES), blk(LANES)],
        out_specs=[pl.BlockSpec((1, g, tt, LANES), lambda i, j: (i, 0, j, 0)),
                   pl.BlockSpec((1, g, tt // SEL_CHUNK, LANES, SEL_CHUNK), lambda i, j: (i, 0, j, 0, 0)),
                   blk(NSA_KV_WIDTH),
                   pl.BlockSpec((1, g, tt // ATT_TQ, LANES, ATT_TQ), lambda i, j: (i, 0, j, 0, 0))],
        out_shape=[jax.ShapeDtypeStruct((b, g, t, LANES), BF16),
                   jax.ShapeDtypeStruct((b, g, t // SEL_CHUNK, LANES, SEL_CHUNK), BF16),
                   jax.ShapeDtypeStruct((b, t, NSA_KV_WIDTH), BF16),
                   jax.ShapeDtypeStruct((b, g, t // ATT_TQ, LANES, ATT_TQ), BF16)],
        compiler_params=_cparams(("parallel", "parallel")),
        name="nsa_kv_prep",
    )(kv4, cos, sin)


def _queries_t(qt, g, other, q_first):
    cols = []
    for r in range(NSA_REP):
        hq = g * NSA_REP + r
        piece = qt[hq * HEAD_DIM:(hq + 1) * HEAD_DIM]
        cols.append(jnp.concatenate([piece, other] if q_first else [other, piece], axis=0))
    return jnp.concatenate(cols, axis=1)


def _heads_from_t(pieces):
    slabs = [jnp.transpose(jnp.concatenate(pieces[2 * j:2 * j + 2], axis=0)) for j in range(len(pieces) // 2)]
    return jnp.concatenate(slabs, axis=1)


def _normalised_heads(acc, tq):
    ot = acc[0:HEAD_DIM] * (1.0 / acc[HEAD_DIM:HEAD_DIM + 1])
    return [ot[:, r * tq:(r + 1) * tq] for r in range(NSA_REP)]


def _cmpattn_kernel(q_ref, cos_ref, sin_ref, kc_ref, vct_ref, ovt_ref, qt_ref, o_ref, nsel_ref, *, tq, n_slc):
    i = pl.program_id(1)
    cos = cos_ref[0]
    sin = sin_ref[0]
    qt = jnp.concatenate([jnp.transpose(_rope(q_ref[0, :, j * LANES:(j + 1) * LANES], cos, sin) * Q_SCALE)
                          for j in range(NSA_WIDTH // LANES)], axis=0).astype(BF16)
    qt_ref[0] = qt
    kc = kc_ref[0]
    vct = vct_ref[0]
    nc = kc.shape[0]
    n = lax.broadcasted_iota(jnp.int32, (nc, NSA_REP * tq), 0)
    t = i * tq + (lax.broadcasted_iota(jnp.int32, (nc, NSA_REP * tq), 1) & (tq - 1))
    visible = n * CMP_STRIDE + (CMP_BLOCK - 1) <= t
    zeros = jnp.zeros((HEAD_DIM, tq), BF16)
    rows = max(SUBLANES, n_slc)
    blk = lax.broadcasted_iota(jnp.int32, (rows, tq), 0)
    tl = i * tq + lax.broadcasted_iota(jnp.int32, (rows, tq), 1)
    cur = tl // SLC_BLOCK
    forced = (blk == 0) | (blk == cur) | (blk == cur - 1)
    valid = (blk * SLC_BLOCK <= tl) & (blk < n_slc)
    sub = lax.broadcasted_iota(jnp.int32, (SUBLANES, tq), 0)
    pieces = []
    for g in range(NSA_KV_HEADS):
        s = jnp.where(visible, _dot(kc, _queries_t(qt, g, zeros, g == 0)), NEG_INF)
        e = jnp.exp2(s - jnp.max(s, axis=0, keepdims=True))
        p = jnp.where(visible, e * (1.0 / jnp.sum(e, axis=0, keepdims=True)), 0.0)
        ot = _dot(vct, p.astype(BF16))
        pieces.extend(ot[g * HEAD_DIM:(g + 1) * HEAD_DIM, r * tq:(r + 1) * tq] for r in range(NSA_REP))
        psum = p[:, 0:tq] + p[:, tq:2 * tq] + p[:, 2 * tq:3 * tq]
        imp = jnp.dot(ovt_ref[...], psum, precision=lax.Precision.HIGHEST, preferred_element_type=F32)
        score = jnp.where(valid, jnp.where(forced, FORCE_SCORE, imp), -1.0)
        groups = [score[SUBLANES * r:SUBLANES * (r + 1)] for r in range(rows // SUBLANES)]
        ranks = [jnp.zeros((SUBLANES, tq), F32) for _ in groups]
        for jp in range(n_slc):
            rowv = score[jp:jp + 1, :]
            for r, grp in enumerate(groups):
                if SUBLANES * r > jp:
                    ahead = rowv >= grp
                elif SUBLANES * r + SUBLANES - 1 < jp:
                    ahead = rowv > grp
                else:
                    ahead = (rowv > grp) | ((rowv == grp) & (sub > jp - SUBLANES * r))
                ranks[r] = ranks[r] + jnp.where(ahead, 1.0, 0.0)
        rank = jnp.concatenate(ranks, axis=0)
        bias = jnp.where((rank < float(min(SLC_TOP, n_slc))) & (blk < n_slc), 0.0, NEG_INF)
        if rows < SEL_ROWS:
            bias = jnp.concatenate([bias, jnp.zeros((SEL_ROWS - rows, tq), F32)], axis=0)
        nsel_ref[0, g] = bias.astype(BF16)
    o_ref[0] = _heads_from_t(pieces)


def _cmpattn(q, cos, sin, kc, vct, ovt):
    b, t, _ = q.shape
    nc = kc.shape[1]
    tq = ATT_TQ
    n_slc = t // SLC_BLOCK
    assert n_slc <= SEL_ROWS and n_slc % SUBLANES == 0
    blk = lambda w: pl.BlockSpec((1, tq, w), lambda i, j: (i, j, 0))
    return pl.pallas_call(
        functools.partial(_cmpattn_kernel, tq=tq, n_slc=n_slc),
        grid=(b, t // tq),
        in_specs=[blk(NSA_WIDTH), blk(LANES), blk(LANES),
                  pl.BlockSpec((1, nc, NSA_KV_WIDTH), lambda i, j: (i, 0, 0)),
                  pl.BlockSpec((1, NSA_KV_WIDTH, nc), lambda i, j: (i, 0, 0)),
                  pl.BlockSpec(ovt.shape, lambda i, j: (0, 0))],
        out_specs=[pl.BlockSpec((1, NSA_WIDTH, tq), lambda i, j: (i, 0, j)),
                   blk(NSA_WIDTH),
                   pl.BlockSpec((1, NSA_KV_HEADS, SEL_ROWS, tq), lambda i, j: (i, 0, 0, j))],
        out_shape=[jax.ShapeDtypeStruct((b, NSA_WIDTH, t), BF16),
                   jax.ShapeDtypeStruct((b, t, NSA_WIDTH), F32),
                   jax.ShapeDtypeStruct((b, NSA_KV_HEADS, SEL_ROWS, t), BF16)],
        compiler_params=_cparams(("parallel", "parallel")),
        name="nsa_compressed_attn_select",
    )(q, cos, sin, kc, vct, ovt)


def _slc_kernel(qt_ref, nsel_ref, ksa_ref, vst_ref, o_ref, *, tq, kc):
    i = pl.program_id(1)
    qt = qt_ref[0]
    c_diag = (i * tq) // kc
    kpos = lax.broadcasted_iota(jnp.int32, (kc, NSA_REP * tq), 0)
    tpos = i * tq + (lax.broadcasted_iota(jnp.int32, (kc, NSA_REP * tq), 1) & (tq - 1))
    qas = [_queries_t(qt, g, nsel_ref[0, g], True) for g in range(NSA_KV_HEADS)]

    def step(c, carry, causal):
        out = []
        for g, (m, acc) in enumerate(carry):
            k = ksa_ref[0, g, pl.ds(pl.multiple_of(c * kc, kc), kc), :]
            s = _dot(k, qas[g])
            if causal:
                s = jnp.where(c * kc + kpos <= tpos, s, NEG_INF)
            m_new = jnp.maximum(m, jnp.max(s, axis=0, keepdims=True))
            p = jnp.exp2(s - m_new).astype(BF16)
            out.append((m_new, jnp.exp2(m - m_new) * acc + _dot(vst_ref[0, g, c], p)))
        return tuple(out)

    init = tuple((jnp.full((1, NSA_REP * tq), NEG_INF, F32), jnp.zeros((LANES, NSA_REP * tq), F32))
                 for _ in range(NSA_KV_HEADS))
    carry = lax.fori_loop(0, c_diag, lambda c, carry: step(c, carry, False), init)
    carry = step(c_diag, carry, True)
    pieces = []
    for _, acc in carry:
        pieces.extend(_normalised_heads(acc, tq))
    o_ref[0] = _heads_from_t(pieces)


def _slc(qt, nsel, ksa, vst):
    b, _, t = qt.shape
    tq = SEL_TQ
    kc = SEL_CHUNK
    g = NSA_KV_HEADS
    return pl.pallas_call(
        functools.partial(_slc_kernel, tq=tq, kc=kc),
        grid=(b, t // tq),
        in_specs=[pl.BlockSpec((1, NSA_WIDTH, tq), lambda i, j: (i, 0, j)),
                  pl.BlockSpec((1, g, SEL_ROWS, tq), lambda i, j: (i, 0, 0, j)),
                  pl.BlockSpec((1, g, t, LANES), lambda i, j: (i, 0, 0, 0)),
                  pl.BlockSpec((1, g, t // kc, LANES, kc), lambda i, j: (i, 0, 0, 0, 0))],
        out_specs=pl.BlockSpec((1, tq, NSA_WIDTH), lambda i, j: (i, j, 0)),
        out_shape=jax.ShapeDtypeStruct((b, t, NSA_WIDTH), F32),
        compiler_params=_cparams(("parallel", "parallel")),
        name="nsa_selected_attn",
    )(qt, nsel, ksa, vst)


def _win_kernel(qt_ref, kw_ref, vwt_ref, o_ref, *, tq, nblk, t_total):
    i = pl.program_id(1)
    qt = qt_ref[0]
    span = nblk * tq
    first_blk = jnp.clip(i - WINDOW // tq, 0, t_total // tq - nblk)
    start = pl.multiple_of(first_blk * tq, tq)
    k = kw_ref[0, pl.ds(start, span), :]
    d = (i * tq + (lax.broadcasted_iota(jnp.int32, (span, NSA_REP * tq), 1) & (tq - 1))
         - (start + lax.broadcasted_iota(jnp.int32, (span, NSA_REP * tq), 0)))
    keep = (d >= 0) & (d < WINDOW)
    zeros = jnp.zeros((HEAD_DIM, tq), BF16)
    pieces = []
    for g in range(NSA_KV_HEADS):
        s = jnp.where(keep, _dot(k, _queries_t(qt, g, zeros, g == 0)), NEG_INF)
        p = jnp.exp2(s - jnp.max(s, axis=0, keepdims=True)).astype(BF16)
        vt = jnp.concatenate([vwt_ref[0, g, first_blk + u] for u in range(nblk)], axis=1)
        pieces.extend(_normalised_heads(_dot(vt, p), tq))
    o_ref[0] = _heads_from_t(pieces)


def _win(qt, kw, vwt):
    b, _, t = qt.shape
    tq = ATT_TQ
    nblk = min(t // tq, WINDOW // tq + 1)
    g = NSA_KV_HEADS
    return pl.pallas_call(
        functools.partial(_win_kernel, tq=tq, nblk=nblk, t_total=t),
        grid=(b, t // tq),
        in_specs=[pl.BlockSpec((1, NSA_WIDTH, tq), lambda i, j: (i, 0, j)),
                  pl.BlockSpec((1, t, NSA_KV_WIDTH), lambda i, j: (i, 0, 0)),
                  pl.BlockSpec((1, g, t // tq, LANES, tq), lambda i, j: (i, 0, 0, 0, 0))],
        out_specs=pl.BlockSpec((1, tq, NSA_WIDTH), lambda i, j: (i, j, 0)),
        out_shape=jax.ShapeDtypeStruct((b, t, NSA_WIDTH), F32),
        compiler_params=_cparams(("parallel", "parallel")),
        name="nsa_window_attn",
    )(qt, kw, vwt)


def _outproj_kernel(x_ref, ya_ref, yb_ref, oc_ref, os_ref, ow_ref, gate_ref, w_ref, g_ref, b_ref, o_ref, *, alpha):
    sg = _sigmoid(gate_ref[...])
    lane = lax.broadcasted_iota(jnp.int32, oc_ref.shape, 1) // HEAD_DIM
    yc = jnp.zeros(oc_ref.shape, F32)
    for br, ref in enumerate((oc_ref, os_ref, ow_ref)):
        wgt = jnp.zeros(oc_ref.shape, F32)
        for hq in range(NSA_HEADS):
            col = hq * N_BRANCH + br
            wgt = jnp.where(lane == hq, sg[:, col:col + 1], wgt)
        yc = yc + wgt * ref[...]
    y = jnp.concatenate([ya_ref[...], yb_ref[...], yc], axis=1).astype(BF16)
    o_ref[...] = _layernorm(alpha * x_ref[...] + _dot(y, w_ref[...]), g_ref[...], b_ref[...])


def _outproj(x2, ya, yb, oc, osl, ow, gate, w, g, bta, alpha):
    n, d = x2.shape
    tm = 256
    row = lambda w_: pl.BlockSpec((tm, w_), lambda i: (i, 0))
    const = lambda shape: pl.BlockSpec(shape, lambda i: (0, 0))
    return pl.pallas_call(
        functools.partial(_outproj_kernel, alpha=alpha),
        grid=(n // tm,),
        in_specs=[row(d), row(POOL_WIDTH), row(RET_WIDTH), row(NSA_WIDTH), row(NSA_WIDTH), row(NSA_WIDTH),
                  row(LANES), const((d, d)), const((1, d)), const((1, d))],
        out_specs=row(d),
        out_shape=jax.ShapeDtypeStruct((n, d), F32),
        compiler_params=_cparams(("parallel",)),
        name="out_proj_ln",
    )(x2, ya, yb, oc, osl, ow, gate, w, g, bta)


def _ffn_kernel(x_ref, xh_ref, wg_ref, wu_ref, cw_ref, cb_ref, wd_ref, g_ref, b_ref, o_ref, xb_ref, act_ref,
                *, alpha, fc):
    j = pl.program_id(1)
    h = CONV_HALO
    xb_ref[0:h, :] = jnp.where(j > 0, xh_ref[0], 0.0).astype(BF16)
    xb_ref[h:, :] = x_ref[0].astype(BF16)
    for c in range(0, wg_ref.shape[1], fc):
        hg = _dot(xb_ref[...], wg_ref[:, c:c + fc])
        up = _dot(xb_ref[h:, :], wu_ref[:, c:c + fc])
        hc = (cb_ref[:, c:c + fc] + pltpu.roll(hg, 2, 0)[h:] * cw_ref[0:1, c:c + fc]
              + pltpu.roll(hg, 1, 0)[h:] * cw_ref[1:2, c:c + fc] + hg[h:] * cw_ref[2:3, c:c + fc])
        act_ref[:, c:c + fc] = (_gelu_tanh(hc) * up).astype(BF16)
    o_ref[0] = _layernorm(alpha * x_ref[0] + _dot(act_ref[...], wd_ref[...]), g_ref[...], b_ref[...])


def _ffn(x, wg, wu, cw, cb, wd, g, bta, alpha):
    b, t, d = x.shape
    dff = wg.shape[1]
    tm = min(t, 512)
    fc = 2 * LANES
    r = tm // CONV_HALO
    const = lambda shape: pl.BlockSpec(shape, lambda i, j: (0, 0), pipeline_mode=pl.Buffered(1))
    return pl.pallas_call(
        functools.partial(_ffn_kernel, alpha=alpha, fc=fc),
        grid=(b, t // tm),
        in_specs=[pl.BlockSpec((1, tm, d), lambda i, j: (i, j, 0)),
                  pl.BlockSpec((1, CONV_HALO, d), lambda i, j: (i, jnp.maximum(j * r - 1, 0), 0)),
                  const((d, dff)), const((d, dff)), const((CONV_WIDTH, dff)), const((1, dff)), const((dff, d)),
                  const((1, d)), const((1, d))],
        out_specs=pl.BlockSpec((1, tm, d), lambda i, j: (i, j, 0)),
        out_shape=jax.ShapeDtypeStruct((b, t, d), F32),
        scratch_shapes=[pltpu.VMEM((CONV_HALO + tm, d), BF16), pltpu.VMEM((tm, dff), BF16)],
        compiler_params=_cparams(("parallel", "parallel")),
        name="conv_ffn_ln",
    )(x, x, wg, wu, cw, cb, wd, g, bta)


def _overlap_t(nc, n_slc):
    rows = max(SUBLANES, n_slc)
    ci = np.arange(nc)[None, :]
    sj = np.arange(rows)[:, None]
    ov = np.clip(np.minimum(ci * CMP_STRIDE + CMP_BLOCK, (sj + 1) * SLC_BLOCK)
                 - np.maximum(ci * CMP_STRIDE, sj * SLC_BLOCK), 0, None).astype(np.float32) / CMP_STRIDE
    ov[n_slc:] = 0.0
    ov[:, nc - 1] = 0.0
    return jnp.asarray(ov)


def kernel(x, positions, w_in, w_out, pool_w, pool_scale, ret_gn_g, cmp_pos_k, cmp_w1_k, cmp_b1_k, cmp_w2_k,
           cmp_pos_v, cmp_w1_v, cmp_b1_v, cmp_w2_v, ffn_w_gate, ffn_w_up, ffn_conv_w, ffn_conv_b, ffn_w_down,
           ln1_g, ln1_b, ln2_g, ln2_b):
    b, t, d = x.shape
    depth = w_in.shape[0]
    alpha = float((2 * depth) ** 0.25)
    n = b * t
    nc = t // CMP_STRIDE
    n_slc = t // SLC_BLOCK

    inv = ROPE_THETA ** (-jnp.arange(0, HEAD_DIM, 2, dtype=F32) / HEAD_DIM)
    inv = jnp.tile(inv, LANES // (HEAD_DIM // 2))[None, :]
    cos, sin = _rope_tables(positions, inv)
    ends = jnp.minimum(jnp.arange(nc) * CMP_STRIDE + CMP_BLOCK - 1, t - 1)
    cos_e, sin_e = _rope_tables(positions[:, ends], inv)
    ovt = _overlap_t(nc, n_slc)
    eye_g = jnp.eye(len(POOL_WINDOWS), dtype=F32)

    for l in range(depth):
        w_in_p = jnp.pad(w_in[l], ((0, 0), (0, IN_PAD - IN_WIDTH))).astype(BF16)
        x2 = x.reshape(n, d)
        v_pool, h_ret, q_nsa, k_cmp, v_cmp, kv4, gate = _inproj(x2, w_in_p)

        wbd = jnp.einsum('gcd,ge->gced', pool_w[l], eye_g).reshape(POOL_WIDTH, POOL_WIDTH).astype(BF16)
        y_a = _pool(v_pool.reshape(b, t, POOL_WIDTH), wbd, pool_scale[l][None, :])

        y_b = _retention(h_ret.reshape(b, t, 4 * RET_WIDTH), cos, sin, ret_gn_g[l][None, :])

        wk = _compress_weights(cmp_pos_k[l], cmp_w1_k[l], cmp_b1_k[l], cmp_w2_k[l])
        wv = _compress_weights(cmp_pos_v[l], cmp_w1_v[l], cmp_b1_v[l], cmp_w2_v[l])
        kc, vct = _compress(k_cmp.reshape(b, nc, CMP_STRIDE * NSA_KV_WIDTH),
                            v_cmp.reshape(b, nc, CMP_STRIDE * NSA_KV_WIDTH), wk, wv, cos_e, sin_e)
        ksa, vst, kw, vwt = _kv_prep(kv4.reshape(b, t, 4 * NSA_KV_WIDTH), cos, sin)
        qt, o_cmp, nsel = _cmpattn(q_nsa.reshape(b, t, NSA_WIDTH), cos, sin, kc, vct, ovt)
        o_slc = _slc(qt, nsel, ksa, vst)
        o_win = _win(qt, kw, vwt)

        x2 = _outproj(x2, y_a.reshape(n, POOL_WIDTH), y_b.reshape(n, RET_WIDTH), o_cmp.reshape(n, NSA_WIDTH),
                      o_slc.reshape(n, NSA_WIDTH), o_win.reshape(n, NSA_WIDTH), gate,
                      w_out[l].astype(BF16), ln1_g[l][None, :], ln1_b[l][None, :], alpha)
        x = _ffn(x2.reshape(b, t, d), ffn_w_gate[l].astype(BF16), ffn_w_up[l].astype(BF16), ffn_conv_w[l],
                 ffn_conv_b[l][None, :], ffn_w_down[l].astype(BF16), ln2_g[l][None, :], ln2_b[l][None, :], alpha)
    return x
```

```python
import functools
import math

import jax
import jax.numpy as jnp
import numpy as np
from jax import lax
from jax.experimental import pallas as pl
from jax.experimental.pallas import tpu as pltpu

F32 = jnp.float32
BF16 = jnp.bfloat16

HEAD_DIM = 64
POOL_WINDOWS = (2, 4, 8, 16)
POOL_GROUP_WIDTH = 64
POOL_WIDTH = POOL_GROUP_WIDTH * len(POOL_WINDOWS)
POOL_HALO = 16
RET_HEADS = 6
RET_WIDTH = RET_HEADS * HEAD_DIM
RET_CHUNK = 128
NSA_HEADS = 6
NSA_WIDTH = NSA_HEADS * HEAD_DIM
NSA_KV_HEADS = 2
NSA_REP = NSA_HEADS // NSA_KV_HEADS
NSA_KV_WIDTH = NSA_KV_HEADS * HEAD_DIM
CMP_BLOCK = 32
CMP_STRIDE = 16
CMP_HIDDEN = 128
SLC_BLOCK = 64
SLC_TOP = 16
WINDOW = 512
N_BRANCH = 3
FORCE_SCORE = 1e6
CONV_WIDTH = 3
CONV_HALO = 16
ROPE_THETA = 10000.0
LN_EPS = 1e-5
GN_EPS = 1e-5
NEG_INF = -1e30
Q_SCALE = HEAD_DIM ** -0.5 * math.log2(math.e)

LANES = 128
SUBLANES = 8
VMEM_LIMIT = 56 * 1024 * 1024

ATT_TQ = 128
SEL_TQ = 256
SEL_CHUNK = 512
SEL_ROWS = LANES - HEAD_DIM
RANK_STEP = 16
WIN_TQ = 256
RET_STEP_CHUNKS = 4

OFF_POOL = 0
OFF_RET = OFF_POOL + POOL_WIDTH
OFF_QNSA = OFF_RET + 4 * RET_WIDTH
OFF_KCMP = OFF_QNSA + NSA_WIDTH
OFF_VCMP = OFF_KCMP + NSA_KV_WIDTH
OFF_KV4 = OFF_VCMP + NSA_KV_WIDTH
OFF_GATE = OFF_KV4 + 4 * NSA_KV_WIDTH
IN_WIDTH = OFF_GATE + NSA_HEADS * N_BRANCH
IN_PAD = OFF_GATE + LANES


def _cparams(sem):
    return pltpu.CompilerParams(dimension_semantics=sem, vmem_limit_bytes=VMEM_LIMIT)


def _rope(x, cos, sin_signed):
    lane = lax.broadcasted_iota(jnp.int32, x.shape, 1)
    first = (lane & (HEAD_DIM - 1)) < HEAD_DIM // 2
    partner = jnp.where(first, pltpu.roll(x, LANES - HEAD_DIM // 2, 1), pltpu.roll(x, HEAD_DIM // 2, 1))
    return x * cos + partner * sin_signed


def _gelu_tanh(x):
    return 0.5 * x * (1.0 + jnp.tanh(math.sqrt(2.0 / math.pi) * (x + 0.044715 * (x * x * x))))


def _sigmoid(x):
    return 1.0 / (1.0 + jnp.exp(-x))


def _layernorm(y, g, b):
    mu = jnp.mean(y, axis=-1, keepdims=True)
    d = y - mu
    var = jnp.mean(d * d, axis=-1, keepdims=True)
    return d * lax.rsqrt(var + LN_EPS) * g + b


def _dot(a, b):
    return jnp.dot(a, b, preferred_element_type=F32)


def _dot_nt(a, b):
    return lax.dot_general(a, b, (((1,), (1,)), ((), ())), preferred_element_type=F32)


def _rope_table_kernel(pos_ref, inv_ref, cos_ref, sin_ref):
    ang = pos_ref[0].astype(F32) * inv_ref[...]
    lane = lax.broadcasted_iota(jnp.int32, ang.shape, 1)
    first = (lane & (HEAD_DIM - 1)) < HEAD_DIM // 2
    s = jnp.sin(ang)
    cos_ref[0] = jnp.cos(ang)
    sin_ref[0] = jnp.where(first, -s, s)


def _rope_tables(pos, inv):
    b, n = pos.shape
    tt = min(n, 512)
    shp = jax.ShapeDtypeStruct((b, n, LANES), F32)
    return pl.pallas_call(
        _rope_table_kernel,
        grid=(b, n // tt),
        in_specs=[pl.BlockSpec((1, tt, 1), lambda i, j: (i, j, 0)),
                  pl.BlockSpec((1, LANES), lambda i, j: (0, 0))],
        out_specs=[pl.BlockSpec((1, tt, LANES), lambda i, j: (i, j, 0))] * 2,
        out_shape=[shp, shp],
        compiler_params=_cparams(("parallel", "parallel")),
        name="rope_tables",
    )(pos.reshape(b, n, 1), inv)


_IN_OUTS = ((OFF_POOL, POOL_WIDTH), (OFF_RET, 4 * RET_WIDTH), (OFF_QNSA, NSA_WIDTH),
            (OFF_KCMP, NSA_KV_WIDTH), (OFF_VCMP, NSA_KV_WIDTH), (OFF_KV4, 4 * NSA_KV_WIDTH),
            (OFF_GATE, LANES))


def _inproj_kernel(x_ref, w_ref, *o_refs):
    xb = x_ref[...].astype(BF16)
    for o_ref, (off, width) in zip(o_refs, _IN_OUTS):
        step = 2 * LANES if width % (2 * LANES) == 0 else LANES
        for c in range(0, width, step):
            o_ref[:, c:c + step] = _dot(xb, w_ref[:, off + c:off + c + step])


def _inproj(x2, w):
    n, d = x2.shape
    tm = 512
    return pl.pallas_call(
        _inproj_kernel,
        grid=(n // tm,),
        in_specs=[pl.BlockSpec((tm, d), lambda i: (i, 0)),
                  pl.BlockSpec((d, IN_PAD), lambda i: (0, 0))],
        out_specs=[pl.BlockSpec((tm, wd), lambda i: (i, 0)) for _, wd in _IN_OUTS],
        out_shape=[jax.ShapeDtypeStruct((n, wd), F32) for _, wd in _IN_OUTS],
        compiler_params=_cparams(("parallel",)),
        name="in_proj",
    )(x2, w)


def _pool_kernel(cur_ref, prev_ref, w_ref, scale_ref, o_ref, *, tt):
    i = pl.program_id(1)
    cur = cur_ref[0]
    prev = jnp.where(i > 0, prev_ref[0], 0.0)
    cat = jnp.concatenate([prev, cur], axis=0)
    s2 = cat + pltpu.roll(cat, 1, 0)
    s4 = s2 + pltpu.roll(s2, 2, 0)
    s8 = s4 + pltpu.roll(s4, 4, 0)
    s16 = s8 + pltpu.roll(s8, 8, 0)
    grp = lax.broadcasted_iota(jnp.int32, cur.shape, 1) // POOL_GROUP_WIDTH
    t = i * tt + lax.broadcasted_iota(jnp.int32, cur.shape, 0)

    def pick(a, b, c, d):
        return jnp.where(grp == 0, a, jnp.where(grp == 1, b, jnp.where(grp == 2, c, d)))

    h = POOL_HALO
    wsum = pick(s2[h:], s4[h:], s8[h:], s16[h:])
    width = pick(*[float(w) for w in POOL_WINDOWS])
    cnt = jnp.minimum((t + 1).astype(F32), width)
    mixed = wsum / cnt - cur
    o_ref[0] = _dot(mixed.astype(BF16), w_ref[...]) * scale_ref[...]


def _pool(v, wbd, scale):
    b, t, c = v.shape
    tt = min(t, 512)
    r = tt // POOL_HALO
    return pl.pallas_call(
        functools.partial(_pool_kernel, tt=tt),
        grid=(b, t // tt),
        in_specs=[pl.BlockSpec((1, tt, c), lambda i, j: (i, j, 0)),
                  pl.BlockSpec((1, POOL_HALO, c), lambda i, j: (i, jnp.maximum(j * r - 1, 0), 0)),
                  pl.BlockSpec((c, c), lambda i, j: (0, 0)),
                  pl.BlockSpec((1, c), lambda i, j: (0, 0))],
        out_specs=pl.BlockSpec((1, tt, c), lambda i, j: (i, j, 0)),
        out_shape=jax.ShapeDtypeStruct((b, t, c), F32),
        compiler_params=_cparams(("parallel", "parallel")),
        name="pool_mixer",
    )(v, v, wbd, scale)


def _ret_consts():
    h, c = RET_HEADS, RET_CHUNK
    lg = np.log1p(-np.power(2.0, -5.0 - np.arange(h, dtype=np.float64)))
    i = np.arange(c, dtype=np.float64)
    diff = i[:, None] - i[None, :]
    dmask = np.where(diff >= 0, np.exp(lg[:, None, None] * np.maximum(diff, 0.0)), 0.0)
    xi = np.repeat(np.exp(lg[:, None] * (i + 1.0)).T, HEAD_DIM, axis=1)
    zeta = np.repeat(np.exp(lg[:, None] * (c - 1.0 - i)).T, HEAD_DIM, axis=1)
    gc = np.repeat(np.exp(lg * c), HEAD_DIM)[None, :]
    return (jnp.asarray(dmask, F32), jnp.asarray(xi, F32), jnp.asarray(zeta, F32), jnp.asarray(gc, F32))


def _ret_kernel(h_ref, cos_ref, sin_ref, dmask_ref, xi_ref, zeta_ref, gc_ref, gn_ref, o_ref, state_ref):
    @pl.when(pl.program_id(1) == 0)
    def _():
        state_ref[...] = jnp.zeros_like(state_ref)

    c = RET_CHUNK
    lane = lax.broadcasted_iota(jnp.int32, (c, LANES), 1)
    row = lax.broadcasted_iota(jnp.int32, (LANES, LANES), 0)
    col = lax.broadcasted_iota(jnp.int32, (LANES, LANES), 1)
    lo = lane < HEAD_DIM
    same_head = (row < HEAD_DIM) == (col < HEAD_DIM)
    states = [state_ref[j] for j in range(RET_WIDTH // LANES)]
    for u in range(h_ref.shape[1] // c):
        rs = slice(u * c, (u + 1) * c)
        cos = cos_ref[0, rs, :]
        sin = sin_ref[0, rs, :]
        for j in range(RET_WIDTH // LANES):
            sl = slice(j * LANES, (j + 1) * LANES)
            q = _rope(h_ref[0, rs, j * LANES:(j + 1) * LANES], cos, sin)
            k = _rope(h_ref[0, rs, RET_WIDTH + j * LANES:RET_WIDTH + (j + 1) * LANES], cos, sin) * (HEAD_DIM ** -0.5)
            v = h_ref[0, rs, 2 * RET_WIDTH + j * LANES:2 * RET_WIDTH + (j + 1) * LANES]
            g = h_ref[0, rs, 3 * RET_WIDTH + j * LANES:3 * RET_WIDTH + (j + 1) * LANES]
            kb = k.astype(BF16)
            vb = v.astype(BF16)
            o = _dot(q.astype(BF16), states[j].astype(BF16)) * xi_ref[:, sl]
            for hh in range(2):
                m = lo if hh == 0 else jnp.logical_not(lo)
                qm = jnp.where(m, q, 0.0).astype(BF16)
                s = _dot_nt(qm, kb) * dmask_ref[2 * j + hh]
                o = o + jnp.where(m, _dot(s.astype(BF16), vb), 0.0)
            kz = (k * zeta_ref[:, sl]).astype(BF16)
            kv = lax.dot_general(kz, vb, (((0,), (0,)), ((), ())), preferred_element_type=F32)
            states[j] = gc_ref[:, sl] * states[j] + jnp.where(same_head, kv, 0.0)
            s_lo = jnp.sum(jnp.where(lo, o, 0.0), axis=-1, keepdims=True)
            s_hi = jnp.sum(jnp.where(lo, 0.0, o), axis=-1, keepdims=True)
            d = o - jnp.where(lo, s_lo, s_hi) * (1.0 / HEAD_DIM)
            d2 = d * d
            v_lo = jnp.sum(jnp.where(lo, d2, 0.0), axis=-1, keepdims=True)
            v_hi = jnp.sum(jnp.where(lo, 0.0, d2), axis=-1, keepdims=True)
            var = jnp.where(lo, v_lo, v_hi) * (1.0 / HEAD_DIM)
            on = d * lax.rsqrt(var + GN_EPS) * gn_ref[:, sl]
            o_ref[0, rs, sl] = g * _sigmoid(g) * on
    for j, st in enumerate(states):
        state_ref[j] = st


def _retention(hret, cos, sin, gn_g):
    b, t, _ = hret.shape
    c = RET_CHUNK
    rows = min(t, RET_STEP_CHUNKS * c)
    dmask, xi, zeta, gc = _ret_consts()
    const = lambda shape: pl.BlockSpec(shape, lambda i, j: (0,) * len(shape))
    return pl.pallas_call(
        _ret_kernel,
        grid=(b, t // rows),
        in_specs=[pl.BlockSpec((1, rows, 4 * RET_WIDTH), lambda i, j: (i, j, 0)),
                  pl.BlockSpec((1, rows, LANES), lambda i, j: (i, j, 0)),
                  pl.BlockSpec((1, rows, LANES), lambda i, j: (i, j, 0)),
                  const((RET_HEADS, c, c)), const((c, RET_WIDTH)), const((c, RET_WIDTH)),
                  const((1, RET_WIDTH)), const((1, RET_WIDTH))],
        out_specs=pl.BlockSpec((1, rows, RET_WIDTH), lambda i, j: (i, j, 0)),
        out_shape=jax.ShapeDtypeStruct((b, t, RET_WIDTH), F32),
        scratch_shapes=[pltpu.VMEM((RET_WIDTH // LANES, LANES, LANES), F32)],
        compiler_params=_cparams(("parallel", "arbitrary")),
        name="retention",
    )(hret, cos, sin, dmask, xi, zeta, gc, gn_g)


def _compress_kernel(k_ref, v_ref, ptk, pbk, wtk, wbk, b1k, w2k, ptv, pbv, wtv, wbv, b1v, w2v,
                     cos_ref, sin_ref, kc_ref, vct_ref):
    def comp(x_ref, pt, pb, wt, wb, b1, w2):
        x = x_ref[0]
        top = _dot((x + pt[...]).astype(BF16), wt[...])
        bot = _dot((x + pb[...]).astype(BF16), wb[...])
        rows = bot.shape[0]
        pre = top + pltpu.roll(bot, rows - 1, 0) + b1[...]
        return _dot(_gelu_tanh(pre).astype(BF16), w2[...])

    kc = comp(k_ref, ptk, pbk, wtk, wbk, b1k, w2k)
    kc_ref[0] = _rope(kc, cos_ref[0], sin_ref[0]).astype(BF16)
    vct_ref[0] = jnp.transpose(comp(v_ref, ptv, pbv, wtv, wbv, b1v, w2v)).astype(BF16)


def _compress_weights(pos, w1, b1, w2):
    half = CMP_BLOCK // 2
    g = NSA_KV_HEADS
    w1r = w1.reshape(CMP_BLOCK, HEAD_DIM, CMP_HIDDEN)
    eye = jnp.eye(g, dtype=F32)

    def lay(wh):
        return jnp.einsum('ldh,ge->lgdeh', wh, eye).reshape(half * g * HEAD_DIM, g * CMP_HIDDEN)

    def tile_pos(p):
        return jnp.broadcast_to(p[:, None, :], (half, g, HEAD_DIM)).reshape(1, half * g * HEAD_DIM)

    w2bd = jnp.einsum('hd,ge->ghed', w2, eye).reshape(g * CMP_HIDDEN, g * HEAD_DIM)
    return (tile_pos(pos[:half]), tile_pos(pos[half:]), lay(w1r[:half]).astype(BF16), lay(w1r[half:]).astype(BF16),
            jnp.tile(b1, g)[None, :], w2bd.astype(BF16))


def _compress(kc_in, vc_in, wk, wv, cos_e, sin_e):
    b, nc, width = kc_in.shape
    const = lambda a: pl.BlockSpec(a.shape, lambda i: (0,) * a.ndim)
    blk = lambda w: pl.BlockSpec((1, nc, w), lambda i: (i, 0, 0))
    return pl.pallas_call(
        _compress_kernel,
        grid=(b,),
        in_specs=[blk(width), blk(width)] + [const(a) for a in wk] + [const(a) for a in wv]
                 + [blk(LANES), blk(LANES)],
        out_specs=[blk(NSA_KV_WIDTH), pl.BlockSpec((1, NSA_KV_WIDTH, nc), lambda i: (i, 0, 0))],
        out_shape=[jax.ShapeDtypeStruct((b, nc, NSA_KV_WIDTH), BF16),
                   jax.ShapeDtypeStruct((b, NSA_KV_WIDTH, nc), BF16)],
        compiler_params=_cparams(("parallel",)),
        name="nsa_compress",
    )(kc_in, vc_in, *wk, *wv, cos_e, sin_e)


def _kv_prep_kernel(kv_ref, cos_ref, sin_ref, ksa_ref, vst_ref, kw_ref, vwt_ref, *, tt):
    j = pl.program_id(1)
    cos = cos_ref[0]
    sin = sin_ref[0]
    w = NSA_KV_WIDTH
    lane = lax.broadcasted_iota(jnp.int32, (tt, LANES), 1)
    tpos = j * tt + lax.broadcasted_iota(jnp.int32, (tt, LANES), 0)
    onehot = jnp.where(lane - HEAD_DIM == tpos // SLC_BLOCK, 1.0, 0.0)
    lo = lane < HEAD_DIM
    ks = _rope(kv_ref[0, :, 0:w], cos, sin)
    ksa_ref[0, 0] = jnp.where(lo, ks, onehot).astype(BF16)
    ksa_ref[0, 1] = jnp.where(lo, pltpu.roll(ks, HEAD_DIM, 1), onehot).astype(BF16)
    kw_ref[0] = _rope(kv_ref[0, :, 2 * w:3 * w], cos, sin).astype(BF16)
    top = lax.broadcasted_iota(jnp.int32, (LANES, tt), 0) < HEAD_DIM

    def store_t(v, ref, chunk):
        vt = jnp.transpose(v)
        per_head = (jnp.where(top, vt, 1.0), jnp.where(top, pltpu.roll(vt, HEAD_DIM, 0), 1.0))
        for g, vg in enumerate(per_head):
            for u in range(tt // chunk):
                ref[0, g, u] = vg[:, u * chunk:(u + 1) * chunk].astype(BF16)

    store_t(kv_ref[0, :, w:2 * w], vst_ref, SEL_CHUNK)
    store_t(kv_ref[0, :, 3 * w:4 * w], vwt_ref, ATT_TQ)


def _kv_prep(kv4, cos, sin):
    b, t, _ = kv4.shape
    tt = min(t, 512)
    g = NSA_KV_HEADS
    blk = lambda w: pl.BlockSpec((1, tt, w), lambda i, j: (i, j, 0))
    return pl.pallas_call(
        functools.partial(_kv_prep_kernel, tt=tt),
        grid=(b, t // tt),
        in_specs=[blk(4 * NSA_KV_WIDTH), blk(LANES), blk(LANES)],
        out_specs=[pl.BlockSpec((1, g, tt, LANES), lambda i, j: (i, 0, j, 0)),
                   pl.BlockSpec((1, g, tt // SEL_CHUNK, LANES, SEL_CHUNK), lambda i, j: (i, 0, j, 0, 0)),
                   blk(NSA_KV_WIDTH),
                   pl.BlockSpec((1, g, tt // ATT_TQ, LANES, ATT_TQ), lambda i, j: (i, 0, j, 0, 0))],
        out_shape=[jax.ShapeDtypeStruct((b, g, t, LANES), BF16),
                   jax.ShapeDtypeStruct((b, g, t // SEL_CHUNK, LANES, SEL_CHUNK), BF16),
                   jax.ShapeDtypeStruct((b, t, NSA_KV_WIDTH), BF16),
                   jax.ShapeDtypeStruct((b, g, t // ATT_TQ, LANES, ATT_TQ), BF16)],
        compiler_params=_cparams(("parallel", "parallel")),
        name="nsa_kv_prep",
    )(kv4, cos, sin)


def _queries_t(qt, g, other, q_first):
    cols = []
    for r in range(NSA_REP):
        hq = g * NSA_REP + r
        piece = qt[hq * HEAD_DIM:(hq + 1) * HEAD_DIM]
        cols.append(jnp.concatenate([piece, other] if q_first else [other, piece], axis=0))
    return jnp.concatenate(cols, axis=1)


def _gated_heads_from_t(pieces, gate_ref, branch):
    sg = _sigmoid(jnp.transpose(gate_ref[0]))
    rows = [hq * N_BRANCH + branch for hq in range(NSA_HEADS)]
    gated = [p * sg[r:r + 1] for p, r in zip(pieces, rows)]
    slabs = [jnp.transpose(jnp.concatenate(gated[2 * j:2 * j + 2], axis=0)) for j in range(NSA_HEADS // 2)]
    return jnp.concatenate(slabs, axis=1)


def _normalised_heads(acc, tq):
    ot = acc[0:HEAD_DIM] * (1.0 / acc[HEAD_DIM:HEAD_DIM + 1])
    return [ot[:, r * tq:(r + 1) * tq] for r in range(NSA_REP)]


def _cmpattn_kernel(q_ref, gate_ref, cos_ref, sin_ref, kc_ref, vct_ref, ovt_ref, qt_ref, o_ref, nsel_ref, *, tq, n_slc):
    i = pl.program_id(1)
    cos = cos_ref[0]
    sin = sin_ref[0]
    qt = jnp.concatenate([jnp.transpose(_rope(q_ref[0, :, j * LANES:(j + 1) * LANES], cos, sin) * Q_SCALE)
                          for j in range(NSA_WIDTH // LANES)], axis=0).astype(BF16)
    qt_ref[0] = qt
    kc = kc_ref[0]
    vct = vct_ref[0]
    nc = kc.shape[0]
    n = lax.broadcasted_iota(jnp.int32, (nc, NSA_REP * tq), 0)
    t = i * tq + (lax.broadcasted_iota(jnp.int32, (nc, NSA_REP * tq), 1) & (tq - 1))
    visible = n * CMP_STRIDE + (CMP_BLOCK - 1) <= t
    zeros = jnp.zeros((HEAD_DIM, tq), BF16)
    blk = lax.broadcasted_iota(jnp.int32, (n_slc, tq), 0)
    tl = i * tq + lax.broadcasted_iota(jnp.int32, (n_slc, tq), 1)
    cur = tl // SLC_BLOCK
    forced = (blk == 0) | (blk == cur) | (blk == cur - 1)
    valid = blk * SLC_BLOCK <= tl
    ovt = ovt_ref[...]
    pieces = []
    scores = []
    for g in range(NSA_KV_HEADS):
        s = jnp.where(visible, _dot(kc, _queries_t(qt, g, zeros, g == 0)), NEG_INF)
        e = jnp.exp2(s - jnp.max(s, axis=0, keepdims=True))
        p = jnp.where(visible, e * (1.0 / jnp.sum(e, axis=0, keepdims=True)), 0.0)
        ot = _dot(vct, p.astype(BF16))
        pieces.extend(ot[g * HEAD_DIM:(g + 1) * HEAD_DIM, r * tq:(r + 1) * tq] for r in range(NSA_REP))
        psum = p[:, 0:tq] + p[:, tq:2 * tq] + p[:, 2 * tq:3 * tq]
        hi = psum.astype(BF16)
        rest = psum - hi.astype(F32)
        mid = rest.astype(BF16)
        lo = (rest - mid.astype(F32)).astype(BF16)
        imp = _dot(ovt, hi) + _dot(ovt, mid) + _dot(ovt, lo)
        scores.append(jnp.where(valid, jnp.where(forced, FORCE_SCORE, imp), -1.0))
    o_ref[0] = _gated_heads_from_t(pieces, gate_ref, 0)

    n_live = ((i + 1) * tq) // SLC_BLOCK
    n_cls = n_slc // RANK_STEP
    cls = jnp.minimum((n_live - 1) // RANK_STEP, n_cls - 1)
    for c in range(n_cls):
        @pl.when(cls == c)
        def _(c=c):
            for g in range(NSA_KV_HEADS):
                nsel_ref[0, g] = _selection_bias(scores[g], (c + 1) * RANK_STEP, min(SLC_TOP, n_slc), tq)


def _selection_bias(score, nb, top, tq):
    sub = lax.broadcasted_iota(jnp.int32, (SUBLANES, tq), 0)
    ties = [jnp.where(sub > k, 1.0, 0.0) for k in range(SUBLANES)]
    groups = [score[SUBLANES * r:SUBLANES * (r + 1)] for r in range(nb // SUBLANES)]
    ranks = [jnp.zeros((SUBLANES, tq), F32) for _ in groups]
    for jp in range(nb):
        rowv = score[jp:jp + 1, :]
        for r, grp in enumerate(groups):
            if SUBLANES * r > jp:
                inc = jnp.where(rowv >= grp, 1.0, 0.0)
            elif SUBLANES * r + SUBLANES - 1 < jp:
                inc = jnp.where(rowv > grp, 1.0, 0.0)
            else:
                inc = jnp.where(rowv > grp, 1.0, jnp.where(rowv == grp, ties[jp - SUBLANES * r], 0.0))
            ranks[r] = ranks[r] + inc
    bias = jnp.where(jnp.concatenate(ranks, axis=0) < float(top), 0.0, NEG_INF)
    if nb < SEL_ROWS:
        bias = jnp.concatenate([bias, jnp.full((SEL_ROWS - nb, tq), NEG_INF, F32)], axis=0)
    return bias.astype(BF16)


def _cmpattn(q, gate, cos, sin, kc, vct, ovt):
    b, t, _ = q.shape
    nc = kc.shape[1]
    tq = ATT_TQ
    n_slc = t // SLC_BLOCK
    assert n_slc <= SEL_ROWS and n_slc % RANK_STEP == 0
    blk = lambda w: pl.BlockSpec((1, tq, w), lambda i, j: (i, j, 0))
    return pl.pallas_call(
        functools.partial(_cmpattn_kernel, tq=tq, n_slc=n_slc),
        grid=(b, t // tq),
        in_specs=[blk(NSA_WIDTH), blk(LANES), blk(LANES), blk(LANES),
                  pl.BlockSpec((1, nc, NSA_KV_WIDTH), lambda i, j: (i, 0, 0)),
                  pl.BlockSpec((1, NSA_KV_WIDTH, nc), lambda i, j: (i, 0, 0)),
                  pl.BlockSpec(ovt.shape, lambda i, j: (0, 0))],
        out_specs=[pl.BlockSpec((1, NSA_WIDTH, tq), lambda i, j: (i, 0, j)),
                   blk(NSA_WIDTH),
                   pl.BlockSpec((1, NSA_KV_HEADS, SEL_ROWS, tq), lambda i, j: (i, 0, 0, j))],
        out_shape=[jax.ShapeDtypeStruct((b, NSA_WIDTH, t), BF16),
                   jax.ShapeDtypeStruct((b, t, NSA_WIDTH), F32),
                   jax.ShapeDtypeStruct((b, NSA_KV_HEADS, SEL_ROWS, t), BF16)],
        compiler_params=_cparams(("parallel", "parallel")),
        name="nsa_compressed_attn_select",
    )(q, gate, cos, sin, kc, vct, ovt)


def _slc_kernel(qt_ref, nsel_ref, gate_ref, ksa_ref, vst_ref, o_ref, *, tq, kc):
    i = pl.program_id(1)
    qt = qt_ref[0]
    c_diag = (i * tq) // kc
    kpos = lax.broadcasted_iota(jnp.int32, (kc, NSA_REP * tq), 0)
    tpos = i * tq + (lax.broadcasted_iota(jnp.int32, (kc, NSA_REP * tq), 1) & (tq - 1))
    qas = [_queries_t(qt, g, nsel_ref[0, g], True) for g in range(NSA_KV_HEADS)]

    def step(c, carry, causal):
        out = []
        for g, (m, acc) in enumerate(carry):
            k = ksa_ref[0, g, pl.ds(pl.multiple_of(c * kc, kc), kc), :]
            s = _dot(k, qas[g])
            if causal:
                s = jnp.where(c * kc + kpos <= tpos, s, NEG_INF)
            m_new = jnp.maximum(m, jnp.max(s, axis=0, keepdims=True))
            p = jnp.exp2(s - m_new).astype(BF16)
            out.append((m_new, jnp.exp2(m - m_new) * acc + _dot(vst_ref[0, g, c], p)))
        return tuple(out)

    init = tuple((jnp.full((1, NSA_REP * tq), NEG_INF, F32), jnp.zeros((LANES, NSA_REP * tq), F32))
                 for _ in range(NSA_KV_HEADS))
    carry = lax.fori_loop(0, c_diag, lambda c, carry: step(c, carry, False), init)
    carry = step(c_diag, carry, True)
    pieces = []
    for _, acc in carry:
        pieces.extend(_normalised_heads(acc, tq))
    o_ref[0] = _gated_heads_from_t(pieces, gate_ref, 1)


def _slc(qt, nsel, gate, ksa, vst):
    b, _, t = qt.shape
    tq = SEL_TQ
    kc = SEL_CHUNK
    g = NSA_KV_HEADS
    return pl.pallas_call(
        functools.partial(_slc_kernel, tq=tq, kc=kc),
        grid=(b, t // tq),
        in_specs=[pl.BlockSpec((1, NSA_WIDTH, tq), lambda i, j: (i, 0, j)),
                  pl.BlockSpec((1, g, SEL_ROWS, tq), lambda i, j: (i, 0, 0, j)),
                  pl.BlockSpec((1, tq, LANES), lambda i, j: (i, j, 0)),
                  pl.BlockSpec((1, g, t, LANES), lambda i, j: (i, 0, 0, 0)),
                  pl.BlockSpec((1, g, t // kc, LANES, kc), lambda i, j: (i, 0, 0, 0, 0))],
        out_specs=pl.BlockSpec((1, tq, NSA_WIDTH), lambda i, j: (i, j, 0)),
        out_shape=jax.ShapeDtypeStruct((b, t, NSA_WIDTH), F32),
        compiler_params=_cparams(("parallel", "parallel")),
        name="nsa_selected_attn",
    )(qt, nsel, gate, ksa, vst)


def _win_kernel(qt_ref, gate_ref, kw_ref, vwt_ref, o_ref, *, tq, nblk, t_total):
    i = pl.program_id(1)
    qt = qt_ref[0]
    kb = ATT_TQ
    span = nblk * kb
    first_blk = jnp.clip(i * (tq // kb) - WINDOW // kb, 0, t_total // kb - nblk)
    start = pl.multiple_of(first_blk * kb, kb)
    k = kw_ref[0, pl.ds(start, span), :]
    d = (i * tq + (lax.broadcasted_iota(jnp.int32, (span, NSA_REP * tq), 1) & (tq - 1))
         - (start + lax.broadcasted_iota(jnp.int32, (span, NSA_REP * tq), 0)))
    keep = (d >= 0) & (d < WINDOW)
    zeros = jnp.zeros((HEAD_DIM, tq), BF16)
    pieces = []
    for g in range(NSA_KV_HEADS):
        s = jnp.where(keep, _dot(k, _queries_t(qt, g, zeros, g == 0)), NEG_INF)
        p = jnp.exp2(s - jnp.max(s, axis=0, keepdims=True)).astype(BF16)
        vt = jnp.concatenate([vwt_ref[0, g, first_blk + u] for u in range(nblk)], axis=1)
        pieces.extend(_normalised_heads(_dot(vt, p), tq))
    o_ref[0] = _gated_heads_from_t(pieces, gate_ref, 2)


def _win(qt, gate, kw, vwt):
    b, _, t = qt.shape
    tq = min(t, WIN_TQ)
    kb = ATT_TQ
    nblk = min(t // kb, (WINDOW + tq) // kb)
    g = NSA_KV_HEADS
    return pl.pallas_call(
        functools.partial(_win_kernel, tq=tq, nblk=nblk, t_total=t),
        grid=(b, t // tq),
        in_specs=[pl.BlockSpec((1, NSA_WIDTH, tq), lambda i, j: (i, 0, j)),
                  pl.BlockSpec((1, tq, LANES), lambda i, j: (i, j, 0)),
                  pl.BlockSpec((1, t, NSA_KV_WIDTH), lambda i, j: (i, 0, 0)),
                  pl.BlockSpec((1, g, t // kb, LANES, kb), lambda i, j: (i, 0, 0, 0, 0))],
        out_specs=pl.BlockSpec((1, tq, NSA_WIDTH), lambda i, j: (i, j, 0)),
        out_shape=jax.ShapeDtypeStruct((b, t, NSA_WIDTH), F32),
        compiler_params=_cparams(("parallel", "parallel")),
        name="nsa_window_attn",
    )(qt, gate, kw, vwt)


def _outproj_kernel(x_ref, ya_ref, yb_ref, oc_ref, os_ref, ow_ref, w_ref, g_ref, b_ref, o_ref, *, alpha):
    yc = oc_ref[...] + os_ref[...] + ow_ref[...]
    y = jnp.concatenate([ya_ref[...], yb_ref[...], yc], axis=1).astype(BF16)
    o_ref[...] = _layernorm(alpha * x_ref[...] + _dot(y, w_ref[...]), g_ref[...], b_ref[...])


def _outproj(x2, ya, yb, oc, osl, ow, w, g, bta, alpha):
    n, d = x2.shape
    tm = 512
    row = lambda w_: pl.BlockSpec((tm, w_), lambda i: (i, 0))
    const = lambda shape: pl.BlockSpec(shape, lambda i: (0, 0))
    return pl.pallas_call(
        functools.partial(_outproj_kernel, alpha=alpha),
        grid=(n // tm,),
        in_specs=[row(d), row(POOL_WIDTH), row(RET_WIDTH), row(NSA_WIDTH), row(NSA_WIDTH), row(NSA_WIDTH),
                  const((d, d)), const((1, d)), const((1, d))],
        out_specs=row(d),
        out_shape=jax.ShapeDtypeStruct((n, d), F32),
        compiler_params=_cparams(("parallel",)),
        name="out_proj_ln",
    )(x2, ya, yb, oc, osl, ow, w, g, bta)


def _ffn_kernel(x_ref, xh_ref, wg_ref, wu_ref, cw_ref, cb_ref, wd_ref, g_ref, b_ref, o_ref, xb_ref, act_ref,
                *, alpha, fc):
    j = pl.program_id(1)
    h = CONV_HALO
    xb_ref[0:h, :] = jnp.where(j > 0, xh_ref[0], 0.0).astype(BF16)
    xb_ref[h:, :] = x_ref[0].astype(BF16)
    for c in range(0, wg_ref.shape[1], fc):
        hg = _dot(xb_ref[...], wg_ref[:, c:c + fc])
        up = _dot(xb_ref[h:, :], wu_ref[:, c:c + fc])
        hc = (cb_ref[:, c:c + fc] + pltpu.roll(hg, 2, 0)[h:] * cw_ref[0:1, c:c + fc]
              + pltpu.roll(hg, 1, 0)[h:] * cw_ref[1:2, c:c + fc] + hg[h:] * cw_ref[2:3, c:c + fc])
        act_ref[:, c:c + fc] = (_gelu_tanh(hc) * up).astype(BF16)
    o_ref[0] = _layernorm(alpha * x_ref[0] + _dot(act_ref[...], wd_ref[...]), g_ref[...], b_ref[...])


def _ffn(x, wg, wu, cw, cb, wd, g, bta, alpha):
    b, t, d = x.shape
    dff = wg.shape[1]
    tm = min(t, 512)
    fc = 2 * LANES
    r = tm // CONV_HALO
    const = lambda shape: pl.BlockSpec(shape, lambda i, j: (0, 0), pipeline_mode=pl.Buffered(1))
    return pl.pallas_call(
        functools.partial(_ffn_kernel, alpha=alpha, fc=fc),
        grid=(b, t // tm),
        in_specs=[pl.BlockSpec((1, tm, d), lambda i, j: (i, j, 0)),
                  pl.BlockSpec((1, CONV_HALO, d), lambda i, j: (i, jnp.maximum(j * r - 1, 0), 0)),
                  const((d, dff)), const((d, dff)), const((CONV_WIDTH, dff)), const((1, dff)), const((dff, d)),
                  const((1, d)), const((1, d))],
        out_specs=pl.BlockSpec((1, tm, d), lambda i, j: (i, j, 0)),
        out_shape=jax.ShapeDtypeStruct((b, t, d), F32),
        scratch_shapes=[pltpu.VMEM((CONV_HALO + tm, d), BF16), pltpu.VMEM((tm, dff), BF16)],
        compiler_params=_cparams(("parallel", "parallel")),
        name="conv_ffn_ln",
    )(x, x, wg, wu, cw, cb, wd, g, bta)


def _overlap_t(nc, n_slc):
    ci = np.arange(nc)[None, :]
    sj = np.arange(n_slc)[:, None]
    ov = np.clip(np.minimum(ci * CMP_STRIDE + CMP_BLOCK, (sj + 1) * SLC_BLOCK)
                 - np.maximum(ci * CMP_STRIDE, sj * SLC_BLOCK), 0, None).astype(np.float32) / CMP_STRIDE
    ov[:, nc - 1] = 0.0
    return jnp.asarray(ov, BF16)


def kernel(x, positions, w_in, w_out, pool_w, pool_scale, ret_gn_g, cmp_pos_k, cmp_w1_k, cmp_b1_k, cmp_w2_k,
           cmp_pos_v, cmp_w1_v, cmp_b1_v, cmp_w2_v, ffn_w_gate, ffn_w_up, ffn_conv_w, ffn_conv_b, ffn_w_down,
           ln1_g, ln1_b, ln2_g, ln2_b):
    b, t, d = x.shape
    depth = w_in.shape[0]
    alpha = float((2 * depth) ** 0.25)
    n = b * t
    nc = t // CMP_STRIDE
    n_slc = t // SLC_BLOCK

    inv = ROPE_THETA ** (-jnp.arange(0, HEAD_DIM, 2, dtype=F32) / HEAD_DIM)
    inv = jnp.tile(inv, LANES // (HEAD_DIM // 2))[None, :]
    cos, sin = _rope_tables(positions, inv)
    ends = jnp.minimum(jnp.arange(nc) * CMP_STRIDE + CMP_BLOCK - 1, t - 1)
    cos_e, sin_e = _rope_tables(positions[:, ends], inv)
    ovt = _overlap_t(nc, n_slc)
    eye_g = jnp.eye(len(POOL_WINDOWS), dtype=F32)

    for l in range(depth):
        w_in_p = jnp.pad(w_in[l], ((0, 0), (0, IN_PAD - IN_WIDTH))).astype(BF16)
        x2 = x.reshape(n, d)
        v_pool, h_ret, q_nsa, k_cmp, v_cmp, kv4, gate = _inproj(x2, w_in_p)

        wbd = jnp.einsum('gcd,ge->gced', pool_w[l], eye_g).reshape(POOL_WIDTH, POOL_WIDTH).astype(BF16)
        y_a = _pool(v_pool.reshape(b, t, POOL_WIDTH), wbd, pool_scale[l][None, :])

        y_b = _retention(h_ret.reshape(b, t, 4 * RET_WIDTH), cos, sin, ret_gn_g[l][None, :])

        wk = _compress_weights(cmp_pos_k[l], cmp_w1_k[l], cmp_b1_k[l], cmp_w2_k[l])
        wv = _compress_weights(cmp_pos_v[l], cmp_w1_v[l], cmp_b1_v[l], cmp_w2_v[l])
        kc, vct = _compress(k_cmp.reshape(b, nc, CMP_STRIDE * NSA_KV_WIDTH),
                            v_cmp.reshape(b, nc, CMP_STRIDE * NSA_KV_WIDTH), wk, wv, cos_e, sin_e)
        ksa, vst, kw, vwt = _kv_prep(kv4.reshape(b, t, 4 * NSA_KV_WIDTH), cos, sin)
        gate3 = gate.reshape(b, t, LANES)
        qt, o_cmp, nsel = _cmpattn(q_nsa.reshape(b, t, NSA_WIDTH), gate3, cos, sin, kc, vct, ovt)
        o_slc = _slc(qt, nsel, gate3, ksa, vst)
        o_win = _win(qt, gate3, kw, vwt)

        x2 = _outproj(x2, y_a.reshape(n, POOL_WIDTH), y_b.reshape(n, RET_WIDTH), o_cmp.reshape(n, NSA_WIDTH),
                      o_slc.reshape(n, NSA_WIDTH), o_win.reshape(n, NSA_WIDTH),
                      w_out[l].astype(BF16), ln1_g[l][None, :], ln1_b[l][None, :], alpha)
        x = _ffn(x2.reshape(b, t, d), ffn_w_gate[l].astype(BF16), ffn_w_up[l].astype(BF16), ffn_conv_w[l],
                 ffn_conv_b[l][None, :], ffn_w_down[l].astype(BF16), ln2_g[l][None, :], ln2_b[l][None, :], alpha)
    return x
```

```python
import functools
import math

import jax
import jax.numpy as jnp
import numpy as np
from jax import lax
from jax.experimental import pallas as pl
from jax.experimental.pallas import tpu as pltpu

F32 = jnp.float32
BF16 = jnp.bfloat16

HEAD_DIM = 64
POOL_WINDOWS = (2, 4, 8, 16)
POOL_GROUP_WIDTH = 64
POOL_WIDTH = POOL_GROUP_WIDTH * len(POOL_WINDOWS)
POOL_HALO = 16
RET_HEADS = 6
RET_WIDTH = RET_HEADS * HEAD_DIM
RET_CHUNK = 128
NSA_HEADS = 6
NSA_WIDTH = NSA_HEADS * HEAD_DIM
NSA_KV_HEADS = 2
NSA_REP = NSA_HEADS // NSA_KV_HEADS
NSA_KV_WIDTH = NSA_KV_HEADS * HEAD_DIM
CMP_BLOCK = 32
CMP_STRIDE = 16
CMP_HIDDEN = 128
SLC_BLOCK = 64
SLC_TOP = 16
WINDOW = 512
N_BRANCH = 3
FORCE_SCORE = 1e6
CONV_WIDTH = 3
CONV_HALO = 16
ROPE_THETA = 10000.0
LN_EPS = 1e-5
GN_EPS = 1e-5
NEG_INF = -1e30
Q_SCALE = HEAD_DIM ** -0.5 * math.log2(math.e)

LANES = 128
SUBLANES = 8
VMEM_LIMIT = 56 * 1024 * 1024

ATT_TQ = 128
SEL_TQ = 512
SEL_CHUNK = 512
SEL_ROWS = LANES - HEAD_DIM
RANK_STEP = 16
WIN_TQ = 256
RET_STEP_CHUNKS = 4

OFF_POOL = 0
OFF_RET = OFF_POOL + POOL_WIDTH
OFF_QNSA = OFF_RET + 4 * RET_WIDTH
OFF_KCMP = OFF_QNSA + NSA_WIDTH
OFF_VCMP = OFF_KCMP + NSA_KV_WIDTH
OFF_KV4 = OFF_VCMP + NSA_KV_WIDTH
OFF_GATE = OFF_KV4 + 4 * NSA_KV_WIDTH
IN_WIDTH = OFF_GATE + NSA_HEADS * N_BRANCH
IN_PAD = OFF_GATE + LANES


def _cparams(sem, flags=None):
    return pltpu.CompilerParams(dimension_semantics=sem, vmem_limit_bytes=VMEM_LIMIT, flags=flags)


def _rope(x, cos, sin_signed):
    lane = lax.broadcasted_iota(jnp.int32, x.shape, 1)
    first = (lane & (HEAD_DIM - 1)) < HEAD_DIM // 2
    partner = jnp.where(first, pltpu.roll(x, LANES - HEAD_DIM // 2, 1), pltpu.roll(x, HEAD_DIM // 2, 1))
    return x * cos + partner * sin_signed


def _gelu_tanh(x):
    return 0.5 * x * (1.0 + jnp.tanh(math.sqrt(2.0 / math.pi) * (x + 0.044715 * (x * x * x))))


def _sigmoid(x):
    return 1.0 / (1.0 + jnp.exp(-x))


def _layernorm(y, g, b):
    mu = jnp.mean(y, axis=-1, keepdims=True)
    d = y - mu
    var = jnp.mean(d * d, axis=-1, keepdims=True)
    return d * lax.rsqrt(var + LN_EPS) * g + b


def _dot(a, b):
    return jnp.dot(a, b, preferred_element_type=F32)


def _dot_nt(a, b):
    return lax.dot_general(a, b, (((1,), (1,)), ((), ())), preferred_element_type=F32)


def _rope_table_kernel(pos_ref, inv_ref, cos_ref, sin_ref):
    ang = pos_ref[0].astype(F32) * inv_ref[...]
    lane = lax.broadcasted_iota(jnp.int32, ang.shape, 1)
    first = (lane & (HEAD_DIM - 1)) < HEAD_DIM // 2
    s = jnp.sin(ang)
    cos_ref[0] = jnp.cos(ang)
    sin_ref[0] = jnp.where(first, -s, s)


def _rope_tables(pos, inv):
    b, n = pos.shape
    tt = min(n, 512)
    shp = jax.ShapeDtypeStruct((b, n, LANES), F32)
    return pl.pallas_call(
        _rope_table_kernel,
        grid=(b, n // tt),
        in_specs=[pl.BlockSpec((1, tt, 1), lambda i, j: (i, j, 0)),
                  pl.BlockSpec((1, LANES), lambda i, j: (0, 0))],
        out_specs=[pl.BlockSpec((1, tt, LANES), lambda i, j: (i, j, 0))] * 2,
        out_shape=[shp, shp],
        compiler_params=_cparams(("parallel", "parallel")),
        name="rope_tables",
    )(pos.reshape(b, n, 1), inv)


_IN_OUTS = ((OFF_POOL, POOL_WIDTH), (OFF_RET, 4 * RET_WIDTH), (OFF_QNSA, NSA_WIDTH),
            (OFF_KCMP, NSA_KV_WIDTH), (OFF_VCMP, NSA_KV_WIDTH), (OFF_KV4, 4 * NSA_KV_WIDTH),
            (OFF_GATE, LANES))


def _inproj_kernel(x_ref, w_ref, *o_refs):
    xb = x_ref[...].astype(BF16)
    for c in range(0, IN_PAD, 2 * LANES):
        y = _dot(xb, w_ref[:, c:c + 2 * LANES])
        for half in range(2):
            col = c + half * LANES
            o_ref, off = next((r, o) for r, (o, wd) in zip(o_refs, _IN_OUTS) if o <= col < o + wd)
            o_ref[:, col - off:col - off + LANES] = y[:, half * LANES:(half + 1) * LANES]


def _inproj(x2, w):
    n, d = x2.shape
    tm = 512
    return pl.pallas_call(
        _inproj_kernel,
        grid=(n // tm,),
        in_specs=[pl.BlockSpec((tm, d), lambda i: (i, 0)),
                  pl.BlockSpec((d, IN_PAD), lambda i: (0, 0))],
        out_specs=[pl.BlockSpec((tm, wd), lambda i: (i, 0)) for _, wd in _IN_OUTS],
        out_shape=[jax.ShapeDtypeStruct((n, wd), F32) for _, wd in _IN_OUTS],
        compiler_params=_cparams(("parallel",)),
        name="in_proj",
    )(x2, w)


def _pool_kernel(cur_ref, prev_ref, w_ref, scale_ref, o_ref, *, tt):
    i = pl.program_id(1)
    cur = cur_ref[0]
    prev = jnp.where(i > 0, prev_ref[0], 0.0)
    cat = jnp.concatenate([prev, cur], axis=0)
    s2 = cat + pltpu.roll(cat, 1, 0)
    s4 = s2 + pltpu.roll(s2, 2, 0)
    s8 = s4 + pltpu.roll(s4, 4, 0)
    s16 = s8 + pltpu.roll(s8, 8, 0)
    grp = lax.broadcasted_iota(jnp.int32, cur.shape, 1) // POOL_GROUP_WIDTH
    t = i * tt + lax.broadcasted_iota(jnp.int32, cur.shape, 0)

    def pick(a, b, c, d):
        return jnp.where(grp == 0, a, jnp.where(grp == 1, b, jnp.where(grp == 2, c, d)))

    h = POOL_HALO
    wsum = pick(s2[h:], s4[h:], s8[h:], s16[h:])
    width = pick(*[float(w) for w in POOL_WINDOWS])
    cnt = jnp.minimum((t + 1).astype(F32), width)
    mixed = wsum / cnt - cur
    o_ref[0] = _dot(mixed.astype(BF16), w_ref[...]) * scale_ref[...]


def _pool(v, wbd, scale):
    b, t, c = v.shape
    tt = min(t, 512)
    r = tt // POOL_HALO
    return pl.pallas_call(
        functools.partial(_pool_kernel, tt=tt),
        grid=(b, t // tt),
        in_specs=[pl.BlockSpec((1, tt, c), lambda i, j: (i, j, 0)),
                  pl.BlockSpec((1, POOL_HALO, c), lambda i, j: (i, jnp.maximum(j * r - 1, 0), 0)),
                  pl.BlockSpec((c, c), lambda i, j: (0, 0)),
                  pl.BlockSpec((1, c), lambda i, j: (0, 0))],
        out_specs=pl.BlockSpec((1, tt, c), lambda i, j: (i, j, 0)),
        out_shape=jax.ShapeDtypeStruct((b, t, c), F32),
        compiler_params=_cparams(("parallel", "parallel")),
        name="pool_mixer",
    )(v, v, wbd, scale)


def _ret_consts():
    h, c = RET_HEADS, RET_CHUNK
    lg = np.log1p(-np.power(2.0, -5.0 - np.arange(h, dtype=np.float64)))
    i = np.arange(c, dtype=np.float64)
    diff = i[:, None] - i[None, :]
    dmask = np.where(diff >= 0, np.exp(lg[:, None, None] * np.maximum(diff, 0.0)), 0.0)
    xi = np.repeat(np.exp(lg[:, None] * (i + 1.0)).T, HEAD_DIM, axis=1)
    zeta = np.repeat(np.exp(lg[:, None] * (c - 1.0 - i)).T, HEAD_DIM, axis=1)
    gc = np.repeat(np.exp(lg * c), HEAD_DIM)[None, :]
    return (jnp.asarray(dmask, F32), jnp.asarray(xi, F32), jnp.asarray(zeta, F32), jnp.asarray(gc, F32))


def _ret_kernel(h_ref, cos_ref, sin_ref, dmask_ref, xi_ref, zeta_ref, gc_ref, gn_ref, o_ref, state_ref):
    @pl.when(pl.program_id(1) == 0)
    def _():
        state_ref[...] = jnp.zeros_like(state_ref)

    c = RET_CHUNK
    lane = lax.broadcasted_iota(jnp.int32, (c, LANES), 1)
    row = lax.broadcasted_iota(jnp.int32, (LANES, LANES), 0)
    col = lax.broadcasted_iota(jnp.int32, (LANES, LANES), 1)
    lo = lane < HEAD_DIM
    same_head = (row < HEAD_DIM) == (col < HEAD_DIM)
    states = [state_ref[j] for j in range(RET_WIDTH // LANES)]
    for u in range(h_ref.shape[1] // c):
        rs = slice(u * c, (u + 1) * c)
        cos = cos_ref[0, rs, :]
        sin = sin_ref[0, rs, :]
        for j in range(RET_WIDTH // LANES):
            sl = slice(j * LANES, (j + 1) * LANES)
            q = _rope(h_ref[0, rs, j * LANES:(j + 1) * LANES], cos, sin)
            k = _rope(h_ref[0, rs, RET_WIDTH + j * LANES:RET_WIDTH + (j + 1) * LANES], cos, sin) * (HEAD_DIM ** -0.5)
            v = h_ref[0, rs, 2 * RET_WIDTH + j * LANES:2 * RET_WIDTH + (j + 1) * LANES]
            g = h_ref[0, rs, 3 * RET_WIDTH + j * LANES:3 * RET_WIDTH + (j + 1) * LANES]
            kb = k.astype(BF16)
            vb = v.astype(BF16)
            o = _dot(q.astype(BF16), states[j].astype(BF16)) * xi_ref[:, sl]
            for hh in range(2):
                m = lo if hh == 0 else jnp.logical_not(lo)
                qm = jnp.where(m, q, 0.0).astype(BF16)
                s = _dot_nt(qm, kb) * dmask_ref[2 * j + hh]
                o = o + jnp.where(m, _dot(s.astype(BF16), vb), 0.0)
            kz = (k * zeta_ref[:, sl]).astype(BF16)
            kv = lax.dot_general(kz, vb, (((0,), (0,)), ((), ())), preferred_element_type=F32)
            states[j] = gc_ref[:, sl] * states[j] + jnp.where(same_head, kv, 0.0)
            s_lo = jnp.sum(jnp.where(lo, o, 0.0), axis=-1, keepdims=True)
            s_hi = jnp.sum(jnp.where(lo, 0.0, o), axis=-1, keepdims=True)
            d = o - jnp.where(lo, s_lo, s_hi) * (1.0 / HEAD_DIM)
            d2 = d * d
            v_lo = jnp.sum(jnp.where(lo, d2, 0.0), axis=-1, keepdims=True)
            v_hi = jnp.sum(jnp.where(lo, 0.0, d2), axis=-1, keepdims=True)
            var = jnp.where(lo, v_lo, v_hi) * (1.0 / HEAD_DIM)
            on = d * lax.rsqrt(var + GN_EPS) * gn_ref[:, sl]
            o_ref[0, rs, sl] = g * _sigmoid(g) * on
    for j, st in enumerate(states):
        state_ref[j] = st


def _retention(hret, cos, sin, gn_g):
    b, t, _ = hret.shape
    c = RET_CHUNK
    rows = min(t, RET_STEP_CHUNKS * c)
    dmask, xi, zeta, gc = _ret_consts()
    const = lambda shape: pl.BlockSpec(shape, lambda i, j: (0,) * len(shape))
    return pl.pallas_call(
        _ret_kernel,
        grid=(b, t // rows),
        in_specs=[pl.BlockSpec((1, rows, 4 * RET_WIDTH), lambda i, j: (i, j, 0)),
                  pl.BlockSpec((1, rows, LANES), lambda i, j: (i, j, 0)),
                  pl.BlockSpec((1, rows, LANES), lambda i, j: (i, j, 0)),
                  const((RET_HEADS, c, c)), const((c, RET_WIDTH)), const((c, RET_WIDTH)),
                  const((1, RET_WIDTH)), const((1, RET_WIDTH))],
        out_specs=pl.BlockSpec((1, rows, RET_WIDTH), lambda i, j: (i, j, 0)),
        out_shape=jax.ShapeDtypeStruct((b, t, RET_WIDTH), F32),
        scratch_shapes=[pltpu.VMEM((RET_WIDTH // LANES, LANES, LANES), F32)],
        compiler_params=_cparams(("parallel", "arbitrary")),
        name="retention",
    )(hret, cos, sin, dmask, xi, zeta, gc, gn_g)


def _compress_kernel(k_ref, v_ref, ptk, pbk, wtk, wbk, b1k, w2k, ptv, pbv, wtv, wbv, b1v, w2v,
                     cos_ref, sin_ref, kc_ref, vct_ref):
    def comp(x_ref, pt, pb, wt, wb, b1, w2):
        x = x_ref[0]
        top = _dot((x + pt[...]).astype(BF16), wt[...])
        bot = _dot((x + pb[...]).astype(BF16), wb[...])
        rows = bot.shape[0]
        pre = top + pltpu.roll(bot, rows - 1, 0) + b1[...]
        return _dot(_gelu_tanh(pre).astype(BF16), w2[...])

    kc = comp(k_ref, ptk, pbk, wtk, wbk, b1k, w2k)
    kc_ref[0] = _rope(kc, cos_ref[0], sin_ref[0]).astype(BF16)
    vct_ref[0] = jnp.transpose(comp(v_ref, ptv, pbv, wtv, wbv, b1v, w2v)).astype(BF16)


def _compress_weights(pos, w1, b1, w2):
    half = CMP_BLOCK // 2
    g = NSA_KV_HEADS
    w1r = w1.reshape(CMP_BLOCK, HEAD_DIM, CMP_HIDDEN)
    eye = jnp.eye(g, dtype=F32)

    def lay(wh):
        return jnp.einsum('ldh,ge->lgdeh', wh, eye).reshape(half * g * HEAD_DIM, g * CMP_HIDDEN)

    def tile_pos(p):
        return jnp.broadcast_to(p[:, None, :], (half, g, HEAD_DIM)).reshape(1, half * g * HEAD_DIM)

    w2bd = jnp.einsum('hd,ge->ghed', w2, eye).reshape(g * CMP_HIDDEN, g * HEAD_DIM)
    return (tile_pos(pos[:half]), tile_pos(pos[half:]), lay(w1r[:half]).astype(BF16), lay(w1r[half:]).astype(BF16),
            jnp.tile(b1, g)[None, :], w2bd.astype(BF16))


def _compress(kc_in, vc_in, wk, wv, cos_e, sin_e):
    b, nc, width = kc_in.shape
    const = lambda a: pl.BlockSpec(a.shape, lambda i: (0,) * a.ndim)
    blk = lambda w: pl.BlockSpec((1, nc, w), lambda i: (i, 0, 0))
    return pl.pallas_call(
        _compress_kernel,
        grid=(b,),
        in_specs=[blk(width), blk(width)] + [const(a) for a in wk] + [const(a) for a in wv]
                 + [blk(LANES), blk(LANES)],
        out_specs=[blk(NSA_KV_WIDTH), pl.BlockSpec((1, NSA_KV_WIDTH, nc), lambda i: (i, 0, 0))],
        out_shape=[jax.ShapeDtypeStruct((b, nc, NSA_KV_WIDTH), BF16),
                   jax.ShapeDtypeStruct((b, NSA_KV_WIDTH, nc), BF16)],
        compiler_params=_cparams(("parallel",)),
        name="nsa_compress",
    )(kc_in, vc_in, *wk, *wv, cos_e, sin_e)


def _kv_prep_kernel(kv_ref, cos_ref, sin_ref, ksa_ref, vst_ref, kw_ref, vwt_ref, *, tt):
    j = pl.program_id(1)
    cos = cos_ref[0]
    sin = sin_ref[0]
    w = NSA_KV_WIDTH
    lane = lax.broadcasted_iota(jnp.int32, (tt, LANES), 1)
    tpos = j * tt + lax.broadcasted_iota(jnp.int32, (tt, LANES), 0)
    onehot = jnp.where(lane - HEAD_DIM == tpos // SLC_BLOCK, 1.0, 0.0)
    lo = lane < HEAD_DIM
    ks = _rope(kv_ref[0, :, 0:w], cos, sin)
    ksa_ref[0, 0] = jnp.where(lo, ks, onehot).astype(BF16)
    ksa_ref[0, 1] = jnp.where(lo, pltpu.roll(ks, HEAD_DIM, 1), onehot).astype(BF16)
    kw_ref[0] = _rope(kv_ref[0, :, 2 * w:3 * w], cos, sin).astype(BF16)
    top = lax.broadcasted_iota(jnp.int32, (LANES, tt), 0) < HEAD_DIM

    def store_t(v, ref, chunk):
        vt = jnp.transpose(v)
        per_head = (jnp.where(top, vt, 1.0), jnp.where(top, pltpu.roll(vt, HEAD_DIM, 0), 1.0))
        for g, vg in enumerate(per_head):
            for u in range(tt // chunk):
                ref[0, g, u] = vg[:, u * chunk:(u + 1) * chunk].astype(BF16)

    store_t(kv_ref[0, :, w:2 * w], vst_ref, SEL_CHUNK)
    store_t(kv_ref[0, :, 3 * w:4 * w], vwt_ref, ATT_TQ)


def _kv_prep(kv4, cos, sin):
    b, t, _ = kv4.shape
    tt = min(t, 512)
    g = NSA_KV_HEADS
    blk = lambda w: pl.BlockSpec((1, tt, w), lambda i, j: (i, j, 0))
    return pl.pallas_call(
        functools.partial(_kv_prep_kernel, tt=tt),
        grid=(b, t // tt),
        in_specs=[blk(4 * NSA_KV_WIDTH), blk(LANES), blk(LANES)],
        out_specs=[pl.BlockSpec((1, g, tt, LANES), lambda i, j: (i, 0, j, 0)),
                   pl.BlockSpec((1, g, tt // SEL_CHUNK, LANES, SEL_CHUNK), lambda i, j: (i, 0, j, 0, 0)),
                   blk(NSA_KV_WIDTH),
                   pl.BlockSpec((1, g, tt // ATT_TQ, LANES, ATT_TQ), lambda i, j: (i, 0, j, 0, 0))],
        out_shape=[jax.ShapeDtypeStruct((b, g, t, LANES), BF16),
                   jax.ShapeDtypeStruct((b, g, t // SEL_CHUNK, LANES, SEL_CHUNK), BF16),
                   jax.ShapeDtypeStruct((b, t, NSA_KV_WIDTH), BF16),
                   jax.ShapeDtypeStruct((b, g, t // ATT_TQ, LANES, ATT_TQ), BF16)],
        compiler_params=_cparams(("parallel", "parallel")),
        name="nsa_kv_prep",
    )(kv4, cos, sin)


def _lane_tiled(x):
    return jnp.concatenate([x] * NSA_REP, axis=1)


def _queries_t(qt, g, other, q_first):
    cols = []
    for r in range(NSA_REP):
        hq = g * NSA_REP + r
        piece = qt[hq * HEAD_DIM:(hq + 1) * HEAD_DIM]
        cols.append(jnp.concatenate([piece, other] if q_first else [other, piece], axis=0))
    return jnp.concatenate(cols, axis=1)


def _gated_heads_from_t(pieces, gate_ref, branch):
    sg = _sigmoid(jnp.transpose(gate_ref[0]))
    rows = [hq * N_BRANCH + branch for hq in range(NSA_HEADS)]
    gated = [p * sg[r:r + 1] for p, r in zip(pieces, rows)]
    slabs = [jnp.transpose(jnp.concatenate(gated[2 * j:2 * j + 2], axis=0)) for j in range(NSA_HEADS // 2)]
    return jnp.concatenate(slabs, axis=1)


def _normalised_heads(acc, tq):
    ot = acc[0:HEAD_DIM] * (1.0 / acc[HEAD_DIM:HEAD_DIM + 1])
    return [ot[:, r * tq:(r + 1) * tq] for r in range(NSA_REP)]


def _cmpattn_kernel(q_ref, gate_ref, cos_ref, sin_ref, kc_ref, vct_ref, ovt_ref, qt_ref, o_ref, nsel_ref, *, tq, n_slc):
    i = pl.program_id(1)
    cos = cos_ref[0]
    sin = sin_ref[0]
    qt = jnp.concatenate([jnp.transpose(_rope(q_ref[0, :, j * LANES:(j + 1) * LANES], cos, sin) * Q_SCALE)
                          for j in range(NSA_WIDTH // LANES)], axis=0).astype(BF16)
    qt_ref[0] = qt
    kc = kc_ref[0]
    vct = vct_ref[0]
    nc = kc.shape[0]
    n = lax.broadcasted_iota(jnp.int32, (nc, tq), 0)
    t = i * tq + lax.broadcasted_iota(jnp.int32, (nc, tq), 1)
    bias = _lane_tiled(jnp.where(n * CMP_STRIDE + (CMP_BLOCK - 1) <= t, 0.0, NEG_INF))
    sees_any = _lane_tiled(jnp.where(i * tq + lax.broadcasted_iota(jnp.int32, (1, tq), 1) >= CMP_BLOCK - 1, 1.0, 0.0))
    zeros = jnp.zeros((HEAD_DIM, tq), BF16)
    blk = lax.broadcasted_iota(jnp.int32, (n_slc, tq), 0)
    tl = i * tq + lax.broadcasted_iota(jnp.int32, (n_slc, tq), 1)
    cur = tl // SLC_BLOCK
    forced = (blk == 0) | (blk == cur) | (blk == cur - 1)
    valid = blk * SLC_BLOCK <= tl
    ovt = ovt_ref[...]
    pieces = []
    scores = []
    for g in range(NSA_KV_HEADS):
        s = _dot(kc, _queries_t(qt, g, zeros, g == 0)) + bias
        e = jnp.exp2(s - jnp.max(s, axis=0, keepdims=True))
        p = e * (sees_any * (1.0 / jnp.sum(e, axis=0, keepdims=True)))
        ot = _dot(vct, p.astype(BF16))
        pieces.extend(ot[g * HEAD_DIM:(g + 1) * HEAD_DIM, r * tq:(r + 1) * tq] for r in range(NSA_REP))
        psum = p[:, 0:tq] + p[:, tq:2 * tq] + p[:, 2 * tq:3 * tq]
        hi = psum.astype(BF16)
        rest = psum - hi.astype(F32)
        mid = rest.astype(BF16)
        lo = (rest - mid.astype(F32)).astype(BF16)
        imp = _dot(ovt, hi) + _dot(ovt, mid) + _dot(ovt, lo)
        scores.append(jnp.where(valid, jnp.where(forced, FORCE_SCORE, imp), -1.0))
    o_ref[0] = _gated_heads_from_t(pieces, gate_ref, 0)

    n_live = ((i + 1) * tq) // SLC_BLOCK
    n_cls = n_slc // RANK_STEP
    cls = jnp.minimum((n_live - 1) // RANK_STEP, n_cls - 1)
    for c in range(n_cls):
        @pl.when(cls == c)
        def _(c=c):
            for g in range(NSA_KV_HEADS):
                nsel_ref[0, g] = _selection_bias(scores[g], (c + 1) * RANK_STEP, min(SLC_TOP, n_slc), tq)


def _selection_bias(score, nb, top, tq):
    sub = lax.broadcasted_iota(jnp.int32, (SUBLANES, tq), 0)
    ties = [jnp.where(sub > k, 1.0, 0.0) for k in range(SUBLANES)]
    groups = [score[SUBLANES * r:SUBLANES * (r + 1)] for r in range(nb // SUBLANES)]
    ranks = [jnp.zeros((SUBLANES, tq), F32) for _ in groups]
    for jp in range(nb):
        rowv = score[jp:jp + 1, :]
        for r, grp in enumerate(groups):
            if SUBLANES * r > jp:
                inc = jnp.where(rowv >= grp, 1.0, 0.0)
            elif SUBLANES * r + SUBLANES - 1 < jp:
                inc = jnp.where(rowv > grp, 1.0, 0.0)
            else:
                inc = jnp.where(rowv > grp, 1.0, jnp.where(rowv == grp, ties[jp - SUBLANES * r], 0.0))
            ranks[r] = ranks[r] + inc
    bias = jnp.where(jnp.concatenate(ranks, axis=0) < float(top), 0.0, NEG_INF)
    if nb < SEL_ROWS:
        bias = jnp.concatenate([bias, jnp.full((SEL_ROWS - nb, tq), NEG_INF, F32)], axis=0)
    return bias.astype(BF16)


def _cmpattn(q, gate, cos, sin, kc, vct, ovt):
    b, t, _ = q.shape
    nc = kc.shape[1]
    tq = ATT_TQ
    n_slc = t // SLC_BLOCK
    assert n_slc <= SEL_ROWS and n_slc % RANK_STEP == 0
    blk = lambda w: pl.BlockSpec((1, tq, w), lambda i, j: (i, j, 0))
    return pl.pallas_call(
        functools.partial(_cmpattn_kernel, tq=tq, n_slc=n_slc),
        grid=(b, t // tq),
        in_specs=[blk(NSA_WIDTH), blk(LANES), blk(LANES), blk(LANES),
                  pl.BlockSpec((1, nc, NSA_KV_WIDTH), lambda i, j: (i, 0, 0)),
                  pl.BlockSpec((1, NSA_KV_WIDTH, nc), lambda i, j: (i, 0, 0)),
                  pl.BlockSpec(ovt.shape, lambda i, j: (0, 0))],
        out_specs=[pl.BlockSpec((1, NSA_WIDTH, tq), lambda i, j: (i, 0, j)),
                   blk(NSA_WIDTH),
                   pl.BlockSpec((1, NSA_KV_HEADS, SEL_ROWS, tq), lambda i, j: (i, 0, 0, j))],
        out_shape=[jax.ShapeDtypeStruct((b, NSA_WIDTH, t), BF16),
                   jax.ShapeDtypeStruct((b, t, NSA_WIDTH), F32),
                   jax.ShapeDtypeStruct((b, NSA_KV_HEADS, SEL_ROWS, t), BF16)],
        compiler_params=_cparams(("parallel", "parallel")),
        name="nsa_compressed_attn_select",
    )(q, gate, cos, sin, kc, vct, ovt)


def _slc_kernel(qt_ref, nsel_ref, gate_ref, ksa_ref, vst_ref, o_ref, *, tq, kc):
    i = pl.program_id(1)
    qt = qt_ref[0]
    c_diag = (i * tq) // kc
    kpos = lax.broadcasted_iota(jnp.int32, (kc, tq), 0)
    tpos = i * tq + lax.broadcasted_iota(jnp.int32, (kc, tq), 1)
    qas = [_queries_t(qt, g, nsel_ref[0, g], True) for g in range(NSA_KV_HEADS)]

    def step(c, carry, causal):
        out = []
        for g, (m, acc) in enumerate(carry):
            k = ksa_ref[0, g, pl.ds(pl.multiple_of(c * kc, kc), kc), :]
            s = _dot(k, qas[g])
            if causal:
                s = s + _lane_tiled(jnp.where(c * kc + kpos <= tpos, 0.0, NEG_INF))
            m_new = jnp.maximum(m, jnp.max(s, axis=0, keepdims=True))
            p = jnp.exp2(s - m_new).astype(BF16)
            out.append((m_new, jnp.exp2(m - m_new) * acc + _dot(vst_ref[0, g, c], p)))
        return tuple(out)

    init = tuple((jnp.full((1, NSA_REP * tq), NEG_INF, F32), jnp.zeros((LANES, NSA_REP * tq), F32))
                 for _ in range(NSA_KV_HEADS))
    carry = lax.fori_loop(0, c_diag, lambda c, carry: step(c, carry, False), init)
    carry = step(c_diag, carry, True)
    pieces = []
    for _, acc in carry:
        pieces.extend(_normalised_heads(acc, tq))
    o_ref[0] = _gated_heads_from_t(pieces, gate_ref, 1)


def _slc(qt, nsel, gate, ksa, vst):
    b, _, t = qt.shape
    tq = SEL_TQ
    kc = SEL_CHUNK
    g = NSA_KV_HEADS
    return pl.pallas_call(
        functools.partial(_slc_kernel, tq=tq, kc=kc),
        grid=(b, t // tq),
        in_specs=[pl.BlockSpec((1, NSA_WIDTH, tq), lambda i, j: (i, 0, j)),
                  pl.BlockSpec((1, g, SEL_ROWS, tq), lambda i, j: (i, 0, 0, j)),
                  pl.BlockSpec((1, tq, LANES), lambda i, j: (i, j, 0)),
                  pl.BlockSpec((1, g, t, LANES), lambda i, j: (i, 0, 0, 0)),
                  pl.BlockSpec((1, g, t // kc, LANES, kc), lambda i, j: (i, 0, 0, 0, 0))],
        out_specs=pl.BlockSpec((1, tq, NSA_WIDTH), lambda i, j: (i, j, 0)),
        out_shape=jax.ShapeDtypeStruct((b, t, NSA_WIDTH), F32),
        compiler_params=_cparams(("parallel", "parallel")),
        name="nsa_selected_attn",
    )(qt, nsel, gate, ksa, vst)


def _win_kernel(qt_ref, gate_ref, kw_ref, vwt_ref, o_ref, *, tq, nblk, t_total):
    i = pl.program_id(1)
    qt = qt_ref[0]
    kb = ATT_TQ
    span = nblk * kb
    first_blk = jnp.clip(i * (tq // kb) - WINDOW // kb, 0, t_total // kb - nblk)
    start = pl.multiple_of(first_blk * kb, kb)
    k = kw_ref[0, pl.ds(start, span), :]
    d = (i * tq + lax.broadcasted_iota(jnp.int32, (span, tq), 1)
         - (start + lax.broadcasted_iota(jnp.int32, (span, tq), 0)))
    bias = _lane_tiled(jnp.where((d >= 0) & (d < WINDOW), 0.0, NEG_INF))
    zeros = jnp.zeros((HEAD_DIM, tq), BF16)
    pieces = []
    for g in range(NSA_KV_HEADS):
        s = _dot(k, _queries_t(qt, g, zeros, g == 0)) + bias
        p = jnp.exp2(s - jnp.max(s, axis=0, keepdims=True)).astype(BF16)
        vt = jnp.concatenate([vwt_ref[0, g, first_blk + u] for u in range(nblk)], axis=1)
        pieces.extend(_normalised_heads(_dot(vt, p), tq))
    o_ref[0] = _gated_heads_from_t(pieces, gate_ref, 2)


def _win(qt, gate, kw, vwt):
    b, _, t = qt.shape
    tq = min(t, WIN_TQ)
    kb = ATT_TQ
    nblk = min(t // kb, (WINDOW + tq) // kb)
    g = NSA_KV_HEADS
    return pl.pallas_call(
        functools.partial(_win_kernel, tq=tq, nblk=nblk, t_total=t),
        grid=(b, t // tq),
        in_specs=[pl.BlockSpec((1, NSA_WIDTH, tq), lambda i, j: (i, 0, j)),
                  pl.BlockSpec((1, tq, LANES), lambda i, j: (i, j, 0)),
                  pl.BlockSpec((1, t, NSA_KV_WIDTH), lambda i, j: (i, 0, 0)),
                  pl.BlockSpec((1, g, t // kb, LANES, kb), lambda i, j: (i, 0, 0, 0, 0))],
        out_specs=pl.BlockSpec((1, tq, NSA_WIDTH), lambda i, j: (i, j, 0)),
        out_shape=jax.ShapeDtypeStruct((b, t, NSA_WIDTH), F32),
        compiler_params=_cparams(("parallel", "parallel")),
        name="nsa_window_attn",
    )(qt, gate, kw, vwt)


def _outproj_kernel(x_ref, ya_ref, yb_ref, oc_ref, os_ref, ow_ref, w_ref, g_ref, b_ref, o_ref, *, alpha):
    yc = oc_ref[...] + os_ref[...] + ow_ref[...]
    y = jnp.concatenate([ya_ref[...], yb_ref[...], yc], axis=1).astype(BF16)
    o_ref[...] = _layernorm(alpha * x_ref[...] + _dot(y, w_ref[...]), g_ref[...], b_ref[...])


def _outproj(x2, ya, yb, oc, osl, ow, w, g, bta, alpha):
    n, d = x2.shape
    tm = 512
    row = lambda w_: pl.BlockSpec((tm, w_), lambda i: (i, 0))
    const = lambda shape: pl.BlockSpec(shape, lambda i: (0, 0))
    return pl.pallas_call(
        functools.partial(_outproj_kernel, alpha=alpha),
        grid=(n // tm,),
        in_specs=[row(d), row(POOL_WIDTH), row(RET_WIDTH), row(NSA_WIDTH), row(NSA_WIDTH), row(NSA_WIDTH),
                  const((d, d)), const((1, d)), const((1, d))],
        out_specs=row(d),
        out_shape=jax.ShapeDtypeStruct((n, d), F32),
        compiler_params=_cparams(("parallel",)),
        name="out_proj_ln",
    )(x2, ya, yb, oc, osl, ow, w, g, bta)


def _ffn_kernel(x_ref, xh_ref, wg_ref, wu_ref, cw_ref, cb_ref, wd_ref, g_ref, b_ref, o_ref, xb_ref, act_ref,
                *, alpha, fc):
    j = pl.program_id(1)
    h = CONV_HALO
    xb_ref[0:h, :] = jnp.where(j > 0, xh_ref[0], 0.0).astype(BF16)
    xb_ref[h:, :] = x_ref[0].astype(BF16)
    for c in range(0, wg_ref.shape[1], fc):
        hg = _dot(xb_ref[...], wg_ref[:, c:c + fc])
        up = _dot(xb_ref[h:, :], wu_ref[:, c:c + fc])
        hc = (cb_ref[:, c:c + fc] + pltpu.roll(hg, 2, 0)[h:] * cw_ref[0:1, c:c + fc]
              + pltpu.roll(hg, 1, 0)[h:] * cw_ref[1:2, c:c + fc] + hg[h:] * cw_ref[2:3, c:c + fc])
        act_ref[:, c:c + fc] = (_gelu_tanh(hc) * up).astype(BF16)
    o_ref[0] = _layernorm(alpha * x_ref[0] + _dot(act_ref[...], wd_ref[...]), g_ref[...], b_ref[...])


def _ffn(x, wg, wu, cw, cb, wd, g, bta, alpha):
    b, t, d = x.shape
    dff = wg.shape[1]
    tm = min(t, 512)
    fc = 2 * LANES
    r = tm // CONV_HALO
    const = lambda shape: pl.BlockSpec(shape, lambda i, j: (0, 0), pipeline_mode=pl.Buffered(1))
    return pl.pallas_call(
        functools.partial(_ffn_kernel, alpha=alpha, fc=fc),
        grid=(b, t // tm),
        in_specs=[pl.BlockSpec((1, tm, d), lambda i, j: (i, j, 0)),
                  pl.BlockSpec((1, CONV_HALO, d), lambda i, j: (i, jnp.maximum(j * r - 1, 0), 0)),
                  const((d, dff)), const((d, dff)), const((CONV_WIDTH, dff)), const((1, dff)), const((dff, d)),
                  const((1, d)), const((1, d))],
        out_specs=pl.BlockSpec((1, tm, d), lambda i, j: (i, j, 0)),
        out_shape=jax.ShapeDtypeStruct((b, t, d), F32),
        scratch_shapes=[pltpu.VMEM((CONV_HALO + tm, d), BF16), pltpu.VMEM((tm, dff), BF16)],
        compiler_params=_cparams(("parallel", "parallel")),
        name="conv_ffn_ln",
    )(x, x, wg, wu, cw, cb, wd, g, bta)


def _overlap_t(nc, n_slc):
    ci = np.arange(nc)[None, :]
    sj = np.arange(n_slc)[:, None]
    ov = np.clip(np.minimum(ci * CMP_STRIDE + CMP_BLOCK, (sj + 1) * SLC_BLOCK)
                 - np.maximum(ci * CMP_STRIDE, sj * SLC_BLOCK), 0, None).astype(np.float32) / CMP_STRIDE
    ov[:, nc - 1] = 0.0
    return jnp.asarray(ov, BF16)


def kernel(x, positions, w_in, w_out, pool_w, pool_scale, ret_gn_g, cmp_pos_k, cmp_w1_k, cmp_b1_k, cmp_w2_k,
           cmp_pos_v, cmp_w1_v, cmp_b1_v, cmp_w2_v, ffn_w_gate, ffn_w_up, ffn_conv_w, ffn_conv_b, ffn_w_down,
           ln1_g, ln1_b, ln2_g, ln2_b):
    b, t, d = x.shape
    depth = w_in.shape[0]
    alpha = float((2 * depth) ** 0.25)
    n = b * t
    nc = t // CMP_STRIDE
    n_slc = t // SLC_BLOCK

    inv = ROPE_THETA ** (-jnp.arange(0, HEAD_DIM, 2, dtype=F32) / HEAD_DIM)
    inv = jnp.tile(inv, LANES // (HEAD_DIM // 2))[None, :]
    cos, sin = _rope_tables(positions, inv)
    ends = jnp.minimum(jnp.arange(nc) * CMP_STRIDE + CMP_BLOCK - 1, t - 1)
    cos_e, sin_e = _rope_tables(positions[:, ends], inv)
    ovt = _overlap_t(nc, n_slc)
    eye_g = jnp.eye(len(POOL_WINDOWS), dtype=F32)

    for l in range(depth):
        w_in_p = jnp.pad(w_in[l], ((0, 0), (0, IN_PAD - IN_WIDTH))).astype(BF16)
        x2 = x.reshape(n, d)
        v_pool, h_ret, q_nsa, k_cmp, v_cmp, kv4, gate = _inproj(x2, w_in_p)

        wbd = jnp.einsum('gcd,ge->gced', pool_w[l], eye_g).reshape(POOL_WIDTH, POOL_WIDTH).astype(BF16)
        y_a = _pool(v_pool.reshape(b, t, POOL_WIDTH), wbd, pool_scale[l][None, :])

        y_b = _retention(h_ret.reshape(b, t, 4 * RET_WIDTH), cos, sin, ret_gn_g[l][None, :])

        wk = _compress_weights(cmp_pos_k[l], cmp_w1_k[l], cmp_b1_k[l], cmp_w2_k[l])
        wv = _compress_weights(cmp_pos_v[l], cmp_w1_v[l], cmp_b1_v[l], cmp_w2_v[l])
        kc, vct = _compress(k_cmp.reshape(b, nc, CMP_STRIDE * NSA_KV_WIDTH),
                            v_cmp.reshape(b, nc, CMP_STRIDE * NSA_KV_WIDTH), wk, wv, cos_e, sin_e)
        ksa, vst, kw, vwt = _kv_prep(kv4.reshape(b, t, 4 * NSA_KV_WIDTH), cos, sin)
        gate3 = gate.reshape(b, t, LANES)
        qt, o_cmp, nsel = _cmpattn(q_nsa.reshape(b, t, NSA_WIDTH), gate3, cos, sin, kc, vct, ovt)
        o_slc = _slc(qt, nsel, gate3, ksa, vst)
        o_win = _win(qt, gate3, kw, vwt)

        x2 = _outproj(x2, y_a.reshape(n, POOL_WIDTH), y_b.reshape(n, RET_WIDTH), o_cmp.reshape(n, NSA_WIDTH),
                      o_slc.reshape(n, NSA_WIDTH), o_win.reshape(n, NSA_WIDTH),
                      w_out[l].astype(BF16), ln1_g[l][None, :], ln1_b[l][None, :], alpha)
        x = _ffn(x2.reshape(b, t, d), ffn_w_gate[l].astype(BF16), ffn_w_up[l].astype(BF16), ffn_conv_w[l],
                 ffn_conv_b[l][None, :], ffn_w_down[l].astype(BF16), ln2_g[l][None, :], ln2_b[l][None, :], alpha)
    return x
```

```python
import functools
import math

import jax
import jax.numpy as jnp
import numpy as np
from jax import lax
from jax.experimental import pallas as pl
from jax.experimental.pallas import tpu as pltpu

F32 = jnp.float32
BF16 = jnp.bfloat16

HEAD_DIM = 64
POOL_WINDOWS = (2, 4, 8, 16)
POOL_GROUP_WIDTH = 64
POOL_WIDTH = POOL_GROUP_WIDTH * len(POOL_WINDOWS)
POOL_HALO = 16
RET_HEADS = 6
RET_WIDTH = RET_HEADS * HEAD_DIM
RET_CHUNK = 128
NSA_HEADS = 6
NSA_WIDTH = NSA_HEADS * HEAD_DIM
NSA_KV_HEADS = 2
NSA_REP = NSA_HEADS // NSA_KV_HEADS
NSA_KV_WIDTH = NSA_KV_HEADS * HEAD_DIM
CMP_BLOCK = 32
CMP_STRIDE = 16
CMP_HIDDEN = 128
SLC_BLOCK = 64
SLC_TOP = 16
WINDOW = 512
N_BRANCH = 3
FORCE_SCORE = 1e6
CONV_WIDTH = 3
CONV_HALO = 16
ROPE_THETA = 10000.0
LN_EPS = 1e-5
GN_EPS = 1e-5
NEG_INF = -1e30
Q_SCALE = HEAD_DIM ** -0.5 * math.log2(math.e)

LANES = 128
SUBLANES = 8
VMEM_LIMIT = 56 * 1024 * 1024

ATT_TQ = 128
SEL_TQ = 512
SEL_CHUNK = 512
SEL_ROWS = LANES - HEAD_DIM
V_ROWS = HEAD_DIM + 16
RANK_STEP = 16
WIN_TQ = 256
RET_STEP_CHUNKS = 4

OFF_POOL = 0
OFF_RET = OFF_POOL + POOL_WIDTH
OFF_QNSA = OFF_RET + 4 * RET_WIDTH
OFF_KCMP = OFF_QNSA + NSA_WIDTH
OFF_VCMP = OFF_KCMP + NSA_KV_WIDTH
OFF_KV4 = OFF_VCMP + NSA_KV_WIDTH
OFF_GATE = OFF_KV4 + 4 * NSA_KV_WIDTH
IN_WIDTH = OFF_GATE + NSA_HEADS * N_BRANCH
IN_PAD = OFF_GATE + LANES


def _cparams(sem, flags=None):
    return pltpu.CompilerParams(dimension_semantics=sem, vmem_limit_bytes=VMEM_LIMIT, flags=flags)


def _rope(x, cos, sin_signed):
    lane = lax.broadcasted_iota(jnp.int32, x.shape, 1)
    first = (lane & (HEAD_DIM - 1)) < HEAD_DIM // 2
    partner = jnp.where(first, pltpu.roll(x, LANES - HEAD_DIM // 2, 1), pltpu.roll(x, HEAD_DIM // 2, 1))
    return x * cos + partner * sin_signed


def _gelu_tanh(x):
    return 0.5 * x * (1.0 + jnp.tanh(math.sqrt(2.0 / math.pi) * (x + 0.044715 * (x * x * x))))


def _sigmoid(x):
    return 1.0 / (1.0 + jnp.exp(-x))


def _layernorm(y, g, b):
    mu = jnp.mean(y, axis=-1, keepdims=True)
    d = y - mu
    var = jnp.mean(d * d, axis=-1, keepdims=True)
    return d * lax.rsqrt(var + LN_EPS) * g + b


def _dot(a, b):
    return jnp.dot(a, b, preferred_element_type=F32)


def _dot_nt(a, b):
    return lax.dot_general(a, b, (((1,), (1,)), ((), ())), preferred_element_type=F32)


def _rope_table_kernel(pos_ref, inv_ref, cos_ref, sin_ref):
    ang = pos_ref[0].astype(F32) * inv_ref[...]
    lane = lax.broadcasted_iota(jnp.int32, ang.shape, 1)
    first = (lane & (HEAD_DIM - 1)) < HEAD_DIM // 2
    s = jnp.sin(ang)
    cos_ref[0] = jnp.cos(ang)
    sin_ref[0] = jnp.where(first, -s, s)


def _rope_tables(pos, inv):
    b, n = pos.shape
    tt = min(n, 512)
    shp = jax.ShapeDtypeStruct((b, n, LANES), F32)
    return pl.pallas_call(
        _rope_table_kernel,
        grid=(b, n // tt),
        in_specs=[pl.BlockSpec((1, tt, 1), lambda i, j: (i, j, 0)),
                  pl.BlockSpec((1, LANES), lambda i, j: (0, 0))],
        out_specs=[pl.BlockSpec((1, tt, LANES), lambda i, j: (i, j, 0))] * 2,
        out_shape=[shp, shp],
        compiler_params=_cparams(("parallel", "parallel")),
        name="rope_tables",
    )(pos.reshape(b, n, 1), inv)


def _pool_mix(cur, prev, t0, w, scale):
    cat = jnp.concatenate([prev, cur], axis=0)
    s2 = cat + pltpu.roll(cat, 1, 0)
    s4 = s2 + pltpu.roll(s2, 2, 0)
    s8 = s4 + pltpu.roll(s4, 4, 0)
    s16 = s8 + pltpu.roll(s8, 8, 0)
    grp = lax.broadcasted_iota(jnp.int32, cur.shape, 1) // POOL_GROUP_WIDTH
    t = t0 + lax.broadcasted_iota(jnp.int32, cur.shape, 0)

    def pick(a, b, c, d):
        return jnp.where(grp == 0, a, jnp.where(grp == 1, b, jnp.where(grp == 2, c, d)))

    h = POOL_HALO
    wsum = pick(s2[h:], s4[h:], s8[h:], s16[h:])
    width = pick(*[float(wd) for wd in POOL_WINDOWS])
    cnt = jnp.minimum((t + 1).astype(F32), width)
    mixed = wsum / cnt - cur
    return _dot(mixed.astype(BF16), w) * scale


def _inproj_kernel(x_ref, xh_ref, cos_ref, sin_ref, w_ref, wpool_ref, pscale_ref,
                   ya_ref, ret_ref, qt_ref, kcmp_ref, vcmp_ref, gate_ref, ksa_ref, vst_ref, kw_ref, vwt_ref, *, tm):
    j = pl.program_id(1)
    cos = cos_ref[0]
    sin = sin_ref[0]
    xb = x_ref[0].astype(BF16)
    lane = lax.broadcasted_iota(jnp.int32, (tm, LANES), 1)
    lo = lane < HEAD_DIM
    tpos = j * tm + lax.broadcasted_iota(jnp.int32, (tm, LANES), 0)
    onehot = jnp.where(lane - HEAD_DIM == tpos // SLC_BLOCK, 1.0, 0.0)
    top = lax.broadcasted_iota(jnp.int32, (LANES, tm), 0) < HEAD_DIM

    def store_t(v, ref, chunk):
        vt = jnp.transpose(v)
        per_head = (jnp.where(top, vt, 1.0), jnp.where(top, pltpu.roll(vt, HEAD_DIM, 0), 1.0))
        for g, vg in enumerate(per_head):
            for u in range(tm // chunk):
                ref[0, g, u] = vg[0:V_ROWS, u * chunk:(u + 1) * chunk].astype(BF16)

    def emit(col, y):
        if col < OFF_RET:
            return
        if col < OFF_QNSA:
            part, k = divmod(col - OFF_RET, RET_WIDTH)
            if part == 0:
                y = _rope(y, cos, sin)
            elif part == 1:
                y = _rope(y, cos, sin) * (HEAD_DIM ** -0.5)
            ret_ref[0, :, col - OFF_RET:col - OFF_RET + LANES] = y
        elif col < OFF_KCMP:
            k = col - OFF_QNSA
            qt_ref[0, k:k + LANES, :] = jnp.transpose(_rope(y, cos, sin) * Q_SCALE).astype(BF16)
        elif col == OFF_KCMP:
            kcmp_ref[0] = y
        elif col == OFF_VCMP:
            vcmp_ref[0] = y
        elif col == OFF_KV4:
            ks = _rope(y, cos, sin)
            ksa_ref[0, 0] = jnp.where(lo, ks, onehot).astype(BF16)
            ksa_ref[0, 1] = jnp.where(lo, pltpu.roll(ks, HEAD_DIM, 1), onehot).astype(BF16)
        elif col == OFF_KV4 + NSA_KV_WIDTH:
            store_t(y, vst_ref, SEL_CHUNK)
        elif col == OFF_KV4 + 2 * NSA_KV_WIDTH:
            kw_ref[0] = _rope(y, cos, sin).astype(BF16)
        elif col == OFF_KV4 + 3 * NSA_KV_WIDTH:
            store_t(y, vwt_ref, ATT_TQ)
        else:
            gate_ref[0] = y

    for c in range(0, IN_PAD, 2 * LANES):
        y = _dot(xb, w_ref[:, c:c + 2 * LANES])
        if c == OFF_POOL:
            prev = jnp.where(j > 0, _dot(xh_ref[0].astype(BF16), w_ref[:, 0:POOL_WIDTH]), 0.0)
            ya_ref[0] = _pool_mix(y, prev, j * tm, wpool_ref[...], pscale_ref[...])
        for half in range(2):
            emit(c + half * LANES, y[:, half * LANES:(half + 1) * LANES])


def _inproj(x, cos, sin, w, wpool, pscale):
    b, t, d = x.shape
    tm = min(t, 512)
    g = NSA_KV_HEADS
    r = tm // POOL_HALO
    row = lambda wd: pl.BlockSpec((1, tm, wd), lambda i, j: (i, j, 0))
    const = lambda a: pl.BlockSpec(a.shape, lambda i, j: (0, 0), pipeline_mode=pl.Buffered(1))
    outs = [
        (row(POOL_WIDTH), (b, t, POOL_WIDTH), F32),
        (row(4 * RET_WIDTH), (b, t, 4 * RET_WIDTH), F32),
        (pl.BlockSpec((1, NSA_WIDTH, tm), lambda i, j: (i, 0, j)), (b, NSA_WIDTH, t), BF16),
        (row(NSA_KV_WIDTH), (b, t, NSA_KV_WIDTH), F32),
        (row(NSA_KV_WIDTH), (b, t, NSA_KV_WIDTH), F32),
        (row(LANES), (b, t, LANES), F32),
        (pl.BlockSpec((1, g, tm, LANES), lambda i, j: (i, 0, j, 0)), (b, g, t, LANES), BF16),
        (pl.BlockSpec((1, g, tm // SEL_CHUNK, V_ROWS, SEL_CHUNK), lambda i, j: (i, 0, j, 0, 0)),
         (b, g, t // SEL_CHUNK, V_ROWS, SEL_CHUNK), BF16),
        (row(NSA_KV_WIDTH), (b, t, NSA_KV_WIDTH), BF16),
        (pl.BlockSpec((1, g, tm // ATT_TQ, V_ROWS, ATT_TQ), lambda i, j: (i, 0, j, 0, 0)),
         (b, g, t // ATT_TQ, V_ROWS, ATT_TQ), BF16),
    ]
    return pl.pallas_call(
        functools.partial(_inproj_kernel, tm=tm),
        grid=(b, t // tm),
        in_specs=[row(d), pl.BlockSpec((1, POOL_HALO, d), lambda i, j: (i, jnp.maximum(j * r - 1, 0), 0)),
                  row(LANES), row(LANES), const(w), const(wpool), const(pscale)],
        out_specs=[o[0] for o in outs],
        out_shape=[jax.ShapeDtypeStruct(o[1], o[2]) for o in outs],
        compiler_params=_cparams(("parallel", "parallel")),
        name="in_proj",
    )(x, x, cos, sin, w, wpool, pscale)


def _ret_consts():
    h, c = RET_HEADS, RET_CHUNK
    lg = np.log1p(-np.power(2.0, -5.0 - np.arange(h, dtype=np.float64)))
    i = np.arange(c, dtype=np.float64)
    diff = i[:, None] - i[None, :]
    dmask = np.where(diff >= 0, np.exp(lg[:, None, None] * np.maximum(diff, 0.0)), 0.0)
    xi = np.repeat(np.exp(lg[:, None] * (i + 1.0)).T, HEAD_DIM, axis=1)
    zeta = np.repeat(np.exp(lg[:, None] * (c - 1.0 - i)).T, HEAD_DIM, axis=1)
    gc = np.repeat(np.exp(lg * c), HEAD_DIM)[None, :]
    return (jnp.asarray(dmask, F32), jnp.asarray(xi, F32), jnp.asarray(zeta, F32), jnp.asarray(gc, F32))


def _ret_kernel(h_ref, dmask_ref, xi_ref, zeta_ref, gc_ref, gn_ref, o_ref, state_ref):
    @pl.when(pl.program_id(1) == 0)
    def _():
        state_ref[...] = jnp.zeros_like(state_ref)

    c = RET_CHUNK
    lane = lax.broadcasted_iota(jnp.int32, (c, LANES), 1)
    row = lax.broadcasted_iota(jnp.int32, (LANES, LANES), 0)
    col = lax.broadcasted_iota(jnp.int32, (LANES, LANES), 1)
    lo = lane < HEAD_DIM
    same_head = (row < HEAD_DIM) == (col < HEAD_DIM)
    states = [state_ref[j] for j in range(RET_WIDTH // LANES)]
    for u in range(h_ref.shape[1] // c):
        rs = slice(u * c, (u + 1) * c)
        for j in range(RET_WIDTH // LANES):
            sl = slice(j * LANES, (j + 1) * LANES)
            q = h_ref[0, rs, j * LANES:(j + 1) * LANES]
            k = h_ref[0, rs, RET_WIDTH + j * LANES:RET_WIDTH + (j + 1) * LANES]
            v = h_ref[0, rs, 2 * RET_WIDTH + j * LANES:2 * RET_WIDTH + (j + 1) * LANES]
            g = h_ref[0, rs, 3 * RET_WIDTH + j * LANES:3 * RET_WIDTH + (j + 1) * LANES]
            kb = k.astype(BF16)
            vb = v.astype(BF16)
            o = _dot(q.astype(BF16), states[j].astype(BF16)) * xi_ref[:, sl]
            for hh in range(2):
                m = lo if hh == 0 else jnp.logical_not(lo)
                qm = jnp.where(m, q, 0.0).astype(BF16)
                s = _dot_nt(qm, kb) * dmask_ref[2 * j + hh]
                o = o + jnp.where(m, _dot(s.astype(BF16), vb), 0.0)
            kz = (k * zeta_ref[:, sl]).astype(BF16)
            kv = lax.dot_general(kz, vb, (((0,), (0,)), ((), ())), preferred_element_type=F32)
            states[j] = gc_ref[:, sl] * states[j] + jnp.where(same_head, kv, 0.0)
            s_lo = jnp.sum(jnp.where(lo, o, 0.0), axis=-1, keepdims=True)
            s_hi = jnp.sum(jnp.where(lo, 0.0, o), axis=-1, keepdims=True)
            d = o - jnp.where(lo, s_lo, s_hi) * (1.0 / HEAD_DIM)
            d2 = d * d
            v_lo = jnp.sum(jnp.where(lo, d2, 0.0), axis=-1, keepdims=True)
            v_hi = jnp.sum(jnp.where(lo, 0.0, d2), axis=-1, keepdims=True)
            var = jnp.where(lo, v_lo, v_hi) * (1.0 / HEAD_DIM)
            on = d * lax.rsqrt(var + GN_EPS) * gn_ref[:, sl]
            o_ref[0, rs, sl] = g * _sigmoid(g) * on
    for j, st in enumerate(states):
        state_ref[j] = st


def _retention(hret, gn_g):
    b, t, _ = hret.shape
    c = RET_CHUNK
    rows = min(t, RET_STEP_CHUNKS * c)
    dmask, xi, zeta, gc = _ret_consts()
    const = lambda shape: pl.BlockSpec(shape, lambda i, j: (0,) * len(shape))
    return pl.pallas_call(
        _ret_kernel,
        grid=(b, t // rows),
        in_specs=[pl.BlockSpec((1, rows, 4 * RET_WIDTH), lambda i, j: (i, j, 0)),
                  const((RET_HEADS, c, c)), const((c, RET_WIDTH)), const((c, RET_WIDTH)),
                  const((1, RET_WIDTH)), const((1, RET_WIDTH))],
        out_specs=pl.BlockSpec((1, rows, RET_WIDTH), lambda i, j: (i, j, 0)),
        out_shape=jax.ShapeDtypeStruct((b, t, RET_WIDTH), F32),
        scratch_shapes=[pltpu.VMEM((RET_WIDTH // LANES, LANES, LANES), F32)],
        compiler_params=_cparams(("parallel", "arbitrary")),
        name="retention",
    )(hret, dmask, xi, zeta, gc, gn_g)


def _compress_kernel(k_ref, v_ref, ptk, pbk, wtk, wbk, b1k, w2k, ptv, pbv, wtv, wbv, b1v, w2v,
                     cos_ref, sin_ref, kc_ref, vct_ref):
    def comp(x_ref, pt, pb, wt, wb, b1, w2):
        x = x_ref[0]
        top = _dot((x + pt[...]).astype(BF16), wt[...])
        bot = _dot((x + pb[...]).astype(BF16), wb[...])
        rows = bot.shape[0]
        pre = top + pltpu.roll(bot, rows - 1, 0) + b1[...]
        return _dot(_gelu_tanh(pre).astype(BF16), w2[...])

    kc = comp(k_ref, ptk, pbk, wtk, wbk, b1k, w2k)
    kc_ref[0] = _rope(kc, cos_ref[0], sin_ref[0]).astype(BF16)
    vct_ref[0] = jnp.transpose(comp(v_ref, ptv, pbv, wtv, wbv, b1v, w2v)).astype(BF16)


def _compress_weights(pos, w1, b1, w2):
    half = CMP_BLOCK // 2
    g = NSA_KV_HEADS
    w1r = w1.reshape(CMP_BLOCK, HEAD_DIM, CMP_HIDDEN)
    eye = jnp.eye(g, dtype=F32)

    def lay(wh):
        return jnp.einsum('ldh,ge->lgdeh', wh, eye).reshape(half * g * HEAD_DIM, g * CMP_HIDDEN)

    def tile_pos(p):
        return jnp.broadcast_to(p[:, None, :], (half, g, HEAD_DIM)).reshape(1, half * g * HEAD_DIM)

    w2bd = jnp.einsum('hd,ge->ghed', w2, eye).reshape(g * CMP_HIDDEN, g * HEAD_DIM)
    return (tile_pos(pos[:half]), tile_pos(pos[half:]), lay(w1r[:half]).astype(BF16), lay(w1r[half:]).astype(BF16),
            jnp.tile(b1, g)[None, :], w2bd.astype(BF16))


def _compress(kc_in, vc_in, wk, wv, cos_e, sin_e):
    b, nc, width = kc_in.shape
    const = lambda a: pl.BlockSpec(a.shape, lambda i: (0,) * a.ndim)
    blk = lambda w: pl.BlockSpec((1, nc, w), lambda i: (i, 0, 0))
    return pl.pallas_call(
        _compress_kernel,
        grid=(b,),
        in_specs=[blk(width), blk(width)] + [const(a) for a in wk] + [const(a) for a in wv]
                 + [blk(LANES), blk(LANES)],
        out_specs=[blk(NSA_KV_WIDTH), pl.BlockSpec((1, NSA_KV_WIDTH, nc), lambda i: (i, 0, 0))],
        out_shape=[jax.ShapeDtypeStruct((b, nc, NSA_KV_WIDTH), BF16),
                   jax.ShapeDtypeStruct((b, NSA_KV_WIDTH, nc), BF16)],
        compiler_params=_cparams(("parallel",)),
        name="nsa_compress",
    )(kc_in, vc_in, *wk, *wv, cos_e, sin_e)


def _lane_tiled(x):
    return jnp.concatenate([x] * NSA_REP, axis=1)


def _queries_t(qt, g, other, q_first):
    cols = []
    for r in range(NSA_REP):
        hq = g * NSA_REP + r
        piece = qt[hq * HEAD_DIM:(hq + 1) * HEAD_DIM]
        cols.append(jnp.concatenate([piece, other] if q_first else [other, piece], axis=0))
    return jnp.concatenate(cols, axis=1)


def _gated_heads_from_t(pieces, gate_ref, branch):
    sg = _sigmoid(jnp.transpose(gate_ref[0]))
    rows = [hq * N_BRANCH + branch for hq in range(NSA_HEADS)]
    gated = [p * sg[r:r + 1] for p, r in zip(pieces, rows)]
    slabs = [jnp.transpose(jnp.concatenate(gated[2 * j:2 * j + 2], axis=0)) for j in range(NSA_HEADS // 2)]
    return jnp.concatenate(slabs, axis=1)


def _normalised_heads(acc, tq):
    ot = acc[0:HEAD_DIM] * (1.0 / acc[HEAD_DIM:HEAD_DIM + 1])
    return [ot[:, r * tq:(r + 1) * tq] for r in range(NSA_REP)]


def _cmpattn_kernel(qt_ref, gate_ref, kc_ref, vct_ref, ovt_ref, o_ref, nsel_ref, *, tq, n_slc):
    i = pl.program_id(1)
    qt = qt_ref[0]
    kc = kc_ref[0]
    vct = vct_ref[0]
    nc = kc.shape[0]
    n = lax.broadcasted_iota(jnp.int32, (nc, tq), 0)
    t = i * tq + lax.broadcasted_iota(jnp.int32, (nc, tq), 1)
    bias = _lane_tiled(jnp.where(n * CMP_STRIDE + (CMP_BLOCK - 1) <= t, 0.0, NEG_INF))
    sees_any = _lane_tiled(jnp.where(i * tq + lax.broadcasted_iota(jnp.int32, (1, tq), 1) >= CMP_BLOCK - 1, 1.0, 0.0))
    zeros = jnp.zeros((HEAD_DIM, tq), BF16)
    blk = lax.broadcasted_iota(jnp.int32, (n_slc, tq), 0)
    tl = i * tq + lax.broadcasted_iota(jnp.int32, (n_slc, tq), 1)
    cur = tl // SLC_BLOCK
    forced = (blk == 0) | (blk == cur) | (blk == cur - 1)
    valid = blk * SLC_BLOCK <= tl
    ovt = ovt_ref[...]
    pieces = []
    scores = []
    for g in range(NSA_KV_HEADS):
        s = _dot(kc, _queries_t(qt, g, zeros, g == 0)) + bias
        e = jnp.exp2(s - jnp.max(s, axis=0, keepdims=True))
        p = e * (sees_any * (1.0 / jnp.sum(e, axis=0, keepdims=True)))
        ot = _dot(vct, p.astype(BF16))
        pieces.extend(ot[g * HEAD_DIM:(g + 1) * HEAD_DIM, r * tq:(r + 1) * tq] for r in range(NSA_REP))
        psum = p[:, 0:tq] + p[:, tq:2 * tq] + p[:, 2 * tq:3 * tq]
        hi = psum.astype(BF16)
        rest = psum - hi.astype(F32)
        mid = rest.astype(BF16)
        lo = (rest - mid.astype(F32)).astype(BF16)
        imp = _dot(ovt, hi) + _dot(ovt, mid) + _dot(ovt, lo)
        scores.append(jnp.where(valid, jnp.where(forced, FORCE_SCORE, imp), -1.0))
    o_ref[0] = _gated_heads_from_t(pieces, gate_ref, 0)

    n_live = ((i + 1) * tq) // SLC_BLOCK
    n_cls = n_slc // RANK_STEP
    cls = jnp.minimum((n_live - 1) // RANK_STEP, n_cls - 1)
    for c in range(n_cls):
        @pl.when(cls == c)
        def _(c=c):
            for g in range(NSA_KV_HEADS):
                nsel_ref[0, g] = _selection_bias(scores[g], (c + 1) * RANK_STEP, min(SLC_TOP, n_slc), tq)


def _selection_bias(score, nb, top, tq):
    sub = lax.broadcasted_iota(jnp.int32, (SUBLANES, tq), 0)
    ties = [jnp.where(sub > k, 1.0, 0.0) for k in range(SUBLANES)]
    groups = [score[SUBLANES * r:SUBLANES * (r + 1)] for r in range(nb // SUBLANES)]
    ranks = [jnp.zeros((SUBLANES, tq), F32) for _ in groups]
    for jp in range(nb):
        rowv = score[jp:jp + 1, :]
        for r, grp in enumerate(groups):
            if SUBLANES * r > jp:
                inc = jnp.where(rowv >= grp, 1.0, 0.0)
            elif SUBLANES * r + SUBLANES - 1 < jp:
                inc = jnp.where(rowv > grp, 1.0, 0.0)
            else:
                inc = jnp.where(rowv > grp, 1.0, jnp.where(rowv == grp, ties[jp - SUBLANES * r], 0.0))
            ranks[r] = ranks[r] + inc
    bias = jnp.where(jnp.concatenate(ranks, axis=0) < float(top), 0.0, NEG_INF)
    if nb < SEL_ROWS:
        bias = jnp.concatenate([bias, jnp.full((SEL_ROWS - nb, tq), NEG_INF, F32)], axis=0)
    return bias.astype(BF16)


def _cmpattn(qt, gate, kc, vct, ovt):
    b, _, t = qt.shape
    nc = kc.shape[1]
    tq = ATT_TQ
    n_slc = t // SLC_BLOCK
    assert n_slc <= SEL_ROWS and n_slc % RANK_STEP == 0
    blk = lambda w: pl.BlockSpec((1, tq, w), lambda i, j: (i, j, 0))
    return pl.pallas_call(
        functools.partial(_cmpattn_kernel, tq=tq, n_slc=n_slc),
        grid=(b, t // tq),
        in_specs=[pl.BlockSpec((1, NSA_WIDTH, tq), lambda i, j: (i, 0, j)), blk(LANES),
                  pl.BlockSpec((1, nc, NSA_KV_WIDTH), lambda i, j: (i, 0, 0)),
                  pl.BlockSpec((1, NSA_KV_WIDTH, nc), lambda i, j: (i, 0, 0)),
                  pl.BlockSpec(ovt.shape, lambda i, j: (0, 0))],
        out_specs=[blk(NSA_WIDTH),
                   pl.BlockSpec((1, NSA_KV_HEADS, SEL_ROWS, tq), lambda i, j: (i, 0, 0, j))],
        out_shape=[jax.ShapeDtypeStruct((b, t, NSA_WIDTH), F32),
                   jax.ShapeDtypeStruct((b, NSA_KV_HEADS, SEL_ROWS, t), BF16)],
        compiler_params=_cparams(("parallel", "parallel")),
        name="nsa_compressed_attn_select",
    )(qt, gate, kc, vct, ovt)


def _slc_kernel(qt_ref, nsel_ref, gate_ref, ksa_ref, vst_ref, o_ref, *, tq, kc):
    i = pl.program_id(1)
    qt = qt_ref[0]
    c_diag = (i * tq) // kc
    kpos = lax.broadcasted_iota(jnp.int32, (kc, tq), 0)
    tpos = i * tq + lax.broadcasted_iota(jnp.int32, (kc, tq), 1)
    qas = [_queries_t(qt, g, nsel_ref[0, g], True) for g in range(NSA_KV_HEADS)]

    def step(c, carry, causal):
        out = []
        for g, (m, acc) in enumerate(carry):
            k = ksa_ref[0, g, pl.ds(pl.multiple_of(c * kc, kc), kc), :]
            s = _dot(k, qas[g])
            if causal:
                s = s + _lane_tiled(jnp.where(c * kc + kpos <= tpos, 0.0, NEG_INF))
            m_new = jnp.maximum(m, jnp.max(s, axis=0, keepdims=True))
            p = jnp.exp2(s - m_new).astype(BF16)
            out.append((m_new, jnp.exp2(m - m_new) * acc + _dot(vst_ref[0, g, c], p)))
        return tuple(out)

    init = tuple((jnp.full((1, NSA_REP * tq), NEG_INF, F32), jnp.zeros((V_ROWS, NSA_REP * tq), F32))
                 for _ in range(NSA_KV_HEADS))
    carry = lax.fori_loop(0, c_diag, lambda c, carry: step(c, carry, False), init)
    carry = step(c_diag, carry, True)
    pieces = []
    for _, acc in carry:
        pieces.extend(_normalised_heads(acc, tq))
    o_ref[0] = _gated_heads_from_t(pieces, gate_ref, 1)


def _slc(qt, nsel, gate, ksa, vst):
    b, _, t = qt.shape
    tq = SEL_TQ
    kc = SEL_CHUNK
    g = NSA_KV_HEADS
    return pl.pallas_call(
        functools.partial(_slc_kernel, tq=tq, kc=kc),
        grid=(b, t // tq),
        in_specs=[pl.BlockSpec((1, NSA_WIDTH, tq), lambda i, j: (i, 0, j)),
                  pl.BlockSpec((1, g, SEL_ROWS, tq), lambda i, j: (i, 0, 0, j)),
                  pl.BlockSpec((1, tq, LANES), lambda i, j: (i, j, 0)),
                  pl.BlockSpec((1, g, t, LANES), lambda i, j: (i, 0, 0, 0)),
                  pl.BlockSpec((1, g, t // kc, V_ROWS, kc), lambda i, j: (i, 0, 0, 0, 0))],
        out_specs=pl.BlockSpec((1, tq, NSA_WIDTH), lambda i, j: (i, j, 0)),
        out_shape=jax.ShapeDtypeStruct((b, t, NSA_WIDTH), F32),
        compiler_params=_cparams(("parallel", "parallel")),
        name="nsa_selected_attn",
    )(qt, nsel, gate, ksa, vst)


def _win_kernel(qt_ref, gate_ref, kw_ref, vwt_ref, o_ref, *, tq, nblk, t_total):
    i = pl.program_id(1)
    qt = qt_ref[0]
    kb = ATT_TQ
    span = nblk * kb
    first_blk = jnp.clip(i * (tq // kb) - WINDOW // kb, 0, t_total // kb - nblk)
    start = pl.multiple_of(first_blk * kb, kb)
    k = kw_ref[0, pl.ds(start, span), :]
    d = (i * tq + lax.broadcasted_iota(jnp.int32, (span, tq), 1)
         - (start + lax.broadcasted_iota(jnp.int32, (span, tq), 0)))
    bias = _lane_tiled(jnp.where((d >= 0) & (d < WINDOW), 0.0, NEG_INF))
    zeros = jnp.zeros((HEAD_DIM, tq), BF16)
    pieces = []
    for g in range(NSA_KV_HEADS):
        s = _dot(k, _queries_t(qt, g, zeros, g == 0)) + bias
        p = jnp.exp2(s - jnp.max(s, axis=0, keepdims=True)).astype(BF16)
        vt = jnp.concatenate([vwt_ref[0, g, first_blk + u] for u in range(nblk)], axis=1)
        pieces.extend(_normalised_heads(_dot(vt, p), tq))
    o_ref[0] = _gated_heads_from_t(pieces, gate_ref, 2)


def _win(qt, gate, kw, vwt):
    b, _, t = qt.shape
    tq = min(t, WIN_TQ)
    kb = ATT_TQ
    nblk = min(t // kb, (WINDOW + tq) // kb)
    g = NSA_KV_HEADS
    return pl.pallas_call(
        functools.partial(_win_kernel, tq=tq, nblk=nblk, t_total=t),
        grid=(b, t // tq),
        in_specs=[pl.BlockSpec((1, NSA_WIDTH, tq), lambda i, j: (i, 0, j)),
                  pl.BlockSpec((1, tq, LANES), lambda i, j: (i, j, 0)),
                  pl.BlockSpec((1, t, NSA_KV_WIDTH), lambda i, j: (i, 0, 0)),
                  pl.BlockSpec((1, g, t // kb, V_ROWS, kb), lambda i, j: (i, 0, 0, 0, 0))],
        out_specs=pl.BlockSpec((1, tq, NSA_WIDTH), lambda i, j: (i, j, 0)),
        out_shape=jax.ShapeDtypeStruct((b, t, NSA_WIDTH), F32),
        compiler_params=_cparams(("parallel", "parallel")),
        name="nsa_window_attn",
    )(qt, gate, kw, vwt)


def _outproj_kernel(x_ref, ya_ref, yb_ref, oc_ref, os_ref, ow_ref, w_ref, g_ref, b_ref, o_ref, *, alpha):
    yc = oc_ref[...] + os_ref[...] + ow_ref[...]
    y = jnp.concatenate([ya_ref[...], yb_ref[...], yc], axis=1).astype(BF16)
    o_ref[...] = _layernorm(alpha * x_ref[...] + _dot(y, w_ref[...]), g_ref[...], b_ref[...])


def _outproj(x2, ya, yb, oc, osl, ow, w, g, bta, alpha):
    n, d = x2.shape
    tm = 512
    row = lambda w_: pl.BlockSpec((tm, w_), lambda i: (i, 0))
    const = lambda shape: pl.BlockSpec(shape, lambda i: (0, 0))
    return pl.pallas_call(
        functools.partial(_outproj_kernel, alpha=alpha),
        grid=(n // tm,),
        in_specs=[row(d), row(POOL_WIDTH), row(RET_WIDTH), row(NSA_WIDTH), row(NSA_WIDTH), row(NSA_WIDTH),
                  const((d, d)), const((1, d)), const((1, d))],
        out_specs=row(d),
        out_shape=jax.ShapeDtypeStruct((n, d), F32),
        compiler_params=_cparams(("parallel",)),
        name="out_proj_ln",
    )(x2, ya, yb, oc, osl, ow, w, g, bta)


def _ffn_kernel(x_ref, xh_ref, wg_ref, wu_ref, cw_ref, cb_ref, wd_ref, g_ref, b_ref, o_ref, xb_ref, act_ref,
                *, alpha, fc):
    j = pl.program_id(1)
    h = CONV_HALO
    xb_ref[0:h, :] = jnp.where(j > 0, xh_ref[0], 0.0).astype(BF16)
    xb_ref[h:, :] = x_ref[0].astype(BF16)
    for c in range(0, wg_ref.shape[1], fc):
        hg = _dot(xb_ref[...], wg_ref[:, c:c + fc])
        up = _dot(xb_ref[h:, :], wu_ref[:, c:c + fc])
        hc = (cb_ref[:, c:c + fc] + pltpu.roll(hg, 2, 0)[h:] * cw_ref[0:1, c:c + fc]
              + pltpu.roll(hg, 1, 0)[h:] * cw_ref[1:2, c:c + fc] + hg[h:] * cw_ref[2:3, c:c + fc])
        act_ref[:, c:c + fc] = (_gelu_tanh(hc) * up).astype(BF16)
    o_ref[0] = _layernorm(alpha * x_ref[0] + _dot(act_ref[...], wd_ref[...]), g_ref[...], b_ref[...])


def _ffn(x, wg, wu, cw, cb, wd, g, bta, alpha):
    b, t, d = x.shape
    dff = wg.shape[1]
    tm = min(t, 512)
    fc = 2 * LANES
    r = tm // CONV_HALO
    const = lambda shape: pl.BlockSpec(shape, lambda i, j: (0, 0), pipeline_mode=pl.Buffered(1))
    return pl.pallas_call(
        functools.partial(_ffn_kernel, alpha=alpha, fc=fc),
        grid=(b, t // tm),
        in_specs=[pl.BlockSpec((1, tm, d), lambda i, j: (i, j, 0)),
                  pl.BlockSpec((1, CONV_HALO, d), lambda i, j: (i, jnp.maximum(j * r - 1, 0), 0)),
                  const((d, dff)), const((d, dff)), const((CONV_WIDTH, dff)), const((1, dff)), const((dff, d)),
                  const((1, d)), const((1, d))],
        out_specs=pl.BlockSpec((1, tm, d), lambda i, j: (i, j, 0)),
        out_shape=jax.ShapeDtypeStruct((b, t, d), F32),
        scratch_shapes=[pltpu.VMEM((CONV_HALO + tm, d), BF16), pltpu.VMEM((tm, dff), BF16)],
        compiler_params=_cparams(("parallel", "parallel")),
        name="conv_ffn_ln",
    )(x, x, wg, wu, cw, cb, wd, g, bta)


def _overlap_t(nc, n_slc):
    ci = np.arange(nc)[None, :]
    sj = np.arange(n_slc)[:, None]
    ov = np.clip(np.minimum(ci * CMP_STRIDE + CMP_BLOCK, (sj + 1) * SLC_BLOCK)
                 - np.maximum(ci * CMP_STRIDE, sj * SLC_BLOCK), 0, None).astype(np.float32) / CMP_STRIDE
    ov[:, nc - 1] = 0.0
    return jnp.asarray(ov, BF16)


def kernel(x, positions, w_in, w_out, pool_w, pool_scale, ret_gn_g, cmp_pos_k, cmp_w1_k, cmp_b1_k, cmp_w2_k,
           cmp_pos_v, cmp_w1_v, cmp_b1_v, cmp_w2_v, ffn_w_gate, ffn_w_up, ffn_conv_w, ffn_conv_b, ffn_w_down,
           ln1_g, ln1_b, ln2_g, ln2_b):
    b, t, d = x.shape
    depth = w_in.shape[0]
    alpha = float((2 * depth) ** 0.25)
    n = b * t
    nc = t // CMP_STRIDE
    n_slc = t // SLC_BLOCK

    inv = ROPE_THETA ** (-jnp.arange(0, HEAD_DIM, 2, dtype=F32) / HEAD_DIM)
    inv = jnp.tile(inv, LANES // (HEAD_DIM // 2))[None, :]
    cos, sin = _rope_tables(positions, inv)
    ends = jnp.minimum(jnp.arange(nc) * CMP_STRIDE + CMP_BLOCK - 1, t - 1)
    cos_e, sin_e = _rope_tables(positions[:, ends], inv)
    ovt = _overlap_t(nc, n_slc)
    eye_g = jnp.eye(len(POOL_WINDOWS), dtype=F32)

    for l in range(depth):
        w_in_p = jnp.pad(w_in[l], ((0, 0), (0, IN_PAD - IN_WIDTH))).astype(BF16)
        x2 = x.reshape(n, d)
        wbd = jnp.einsum('gcd,ge->gced', pool_w[l], eye_g).reshape(POOL_WIDTH, POOL_WIDTH).astype(BF16)
        y_a, h_ret, qt, k_cmp, v_cmp, gate3, ksa, vst, kw, vwt = _inproj(
            x, cos, sin, w_in_p, wbd, pool_scale[l][None, :])

        y_b = _retention(h_ret, ret_gn_g[l][None, :])

        wk = _compress_weights(cmp_pos_k[l], cmp_w1_k[l], cmp_b1_k[l], cmp_w2_k[l])
        wv = _compress_weights(cmp_pos_v[l], cmp_w1_v[l], cmp_b1_v[l], cmp_w2_v[l])
        kc, vct = _compress(k_cmp.reshape(b, nc, CMP_STRIDE * NSA_KV_WIDTH),
                            v_cmp.reshape(b, nc, CMP_STRIDE * NSA_KV_WIDTH), wk, wv, cos_e, sin_e)
        o_cmp, nsel = _cmpattn(qt, gate3, kc, vct, ovt)
        o_slc = _slc(qt, nsel, gate3, ksa, vst)
        o_win = _win(qt, gate3, kw, vwt)

        x2 = _outproj(x2, y_a.reshape(n, POOL_WIDTH), y_b.reshape(n, RET_WIDTH), o_cmp.reshape(n, NSA_WIDTH),
                      o_slc.reshape(n, NSA_WIDTH), o_win.reshape(n, NSA_WIDTH),
                      w_out[l].astype(BF16), ln1_g[l][None, :], ln1_b[l][None, :], alpha)
        x = _ffn(x2.reshape(b, t, d), ffn_w_gate[l].astype(BF16), ffn_w_up[l].astype(BF16), ffn_conv_w[l],
                 ffn_conv_b[l][None, :], ffn_w_down[l].astype(BF16), ln2_g[l][None, :], ln2_b[l][None, :], alpha)
    return x
```

```python
import functools
import math

import jax
import jax.numpy as jnp
import numpy as np
from jax import lax
from jax.experimental import pallas as pl
from jax.experimental.pallas import tpu as pltpu

F32 = jnp.float32
BF16 = jnp.bfloat16

HEAD_DIM = 64
POOL_WINDOWS = (2, 4, 8, 16)
POOL_GROUP_WIDTH = 64
POOL_WIDTH = POOL_GROUP_WIDTH * len(POOL_WINDOWS)
POOL_HALO = 16
RET_HEADS = 6
RET_WIDTH = RET_HEADS * HEAD_DIM
RET_CHUNK = 128
NSA_HEADS = 6
NSA_WIDTH = NSA_HEADS * HEAD_DIM
NSA_KV_HEADS = 2
NSA_REP = NSA_HEADS // NSA_KV_HEADS
NSA_KV_WIDTH = NSA_KV_HEADS * HEAD_DIM
CMP_BLOCK = 32
CMP_STRIDE = 16
CMP_HIDDEN = 128
SLC_BLOCK = 64
SLC_TOP = 16
WINDOW = 512
N_BRANCH = 3
FORCE_SCORE = 1e6
CONV_WIDTH = 3
CONV_HALO = 16
ROPE_THETA = 10000.0
LN_EPS = 1e-5
GN_EPS = 1e-5
NEG_INF = -1e30
Q_SCALE = HEAD_DIM ** -0.5 * math.log2(math.e)

LANES = 128
SUBLANES = 8
VMEM_LIMIT = 56 * 1024 * 1024

ATT_TQ = 128
SEL_TQ = 512
SEL_CHUNK = 512
SEL_STEP = 512
SEL_ROWS = LANES - HEAD_DIM
V_ROWS = HEAD_DIM + 16
RANK_STEP = 16
WIN_TQ = 256
RET_STEP_CHUNKS = 8
CMP_TQ = 256

OFF_POOL = 0
OFF_RET = OFF_POOL + POOL_WIDTH
OFF_QNSA = OFF_RET + 4 * RET_WIDTH
OFF_KCMP = OFF_QNSA + NSA_WIDTH
OFF_VCMP = OFF_KCMP + NSA_KV_WIDTH
OFF_KV4 = OFF_VCMP + NSA_KV_WIDTH
OFF_GATE = OFF_KV4 + 4 * NSA_KV_WIDTH
IN_WIDTH = OFF_GATE + NSA_HEADS * N_BRANCH
IN_PAD = OFF_GATE + LANES


def _cparams(sem, flags=None):
    return pltpu.CompilerParams(dimension_semantics=sem, vmem_limit_bytes=VMEM_LIMIT, flags=flags)


def _rope(x, cos, sin_signed):
    lane = lax.broadcasted_iota(jnp.int32, x.shape, 1)
    first = (lane & (HEAD_DIM - 1)) < HEAD_DIM // 2
    partner = jnp.where(first, pltpu.roll(x, LANES - HEAD_DIM // 2, 1), pltpu.roll(x, HEAD_DIM // 2, 1))
    return x * cos + partner * sin_signed


def _gelu_tanh(x):
    return 0.5 * x * (1.0 + jnp.tanh(math.sqrt(2.0 / math.pi) * (x + 0.044715 * (x * x * x))))


def _sigmoid(x):
    return 1.0 / (1.0 + jnp.exp(-x))


def _layernorm(y, g, b):
    mu = jnp.mean(y, axis=-1, keepdims=True)
    d = y - mu
    var = jnp.mean(d * d, axis=-1, keepdims=True)
    return d * lax.rsqrt(var + LN_EPS) * g + b


def _dot(a, b):
    return jnp.dot(a, b, preferred_element_type=F32)


def _dot_nt(a, b):
    return lax.dot_general(a, b, (((1,), (1,)), ((), ())), preferred_element_type=F32)


def _rope_table_kernel(pos_ref, inv_ref, cos_ref, sin_ref):
    ang = pos_ref[0].astype(F32) * inv_ref[...]
    lane = lax.broadcasted_iota(jnp.int32, ang.shape, 1)
    first = (lane & (HEAD_DIM - 1)) < HEAD_DIM // 2
    s = jnp.sin(ang)
    cos_ref[0] = jnp.cos(ang)
    sin_ref[0] = jnp.where(first, -s, s)


def _rope_tables(pos, inv):
    b, n = pos.shape
    tt = min(n, 512)
    shp = jax.ShapeDtypeStruct((b, n, LANES), F32)
    return pl.pallas_call(
        _rope_table_kernel,
        grid=(b, n // tt),
        in_specs=[pl.BlockSpec((1, tt, 1), lambda i, j: (i, j, 0)),
                  pl.BlockSpec((1, LANES), lambda i, j: (0, 0))],
        out_specs=[pl.BlockSpec((1, tt, LANES), lambda i, j: (i, j, 0))] * 2,
        out_shape=[shp, shp],
        compiler_params=_cparams(("parallel", "parallel")),
        name="rope_tables",
    )(pos.reshape(b, n, 1), inv)


def _pool_mix(cur, prev, t0, w, scale):
    cat = jnp.concatenate([prev, cur], axis=0)
    s2 = cat + pltpu.roll(cat, 1, 0)
    s4 = s2 + pltpu.roll(s2, 2, 0)
    s8 = s4 + pltpu.roll(s4, 4, 0)
    s16 = s8 + pltpu.roll(s8, 8, 0)
    grp = lax.broadcasted_iota(jnp.int32, cur.shape, 1) // POOL_GROUP_WIDTH
    t = t0 + lax.broadcasted_iota(jnp.int32, cur.shape, 0)

    def pick(a, b, c, d):
        return jnp.where(grp == 0, a, jnp.where(grp == 1, b, jnp.where(grp == 2, c, d)))

    h = POOL_HALO
    wsum = pick(s2[h:], s4[h:], s8[h:], s16[h:])
    width = pick(*[float(wd) for wd in POOL_WINDOWS])
    cnt = jnp.minimum((t + 1).astype(F32), width)
    mixed = wsum / cnt - cur
    return _dot(mixed.astype(BF16), w) * scale


def _inproj_kernel(x_ref, xh_ref, cos_ref, sin_ref, w_ref, wpool_ref, pscale_ref,
                   ya_ref, ret_ref, qt_ref, kcmp_ref, vcmp_ref, gate_ref, ksa_ref, vst_ref, kw_ref, vwt_ref, *, tm):
    j = pl.program_id(1)
    cos = cos_ref[0]
    sin = sin_ref[0]
    xb = x_ref[0].astype(BF16)
    lane = lax.broadcasted_iota(jnp.int32, (tm, LANES), 1)
    lo = lane < HEAD_DIM
    tpos = j * tm + lax.broadcasted_iota(jnp.int32, (tm, LANES), 0)
    onehot = jnp.where(lane - HEAD_DIM == tpos // SLC_BLOCK, 1.0, 0.0)
    top = lax.broadcasted_iota(jnp.int32, (LANES, tm), 0) < HEAD_DIM

    def store_t(v, ref, chunk):
        vt = jnp.transpose(v)
        per_head = (jnp.where(top, vt, 1.0), jnp.where(top, pltpu.roll(vt, HEAD_DIM, 0), 1.0))
        for g, vg in enumerate(per_head):
            for u in range(tm // chunk):
                ref[0, g, u] = vg[0:V_ROWS, u * chunk:(u + 1) * chunk].astype(BF16)

    def emit(col, y):
        if col < OFF_RET:
            return
        if col < OFF_QNSA:
            part, k = divmod(col - OFF_RET, RET_WIDTH)
            if part == 0:
                y = _rope(y, cos, sin)
            elif part == 1:
                y = _rope(y, cos, sin) * (HEAD_DIM ** -0.5)
            ret_ref[0, :, col - OFF_RET:col - OFF_RET + LANES] = y
        elif col < OFF_KCMP:
            k = col - OFF_QNSA
            qt_ref[0, k:k + LANES, :] = jnp.transpose(_rope(y, cos, sin) * Q_SCALE).astype(BF16)
        elif col == OFF_KCMP:
            kcmp_ref[0] = y
        elif col == OFF_VCMP:
            vcmp_ref[0] = y
        elif col == OFF_KV4:
            ks = _rope(y, cos, sin)
            ksa_ref[0, 0] = jnp.where(lo, ks, onehot).astype(BF16)
            ksa_ref[0, 1] = jnp.where(lo, pltpu.roll(ks, HEAD_DIM, 1), onehot).astype(BF16)
        elif col == OFF_KV4 + NSA_KV_WIDTH:
            store_t(y, vst_ref, SEL_CHUNK)
        elif col == OFF_KV4 + 2 * NSA_KV_WIDTH:
            kw_ref[0] = _rope(y, cos, sin).astype(BF16)
        elif col == OFF_KV4 + 3 * NSA_KV_WIDTH:
            store_t(y, vwt_ref, ATT_TQ)
        else:
            gate_ref[0] = y

    for c in range(0, IN_PAD, 2 * LANES):
        y = _dot(xb, w_ref[:, c:c + 2 * LANES])
        if c == OFF_POOL:
            prev = jnp.where(j > 0, _dot(xh_ref[0].astype(BF16), w_ref[:, 0:POOL_WIDTH]), 0.0)
            ya_ref[0] = _pool_mix(y, prev, j * tm, wpool_ref[...], pscale_ref[...])
        for half in range(2):
            emit(c + half * LANES, y[:, half * LANES:(half + 1) * LANES])


def _inproj(x, cos, sin, w, wpool, pscale):
    b, t, d = x.shape
    tm = min(t, 512)
    g = NSA_KV_HEADS
    r = tm // POOL_HALO
    row = lambda wd: pl.BlockSpec((1, tm, wd), lambda i, j: (i, j, 0))
    const = lambda a: pl.BlockSpec(a.shape, lambda i, j: (0, 0), pipeline_mode=pl.Buffered(1))
    outs = [
        (row(POOL_WIDTH), (b, t, POOL_WIDTH), F32),
        (row(4 * RET_WIDTH), (b, t, 4 * RET_WIDTH), F32),
        (pl.BlockSpec((1, NSA_WIDTH, tm), lambda i, j: (i, 0, j)), (b, NSA_WIDTH, t), BF16),
        (row(NSA_KV_WIDTH), (b, t, NSA_KV_WIDTH), F32),
        (row(NSA_KV_WIDTH), (b, t, NSA_KV_WIDTH), F32),
        (row(LANES), (b, t, LANES), F32),
        (pl.BlockSpec((1, g, tm, LANES), lambda i, j: (i, 0, j, 0)), (b, g, t, LANES), BF16),
        (pl.BlockSpec((1, g, tm // SEL_CHUNK, V_ROWS, SEL_CHUNK), lambda i, j: (i, 0, j, 0, 0)),
         (b, g, t // SEL_CHUNK, V_ROWS, SEL_CHUNK), BF16),
        (row(NSA_KV_WIDTH), (b, t, NSA_KV_WIDTH), BF16),
        (pl.BlockSpec((1, g, tm // ATT_TQ, V_ROWS, ATT_TQ), lambda i, j: (i, 0, j, 0, 0)),
         (b, g, t // ATT_TQ, V_ROWS, ATT_TQ), BF16),
    ]
    return pl.pallas_call(
        functools.partial(_inproj_kernel, tm=tm),
        grid=(b, t // tm),
        in_specs=[row(d), pl.BlockSpec((1, POOL_HALO, d), lambda i, j: (i, jnp.maximum(j * r - 1, 0), 0)),
                  row(LANES), row(LANES), const(w), const(wpool), const(pscale)],
        out_specs=[o[0] for o in outs],
        out_shape=[jax.ShapeDtypeStruct(o[1], o[2]) for o in outs],
        compiler_params=_cparams(("parallel", "parallel")),
        name="in_proj",
    )(x, x, cos, sin, w, wpool, pscale)


def _ret_consts():
    h, c = RET_HEADS, RET_CHUNK
    lg = np.log1p(-np.power(2.0, -5.0 - np.arange(h, dtype=np.float64)))
    i = np.arange(c, dtype=np.float64)
    diff = i[:, None] - i[None, :]
    dmask = np.where(diff >= 0, np.exp(lg[:, None, None] * np.maximum(diff, 0.0)), 0.0)
    xi = np.repeat(np.exp(lg[:, None] * (i + 1.0)).T, HEAD_DIM, axis=1)
    zeta = np.repeat(np.exp(lg[:, None] * (c - 1.0 - i)).T, HEAD_DIM, axis=1)
    gc = np.repeat(np.exp(lg * c), HEAD_DIM)[None, :]
    return (jnp.asarray(dmask, F32), jnp.asarray(xi, F32), jnp.asarray(zeta, F32), jnp.asarray(gc, F32))


def _ret_kernel(h_ref, dmask_ref, xi_ref, zeta_ref, gc_ref, gn_ref, o_ref, state_ref):
    @pl.when(pl.program_id(1) == 0)
    def _():
        state_ref[...] = jnp.zeros_like(state_ref)

    c = RET_CHUNK
    lane = lax.broadcasted_iota(jnp.int32, (c, LANES), 1)
    row = lax.broadcasted_iota(jnp.int32, (LANES, LANES), 0)
    col = lax.broadcasted_iota(jnp.int32, (LANES, LANES), 1)
    lo = lane < HEAD_DIM
    same_head = (row < HEAD_DIM) == (col < HEAD_DIM)
    states = [state_ref[j] for j in range(RET_WIDTH // LANES)]
    for u in range(h_ref.shape[1] // c):
        rs = slice(u * c, (u + 1) * c)
        for j in range(RET_WIDTH // LANES):
            sl = slice(j * LANES, (j + 1) * LANES)
            q = h_ref[0, rs, j * LANES:(j + 1) * LANES]
            k = h_ref[0, rs, RET_WIDTH + j * LANES:RET_WIDTH + (j + 1) * LANES]
            v = h_ref[0, rs, 2 * RET_WIDTH + j * LANES:2 * RET_WIDTH + (j + 1) * LANES]
            g = h_ref[0, rs, 3 * RET_WIDTH + j * LANES:3 * RET_WIDTH + (j + 1) * LANES]
            kb = k.astype(BF16)
            vb = v.astype(BF16)
            o = _dot(q.astype(BF16), states[j].astype(BF16)) * xi_ref[:, sl]
            for hh in range(2):
                m = lo if hh == 0 else jnp.logical_not(lo)
                qm = jnp.where(m, q, 0.0).astype(BF16)
                s = _dot_nt(qm, kb) * dmask_ref[2 * j + hh]
                o = o + jnp.where(m, _dot(s.astype(BF16), vb), 0.0)
            kz = (k * zeta_ref[:, sl]).astype(BF16)
            kv = lax.dot_general(kz, vb, (((0,), (0,)), ((), ())), preferred_element_type=F32)
            states[j] = gc_ref[:, sl] * states[j] + jnp.where(same_head, kv, 0.0)
            s_lo = jnp.sum(jnp.where(lo, o, 0.0), axis=-1, keepdims=True)
            s_hi = jnp.sum(jnp.where(lo, 0.0, o), axis=-1, keepdims=True)
            d = o - jnp.where(lo, s_lo, s_hi) * (1.0 / HEAD_DIM)
            d2 = d * d
            v_lo = jnp.sum(jnp.where(lo, d2, 0.0), axis=-1, keepdims=True)
            v_hi = jnp.sum(jnp.where(lo, 0.0, d2), axis=-1, keepdims=True)
            var = jnp.where(lo, v_lo, v_hi) * (1.0 / HEAD_DIM)
            on = d * lax.rsqrt(var + GN_EPS) * gn_ref[:, sl]
            o_ref[0, rs, sl] = g * _sigmoid(g) * on
    for j, st in enumerate(states):
        state_ref[j] = st


def _retention(hret, gn_g):
    b, t, _ = hret.shape
    c = RET_CHUNK
    rows = min(t, RET_STEP_CHUNKS * c)
    dmask, xi, zeta, gc = _ret_consts()
    const = lambda shape: pl.BlockSpec(shape, lambda i, j: (0,) * len(shape))
    return pl.pallas_call(
        _ret_kernel,
        grid=(b, t // rows),
        in_specs=[pl.BlockSpec((1, rows, 4 * RET_WIDTH), lambda i, j: (i, j, 0)),
                  const((RET_HEADS, c, c)), const((c, RET_WIDTH)), const((c, RET_WIDTH)),
                  const((1, RET_WIDTH)), const((1, RET_WIDTH))],
        out_specs=pl.BlockSpec((1, rows, RET_WIDTH), lambda i, j: (i, j, 0)),
        out_shape=jax.ShapeDtypeStruct((b, t, RET_WIDTH), F32),
        scratch_shapes=[pltpu.VMEM((RET_WIDTH // LANES, LANES, LANES), F32)],
        compiler_params=_cparams(("parallel", "arbitrary")),
        name="retention",
    )(hret, dmask, xi, zeta, gc, gn_g)


def _compress_kernel(k_ref, v_ref, ptk, pbk, wtk, wbk, b1k, w2k, ptv, pbv, wtv, wbv, b1v, w2v,
                     cos_ref, sin_ref, kc_ref, vct_ref):
    def comp(x_ref, pt, pb, wt, wb, b1, w2):
        x = x_ref[0]
        top = _dot((x + pt[...]).astype(BF16), wt[...])
        bot = _dot((x + pb[...]).astype(BF16), wb[...])
        rows = bot.shape[0]
        pre = top + pltpu.roll(bot, rows - 1, 0) + b1[...]
        return _dot(_gelu_tanh(pre).astype(BF16), w2[...])

    kc = comp(k_ref, ptk, pbk, wtk, wbk, b1k, w2k)
    kc_ref[0] = _rope(kc, cos_ref[0], sin_ref[0]).astype(BF16)
    vct_ref[0] = jnp.transpose(comp(v_ref, ptv, pbv, wtv, wbv, b1v, w2v)).astype(BF16)


def _compress_weights(pos, w1, b1, w2):
    half = CMP_BLOCK // 2
    g = NSA_KV_HEADS
    w1r = w1.reshape(CMP_BLOCK, HEAD_DIM, CMP_HIDDEN)
    eye = jnp.eye(g, dtype=F32)

    def lay(wh):
        return jnp.einsum('ldh,ge->lgdeh', wh, eye).reshape(half * g * HEAD_DIM, g * CMP_HIDDEN)

    def tile_pos(p):
        return jnp.broadcast_to(p[:, None, :], (half, g, HEAD_DIM)).reshape(1, half * g * HEAD_DIM)

    w2bd = jnp.einsum('hd,ge->ghed', w2, eye).reshape(g * CMP_HIDDEN, g * HEAD_DIM)
    return (tile_pos(pos[:half]), tile_pos(pos[half:]), lay(w1r[:half]).astype(BF16), lay(w1r[half:]).astype(BF16),
            jnp.tile(b1, g)[None, :], w2bd.astype(BF16))


def _compress(kc_in, vc_in, wk, wv, cos_e, sin_e):
    b, nc, width = kc_in.shape
    const = lambda a: pl.BlockSpec(a.shape, lambda i: (0,) * a.ndim)
    blk = lambda w: pl.BlockSpec((1, nc, w), lambda i: (i, 0, 0))
    return pl.pallas_call(
        _compress_kernel,
        grid=(b,),
        in_specs=[blk(width), blk(width)] + [const(a) for a in wk] + [const(a) for a in wv]
                 + [blk(LANES), blk(LANES)],
        out_specs=[blk(NSA_KV_WIDTH), pl.BlockSpec((1, NSA_KV_WIDTH, nc), lambda i: (i, 0, 0))],
        out_shape=[jax.ShapeDtypeStruct((b, nc, NSA_KV_WIDTH), BF16),
                   jax.ShapeDtypeStruct((b, NSA_KV_WIDTH, nc), BF16)],
        compiler_params=_cparams(("parallel",)),
        name="nsa_compress",
    )(kc_in, vc_in, *wk, *wv, cos_e, sin_e)


def _lane_tiled(x):
    return jnp.concatenate([x] * NSA_REP, axis=1)


def _queries_t(qt, g, other, q_first):
    cols = []
    for r in range(NSA_REP):
        hq = g * NSA_REP + r
        piece = qt[hq * HEAD_DIM:(hq + 1) * HEAD_DIM]
        cols.append(jnp.concatenate([piece, other] if q_first else [other, piece], axis=0))
    return jnp.concatenate(cols, axis=1)


def _gated_heads_from_t(pieces, gate_ref, branch):
    sg = _sigmoid(jnp.transpose(gate_ref[0]))
    rows = [hq * N_BRANCH + branch for hq in range(NSA_HEADS)]
    gated = [p * sg[r:r + 1] for p, r in zip(pieces, rows)]
    slabs = [jnp.transpose(jnp.concatenate(gated[2 * j:2 * j + 2], axis=0)) for j in range(NSA_HEADS // 2)]
    return jnp.concatenate(slabs, axis=1)


def _normalised_heads(acc, tq):
    ot = acc[0:HEAD_DIM] * (1.0 / acc[HEAD_DIM:HEAD_DIM + 1])
    return [ot[:, r * tq:(r + 1) * tq] for r in range(NSA_REP)]


def _cmpattn_kernel(qt_ref, gate_ref, kc_ref, vct_ref, ovt_ref, o_ref, nsel_ref, *, tq, n_slc):
    i = pl.program_id(1)
    qt = qt_ref[0]
    kc = kc_ref[0]
    vct = vct_ref[0]
    nc = kc.shape[0]
    n = lax.broadcasted_iota(jnp.int32, (nc, tq), 0)
    t = i * tq + lax.broadcasted_iota(jnp.int32, (nc, tq), 1)
    bias = _lane_tiled(jnp.where(n * CMP_STRIDE + (CMP_BLOCK - 1) <= t, 0.0, NEG_INF))
    sees_any = _lane_tiled(jnp.where(i * tq + lax.broadcasted_iota(jnp.int32, (1, tq), 1) >= CMP_BLOCK - 1, 1.0, 0.0))
    zeros = jnp.zeros((HEAD_DIM, tq), BF16)
    blk = lax.broadcasted_iota(jnp.int32, (n_slc, tq), 0)
    tl = i * tq + lax.broadcasted_iota(jnp.int32, (n_slc, tq), 1)
    cur = tl // SLC_BLOCK
    forced = (blk == 0) | (blk == cur) | (blk == cur - 1)
    valid = blk * SLC_BLOCK <= tl
    ovt = ovt_ref[...]
    pieces = []
    scores = []
    for g in range(NSA_KV_HEADS):
        s = _dot(kc, _queries_t(qt, g, zeros, g == 0)) + bias
        e = jnp.exp2(s - jnp.max(s, axis=0, keepdims=True))
        p = e * (sees_any * (1.0 / jnp.sum(e, axis=0, keepdims=True)))
        ot = _dot(vct, p.astype(BF16))
        pieces.extend(ot[g * HEAD_DIM:(g + 1) * HEAD_DIM, r * tq:(r + 1) * tq] for r in range(NSA_REP))
        psum = p[:, 0:tq] + p[:, tq:2 * tq] + p[:, 2 * tq:3 * tq]
        hi = psum.astype(BF16)
        rest = psum - hi.astype(F32)
        mid = rest.astype(BF16)
        lo = (rest - mid.astype(F32)).astype(BF16)
        imp = _dot(ovt, hi) + _dot(ovt, mid) + _dot(ovt, lo)
        scores.append(jnp.where(valid, jnp.where(forced, FORCE_SCORE, imp), -1.0))
    o_ref[0] = _gated_heads_from_t(pieces, gate_ref, 0)

    n_live = ((i + 1) * tq) // SLC_BLOCK
    n_cls = n_slc // RANK_STEP
    cls = jnp.minimum((n_live - 1) // RANK_STEP, n_cls - 1)
    for c in range(n_cls):
        @pl.when(cls == c)
        def _(c=c):
            for g in range(NSA_KV_HEADS):
                nsel_ref[0, g] = _selection_bias(scores[g], (c + 1) * RANK_STEP, min(SLC_TOP, n_slc), tq)


def _selection_bias(score, nb, top, tq):
    sub = lax.broadcasted_iota(jnp.int32, (SUBLANES, tq), 0)
    ties = [jnp.where(sub > k, 1.0, 0.0) for k in range(SUBLANES)]
    groups = [score[SUBLANES * r:SUBLANES * (r + 1)] for r in range(nb // SUBLANES)]
    ranks = [jnp.zeros((SUBLANES, tq), F32) for _ in groups]
    for jp in range(nb):
        rowv = score[jp:jp + 1, :]
        for r, grp in enumerate(groups):
            if SUBLANES * r > jp:
                inc = jnp.where(rowv >= grp, 1.0, 0.0)
            elif SUBLANES * r + SUBLANES - 1 < jp:
                inc = jnp.where(rowv > grp, 1.0, 0.0)
            else:
                inc = jnp.where(rowv > grp, 1.0, jnp.where(rowv == grp, ties[jp - SUBLANES * r], 0.0))
            ranks[r] = ranks[r] + inc
    bias = jnp.where(jnp.concatenate(ranks, axis=0) < float(top), 0.0, NEG_INF)
    if nb < SEL_ROWS:
        bias = jnp.concatenate([bias, jnp.full((SEL_ROWS - nb, tq), NEG_INF, F32)], axis=0)
    return bias.astype(BF16)


def _cmpattn(qt, gate, kc, vct, ovt):
    b, _, t = qt.shape
    nc = kc.shape[1]
    tq = CMP_TQ
    n_slc = t // SLC_BLOCK
    assert n_slc <= SEL_ROWS and n_slc % RANK_STEP == 0
    blk = lambda w: pl.BlockSpec((1, tq, w), lambda i, j: (i, j, 0))
    return pl.pallas_call(
        functools.partial(_cmpattn_kernel, tq=tq, n_slc=n_slc),
        grid=(b, t // tq),
        in_specs=[pl.BlockSpec((1, NSA_WIDTH, tq), lambda i, j: (i, 0, j)), blk(LANES),
                  pl.BlockSpec((1, nc, NSA_KV_WIDTH), lambda i, j: (i, 0, 0)),
                  pl.BlockSpec((1, NSA_KV_WIDTH, nc), lambda i, j: (i, 0, 0)),
                  pl.BlockSpec(ovt.shape, lambda i, j: (0, 0))],
        out_specs=[blk(NSA_WIDTH),
                   pl.BlockSpec((1, NSA_KV_HEADS, SEL_ROWS, tq), lambda i, j: (i, 0, 0, j))],
        out_shape=[jax.ShapeDtypeStruct((b, t, NSA_WIDTH), F32),
                   jax.ShapeDtypeStruct((b, NSA_KV_HEADS, SEL_ROWS, t), BF16)],
        compiler_params=_cparams(("parallel", "parallel")),
        name="nsa_compressed_attn_select",
    )(qt, gate, kc, vct, ovt)


def _slc_kernel(qt_ref, nsel_ref, gate_ref, ksa_ref, vst_ref, o_ref, *, tq, kc):
    i = pl.program_id(1)
    qt = qt_ref[0]
    c_diag = (i * tq) // kc
    kpos = lax.broadcasted_iota(jnp.int32, (kc, tq), 0)
    tpos = i * tq + lax.broadcasted_iota(jnp.int32, (kc, tq), 1)
    qas = [_queries_t(qt, g, nsel_ref[0, g], True) for g in range(NSA_KV_HEADS)]

    def step(c, carry, causal):
        out = []
        for g, (m, acc) in enumerate(carry):
            k = ksa_ref[0, g, pl.ds(pl.multiple_of(c * kc, kc), kc), :]
            s = _dot(k, qas[g])
            if causal:
                s = s + _lane_tiled(jnp.where(c * kc + kpos <= tpos, 0.0, NEG_INF))
            m_new = jnp.maximum(m, jnp.max(s, axis=0, keepdims=True))
            p = jnp.exp2(s - m_new).astype(BF16)
            per = kc // SEL_CHUNK
            v = jnp.concatenate([vst_ref[0, g, c * per + u] for u in range(per)], axis=1)
            out.append((m_new, jnp.exp2(m - m_new) * acc + _dot(v, p)))
        return tuple(out)

    init = tuple((jnp.full((1, NSA_REP * tq), NEG_INF, F32), jnp.zeros((V_ROWS, NSA_REP * tq), F32))
                 for _ in range(NSA_KV_HEADS))
    carry = lax.fori_loop(0, c_diag, lambda c, carry: step(c, carry, False), init)
    carry = step(c_diag, carry, True)
    pieces = []
    for _, acc in carry:
        pieces.extend(_normalised_heads(acc, tq))
    o_ref[0] = _gated_heads_from_t(pieces, gate_ref, 1)


def _slc(qt, nsel, gate, ksa, vst):
    b, _, t = qt.shape
    tq = SEL_TQ
    kc = min(t, SEL_STEP)
    g = NSA_KV_HEADS
    return pl.pallas_call(
        functools.partial(_slc_kernel, tq=tq, kc=kc),
        grid=(b, t // tq),
        in_specs=[pl.BlockSpec((1, NSA_WIDTH, tq), lambda i, j: (i, 0, j)),
                  pl.BlockSpec((1, g, SEL_ROWS, tq), lambda i, j: (i, 0, 0, j)),
                  pl.BlockSpec((1, tq, LANES), lambda i, j: (i, j, 0)),
                  pl.BlockSpec((1, g, t, LANES), lambda i, j: (i, 0, 0, 0)),
                  pl.BlockSpec((1, g, t // SEL_CHUNK, V_ROWS, SEL_CHUNK), lambda i, j: (i, 0, 0, 0, 0))],
        out_specs=pl.BlockSpec((1, tq, NSA_WIDTH), lambda i, j: (i, j, 0)),
        out_shape=jax.ShapeDtypeStruct((b, t, NSA_WIDTH), F32),
        compiler_params=_cparams(("parallel", "parallel")),
        name="nsa_selected_attn",
    )(qt, nsel, gate, ksa, vst)


def _win_kernel(qt_ref, gate_ref, kw_ref, vwt_ref, o_ref, *, tq, nblk, t_total):
    i = pl.program_id(1)
    qt = qt_ref[0]
    kb = ATT_TQ
    span = nblk * kb
    first_blk = jnp.clip(i * (tq // kb) - WINDOW // kb, 0, t_total // kb - nblk)
    start = pl.multiple_of(first_blk * kb, kb)
    k = kw_ref[0, pl.ds(start, span), :]
    d = (i * tq + lax.broadcasted_iota(jnp.int32, (span, tq), 1)
         - (start + lax.broadcasted_iota(jnp.int32, (span, tq), 0)))
    bias = _lane_tiled(jnp.where((d >= 0) & (d < WINDOW), 0.0, NEG_INF))
    zeros = jnp.zeros((HEAD_DIM, tq), BF16)
    pieces = []
    for g in range(NSA_KV_HEADS):
        s = _dot(k, _queries_t(qt, g, zeros, g == 0)) + bias
        p = jnp.exp2(s - jnp.max(s, axis=0, keepdims=True)).astype(BF16)
        vt = jnp.concatenate([vwt_ref[0, g, first_blk + u] for u in range(nblk)], axis=1)
        pieces.extend(_normalised_heads(_dot(vt, p), tq))
    o_ref[0] = _gated_heads_from_t(pieces, gate_ref, 2)


def _win(qt, gate, kw, vwt):
    b, _, t = qt.shape
    tq = min(t, WIN_TQ)
    kb = ATT_TQ
    nblk = min(t // kb, (WINDOW + tq) // kb)
    g = NSA_KV_HEADS
    return pl.pallas_call(
        functools.partial(_win_kernel, tq=tq, nblk=nblk, t_total=t),
        grid=(b, t // tq),
        in_specs=[pl.BlockSpec((1, NSA_WIDTH, tq), lambda i, j: (i, 0, j)),
                  pl.BlockSpec((1, tq, LANES), lambda i, j: (i, j, 0)),
                  pl.BlockSpec((1, t, NSA_KV_WIDTH), lambda i, j: (i, 0, 0)),
                  pl.BlockSpec((1, g, t // kb, V_ROWS, kb), lambda i, j: (i, 0, 0, 0, 0))],
        out_specs=pl.BlockSpec((1, tq, NSA_WIDTH), lambda i, j: (i, j, 0)),
        out_shape=jax.ShapeDtypeStruct((b, t, NSA_WIDTH), F32),
        compiler_params=_cparams(("parallel", "parallel")),
        name="nsa_window_attn",
    )(qt, gate, kw, vwt)


N_MIX_IN = 6


def _post_kernel(*refs, alpha, fc):
    tiles, halos = refs[0:N_MIX_IN], refs[N_MIX_IN:2 * N_MIX_IN]
    (wo_ref, g1_ref, b1_ref, wg_ref, wu_ref, cw_ref, cb_ref, wd_ref, g2_ref, b2_ref,
     o_ref, x1_ref, xb_ref, act_ref) = refs[2 * N_MIX_IN:]
    j = pl.program_id(1)
    h = CONV_HALO

    def mixed(x_ref, ya_ref, yb_ref, oc_ref, os_ref, ow_ref):
        yc = oc_ref[0] + os_ref[0] + ow_ref[0]
        y = jnp.concatenate([ya_ref[0], yb_ref[0], yc], axis=1).astype(BF16)
        return _layernorm(alpha * x_ref[0] + _dot(y, wo_ref[...]), g1_ref[...], b1_ref[...])

    x1_ref[...] = mixed(*tiles)
    xb_ref[0:h, :] = jnp.where(j > 0, mixed(*halos), 0.0).astype(BF16)
    xb_ref[h:, :] = x1_ref[...].astype(BF16)
    for c in range(0, wg_ref.shape[1], fc):
        hg = _dot(xb_ref[...], wg_ref[:, c:c + fc])
        up = _dot(xb_ref[h:, :], wu_ref[:, c:c + fc])
        hc = (cb_ref[:, c:c + fc] + pltpu.roll(hg, 2, 0)[h:] * cw_ref[0:1, c:c + fc]
              + pltpu.roll(hg, 1, 0)[h:] * cw_ref[1:2, c:c + fc] + hg[h:] * cw_ref[2:3, c:c + fc])
        act_ref[:, c:c + fc] = (_gelu_tanh(hc) * up).astype(BF16)
    o_ref[0] = _layernorm(alpha * x1_ref[...] + _dot(act_ref[...], wd_ref[...]), g2_ref[...], b2_ref[...])


def _post(mix_in, wo, g1, b1, wg, wu, cw, cb, wd, g2, b2, alpha):
    b, t, d = mix_in[0].shape
    dff = wg.shape[1]
    tm = min(t, 512)
    fc = 2 * LANES
    r = tm // CONV_HALO
    const = lambda a: pl.BlockSpec(a.shape, lambda i, j: (0, 0), pipeline_mode=pl.Buffered(1))
    tile = lambda a: pl.BlockSpec((1, tm, a.shape[2]), lambda i, j: (i, j, 0))
    halo = lambda a: pl.BlockSpec((1, CONV_HALO, a.shape[2]), lambda i, j: (i, jnp.maximum(j * r - 1, 0), 0))
    weights = (wo, g1, b1, wg, wu, cw, cb, wd, g2, b2)
    assert len(mix_in) == N_MIX_IN
    return pl.pallas_call(
        functools.partial(_post_kernel, alpha=alpha, fc=fc),
        grid=(b, t // tm),
        in_specs=[tile(a) for a in mix_in] + [halo(a) for a in mix_in] + [const(a) for a in weights],
        out_specs=pl.BlockSpec((1, tm, d), lambda i, j: (i, j, 0)),
        out_shape=jax.ShapeDtypeStruct((b, t, d), F32),
        scratch_shapes=[pltpu.VMEM((tm, d), F32), pltpu.VMEM((CONV_HALO + tm, d), BF16),
                        pltpu.VMEM((tm, dff), BF16)],
        compiler_params=_cparams(("parallel", "parallel")),
        name="out_proj_ffn",
    )(*mix_in, *mix_in, *weights)


def _overlap_t(nc, n_slc):
    ci = np.arange(nc)[None, :]
    sj = np.arange(n_slc)[:, None]
    ov = np.clip(np.minimum(ci * CMP_STRIDE + CMP_BLOCK, (sj + 1) * SLC_BLOCK)
                 - np.maximum(ci * CMP_STRIDE, sj * SLC_BLOCK), 0, None).astype(np.float32) / CMP_STRIDE
    ov[:, nc - 1] = 0.0
    return jnp.asarray(ov, BF16)


def kernel(x, positions, w_in, w_out, pool_w, pool_scale, ret_gn_g, cmp_pos_k, cmp_w1_k, cmp_b1_k, cmp_w2_k,
           cmp_pos_v, cmp_w1_v, cmp_b1_v, cmp_w2_v, ffn_w_gate, ffn_w_up, ffn_conv_w, ffn_conv_b, ffn_w_down,
           ln1_g, ln1_b, ln2_g, ln2_b):
    b, t, d = x.shape
    depth = w_in.shape[0]
    alpha = float((2 * depth) ** 0.25)
    nc = t // CMP_STRIDE
    n_slc = t // SLC_BLOCK

    inv = ROPE_THETA ** (-jnp.arange(0, HEAD_DIM, 2, dtype=F32) / HEAD_DIM)
    inv = jnp.tile(inv, LANES // (HEAD_DIM // 2))[None, :]
    cos, sin = _rope_tables(positions, inv)
    ends = jnp.minimum(jnp.arange(nc) * CMP_STRIDE + CMP_BLOCK - 1, t - 1)
    cos_e, sin_e = _rope_tables(positions[:, ends], inv)
    ovt = _overlap_t(nc, n_slc)
    eye_g = jnp.eye(len(POOL_WINDOWS), dtype=F32)

    for l in range(depth):
        w_in_p = jnp.pad(w_in[l], ((0, 0), (0, IN_PAD - IN_WIDTH))).astype(BF16)
        wbd = jnp.einsum('gcd,ge->gced', pool_w[l], eye_g).reshape(POOL_WIDTH, POOL_WIDTH).astype(BF16)
        y_a, h_ret, qt, k_cmp, v_cmp, gate3, ksa, vst, kw, vwt = _inproj(
            x, cos, sin, w_in_p, wbd, pool_scale[l][None, :])

        y_b = _retention(h_ret, ret_gn_g[l][None, :])

        wk = _compress_weights(cmp_pos_k[l], cmp_w1_k[l], cmp_b1_k[l], cmp_w2_k[l])
        wv = _compress_weights(cmp_pos_v[l], cmp_w1_v[l], cmp_b1_v[l], cmp_w2_v[l])
        kc, vct = _compress(k_cmp.reshape(b, nc, CMP_STRIDE * NSA_KV_WIDTH),
                            v_cmp.reshape(b, nc, CMP_STRIDE * NSA_KV_WIDTH), wk, wv, cos_e, sin_e)
        o_cmp, nsel = _cmpattn(qt, gate3, kc, vct, ovt)
        o_slc = _slc(qt, nsel, gate3, ksa, vst)
        o_win = _win(qt, gate3, kw, vwt)

        x = _post((x, y_a, y_b, o_cmp, o_slc, o_win), w_out[l].astype(BF16), ln1_g[l][None, :], ln1_b[l][None, :],
                  ffn_w_gate[l].astype(BF16), ffn_w_up[l].astype(BF16), ffn_conv_w[l], ffn_conv_b[l][None, :],
                  ffn_w_down[l].astype(BF16), ln2_g[l][None, :], ln2_b[l][None, :], alpha)
    return x
```

```python
import functools
import math

import jax
import jax.numpy as jnp
import numpy as np
from jax import lax
from jax.experimental import pallas as pl
from jax.experimental.pallas import tpu as pltpu

F32 = jnp.float32
BF16 = jnp.bfloat16

HEAD_DIM = 64
POOL_WINDOWS = (2, 4, 8, 16)
POOL_GROUP_WIDTH = 64
POOL_WIDTH = POOL_GROUP_WIDTH * len(POOL_WINDOWS)
POOL_HALO = 16
RET_HEADS = 6
RET_WIDTH = RET_HEADS * HEAD_DIM
RET_CHUNK = 256
NSA_HEADS = 6
NSA_WIDTH = NSA_HEADS * HEAD_DIM
NSA_KV_HEADS = 2
NSA_REP = NSA_HEADS // NSA_KV_HEADS
NSA_KV_WIDTH = NSA_KV_HEADS * HEAD_DIM
CMP_BLOCK = 32
CMP_STRIDE = 16
CMP_HIDDEN = 128
SLC_BLOCK = 64
SLC_TOP = 16
WINDOW = 512
N_BRANCH = 3
FORCE_SCORE = 1e6
CONV_WIDTH = 3
CONV_HALO = 16
ROPE_THETA = 10000.0
LN_EPS = 1e-5
GN_EPS = 1e-5
NEG_INF = -1e30
Q_SCALE = HEAD_DIM ** -0.5 * math.log2(math.e)

LANES = 128
SUBLANES = 8
VMEM_LIMIT = 56 * 1024 * 1024

ATT_TQ = 128
SEL_TQ = 512
SEL_CHUNK = 512
SEL_STEP = 512
SEL_ROWS = LANES - HEAD_DIM
V_ROWS = HEAD_DIM + 16
RANK_STEP = 16
WIN_TQ = 256
RET_STEP_CHUNKS = 4
CMP_TQ = 256

OFF_POOL = 0
OFF_RET = OFF_POOL + POOL_WIDTH
OFF_QNSA = OFF_RET + 4 * RET_WIDTH
OFF_KCMP = OFF_QNSA + NSA_WIDTH
OFF_VCMP = OFF_KCMP + NSA_KV_WIDTH
OFF_KV4 = OFF_VCMP + NSA_KV_WIDTH
OFF_GATE = OFF_KV4 + 4 * NSA_KV_WIDTH
IN_WIDTH = OFF_GATE + NSA_HEADS * N_BRANCH
IN_PAD = OFF_GATE + LANES


def _cparams(sem, flags=None):
    return pltpu.CompilerParams(dimension_semantics=sem, vmem_limit_bytes=VMEM_LIMIT, flags=flags)


def _rope(x, cos, sin_signed):
    lane = lax.broadcasted_iota(jnp.int32, x.shape, 1)
    first = (lane & (HEAD_DIM - 1)) < HEAD_DIM // 2
    partner = jnp.where(first, pltpu.roll(x, LANES - HEAD_DIM // 2, 1), pltpu.roll(x, HEAD_DIM // 2, 1))
    return x * cos + partner * sin_signed


def _gelu_tanh(x):
    return 0.5 * x * (1.0 + jnp.tanh(math.sqrt(2.0 / math.pi) * (x + 0.044715 * (x * x * x))))


def _sigmoid(x):
    return 1.0 / (1.0 + jnp.exp(-x))


def _layernorm(y, g, b):
    mu = jnp.mean(y, axis=-1, keepdims=True)
    d = y - mu
    var = jnp.mean(d * d, axis=-1, keepdims=True)
    return d * lax.rsqrt(var + LN_EPS) * g + b


def _dot(a, b):
    return jnp.dot(a, b, preferred_element_type=F32)


def _dot_nt(a, b):
    return lax.dot_general(a, b, (((1,), (1,)), ((), ())), preferred_element_type=F32)


def _rope_table_kernel(pos_ref, inv_ref, cos_ref, sin_ref):
    ang = pos_ref[0].astype(F32) * inv_ref[...]
    lane = lax.broadcasted_iota(jnp.int32, ang.shape, 1)
    first = (lane & (HEAD_DIM - 1)) < HEAD_DIM // 2
    s = jnp.sin(ang)
    cos_ref[0] = jnp.cos(ang)
    sin_ref[0] = jnp.where(first, -s, s)


def _rope_tables(pos, inv):
    b, n = pos.shape
    tt = min(n, 512)
    shp = jax.ShapeDtypeStruct((b, n, LANES), F32)
    return pl.pallas_call(
        _rope_table_kernel,
        grid=(b, n // tt),
        in_specs=[pl.BlockSpec((1, tt, 1), lambda i, j: (i, j, 0)),
                  pl.BlockSpec((1, LANES), lambda i, j: (0, 0))],
        out_specs=[pl.BlockSpec((1, tt, LANES), lambda i, j: (i, j, 0))] * 2,
        out_shape=[shp, shp],
        compiler_params=_cparams(("parallel", "parallel")),
        name="rope_tables",
    )(pos.reshape(b, n, 1), inv)


def _pool_mix(cur, prev, t0, w, scale):
    cat = jnp.concatenate([prev, cur], axis=0)
    s2 = cat + pltpu.roll(cat, 1, 0)
    s4 = s2 + pltpu.roll(s2, 2, 0)
    s8 = s4 + pltpu.roll(s4, 4, 0)
    s16 = s8 + pltpu.roll(s8, 8, 0)
    grp = lax.broadcasted_iota(jnp.int32, cur.shape, 1) // POOL_GROUP_WIDTH
    t = t0 + lax.broadcasted_iota(jnp.int32, cur.shape, 0)

    def pick(a, b, c, d):
        return jnp.where(grp == 0, a, jnp.where(grp == 1, b, jnp.where(grp == 2, c, d)))

    h = POOL_HALO
    wsum = pick(s2[h:], s4[h:], s8[h:], s16[h:])
    width = pick(*[float(wd) for wd in POOL_WINDOWS])
    cnt = jnp.minimum((t + 1).astype(F32), width)
    mixed = wsum / cnt - cur
    return _dot(mixed.astype(BF16), w) * scale


def _inproj_kernel(x_ref, xh_ref, cos_ref, sin_ref, w_ref, wpool_ref, pscale_ref,
                   ya_ref, ret_ref, qt_ref, kcmp_ref, vcmp_ref, gate_ref, ksa_ref, vst_ref, kw_ref, vwt_ref,
                   stage_ref, *, tm):
    j = pl.program_id(1)
    cos = cos_ref[0]
    sin = sin_ref[0]
    xb = x_ref[0].astype(BF16)
    lane = lax.broadcasted_iota(jnp.int32, (tm, LANES), 1)
    lo = lane < HEAD_DIM
    tpos = j * tm + lax.broadcasted_iota(jnp.int32, (tm, LANES), 0)
    onehot = jnp.where(lane - HEAD_DIM == tpos // SLC_BLOCK, 1.0, 0.0)
    top = lax.broadcasted_iota(jnp.int32, (LANES, tm), 0) < HEAD_DIM

    def store_t(v, ref, chunk):
        vt = jnp.transpose(v)
        per_head = (jnp.where(top, vt, 1.0), jnp.where(top, pltpu.roll(vt, HEAD_DIM, 0), 1.0))
        for g, vg in enumerate(per_head):
            for u in range(tm // chunk):
                ref[0, g, u] = vg[0:V_ROWS, u * chunk:(u + 1) * chunk].astype(BF16)

    def emit(col, y):
        if col < OFF_RET:
            return
        if col < OFF_QNSA:
            part, k = divmod(col - OFF_RET, RET_WIDTH)
            if part == 0:
                y = _rope(y, cos, sin)
            elif part == 1:
                y = _rope(y, cos, sin) * (HEAD_DIM ** -0.5)
            ret_ref[0, :, col - OFF_RET:col - OFF_RET + LANES] = y
        elif col < OFF_KCMP:
            k = col - OFF_QNSA
            qt_ref[0, k:k + LANES, :] = jnp.transpose(_rope(y, cos, sin) * Q_SCALE).astype(BF16)
        elif col in (OFF_KCMP, OFF_VCMP):
            out_ref = kcmp_ref if col == OFF_KCMP else vcmp_ref
            stage_ref[...] = y
            for l in range(CMP_STRIDE):
                out_ref[0, :, l * LANES:(l + 1) * LANES] = stage_ref[pl.ds(l, tm // CMP_STRIDE, stride=CMP_STRIDE), :]
        elif col == OFF_KV4:
            ks = _rope(y, cos, sin)
            ksa_ref[0, 0] = jnp.where(lo, ks, onehot).astype(BF16)
            ksa_ref[0, 1] = jnp.where(lo, pltpu.roll(ks, HEAD_DIM, 1), onehot).astype(BF16)
        elif col == OFF_KV4 + NSA_KV_WIDTH:
            store_t(y, vst_ref, SEL_CHUNK)
        elif col == OFF_KV4 + 2 * NSA_KV_WIDTH:
            kw_ref[0] = _rope(y, cos, sin).astype(BF16)
        elif col == OFF_KV4 + 3 * NSA_KV_WIDTH:
            store_t(y, vwt_ref, ATT_TQ)
        else:
            gate_ref[0] = y

    for c in range(0, IN_PAD, 2 * LANES):
        y = _dot(xb, w_ref[:, c:c + 2 * LANES])
        if c == OFF_POOL:
            prev = jnp.where(j > 0, _dot(xh_ref[0].astype(BF16), w_ref[:, 0:POOL_WIDTH]), 0.0)
            ya_ref[0] = _pool_mix(y, prev, j * tm, wpool_ref[...], pscale_ref[...])
        for half in range(2):
            emit(c + half * LANES, y[:, half * LANES:(half + 1) * LANES])


def _inproj(x, cos, sin, w, wpool, pscale):
    b, t, d = x.shape
    tm = min(t, 512)
    g = NSA_KV_HEADS
    r = tm // POOL_HALO
    row = lambda wd: pl.BlockSpec((1, tm, wd), lambda i, j: (i, j, 0))
    const = lambda a: pl.BlockSpec(a.shape, lambda i, j: (0, 0), pipeline_mode=pl.Buffered(1))
    strided = pl.BlockSpec((1, tm // CMP_STRIDE, CMP_STRIDE * NSA_KV_WIDTH), lambda i, j: (i, j, 0))
    outs = [
        (row(POOL_WIDTH), (b, t, POOL_WIDTH), F32),
        (row(4 * RET_WIDTH), (b, t, 4 * RET_WIDTH), F32),
        (pl.BlockSpec((1, NSA_WIDTH, tm), lambda i, j: (i, 0, j)), (b, NSA_WIDTH, t), BF16),
        (strided, (b, t // CMP_STRIDE, CMP_STRIDE * NSA_KV_WIDTH), F32),
        (strided, (b, t // CMP_STRIDE, CMP_STRIDE * NSA_KV_WIDTH), F32),
        (row(LANES), (b, t, LANES), F32),
        (pl.BlockSpec((1, g, tm, LANES), lambda i, j: (i, 0, j, 0)), (b, g, t, LANES), BF16),
        (pl.BlockSpec((1, g, tm // SEL_CHUNK, V_ROWS, SEL_CHUNK), lambda i, j: (i, 0, j, 0, 0)),
         (b, g, t // SEL_CHUNK, V_ROWS, SEL_CHUNK), BF16),
        (row(NSA_KV_WIDTH), (b, t, NSA_KV_WIDTH), BF16),
        (pl.BlockSpec((1, g, tm // ATT_TQ, V_ROWS, ATT_TQ), lambda i, j: (i, 0, j, 0, 0)),
         (b, g, t // ATT_TQ, V_ROWS, ATT_TQ), BF16),
    ]
    return pl.pallas_call(
        functools.partial(_inproj_kernel, tm=tm),
        grid=(b, t // tm),
        in_specs=[row(d), pl.BlockSpec((1, POOL_HALO, d), lambda i, j: (i, jnp.maximum(j * r - 1, 0), 0)),
                  row(LANES), row(LANES), const(w), const(wpool), const(pscale)],
        out_specs=[o[0] for o in outs],
        out_shape=[jax.ShapeDtypeStruct(o[1], o[2]) for o in outs],
        scratch_shapes=[pltpu.VMEM((tm, NSA_KV_WIDTH), F32)],
        compiler_params=_cparams(("parallel", "parallel")),
        name="in_proj",
    )(x, x, cos, sin, w, wpool, pscale)


def _ret_consts():
    h, c = RET_HEADS, RET_CHUNK
    lg = np.log1p(-np.power(2.0, -5.0 - np.arange(h, dtype=np.float64)))
    i = np.arange(c, dtype=np.float64)
    diff = i[:, None] - i[None, :]
    dmask = np.where(diff >= 0, np.exp(lg[:, None, None] * np.maximum(diff, 0.0)), 0.0)
    xi = np.repeat(np.exp(lg[:, None] * (i + 1.0)).T, HEAD_DIM, axis=1)
    zeta = np.repeat(np.exp(lg[:, None] * (c - 1.0 - i)).T, HEAD_DIM, axis=1)
    gc = np.repeat(np.exp(lg * c), HEAD_DIM)[None, :]
    return (jnp.asarray(dmask, F32), jnp.asarray(xi, F32), jnp.asarray(zeta, F32), jnp.asarray(gc, F32))


def _ret_kernel(h_ref, dmask_ref, xi_ref, zeta_ref, gc_ref, gn_ref, o_ref, state_ref):
    @pl.when(pl.program_id(1) == 0)
    def _():
        state_ref[...] = jnp.zeros_like(state_ref)

    c = RET_CHUNK
    lane = lax.broadcasted_iota(jnp.int32, (c, LANES), 1)
    row = lax.broadcasted_iota(jnp.int32, (LANES, LANES), 0)
    col = lax.broadcasted_iota(jnp.int32, (LANES, LANES), 1)
    lo = lane < HEAD_DIM
    same_head = (row < HEAD_DIM) == (col < HEAD_DIM)
    states = [state_ref[j] for j in range(RET_WIDTH // LANES)]
    for u in range(h_ref.shape[1] // c):
        rs = slice(u * c, (u + 1) * c)
        for j in range(RET_WIDTH // LANES):
            sl = slice(j * LANES, (j + 1) * LANES)
            q = h_ref[0, rs, j * LANES:(j + 1) * LANES]
            k = h_ref[0, rs, RET_WIDTH + j * LANES:RET_WIDTH + (j + 1) * LANES]
            v = h_ref[0, rs, 2 * RET_WIDTH + j * LANES:2 * RET_WIDTH + (j + 1) * LANES]
            g = h_ref[0, rs, 3 * RET_WIDTH + j * LANES:3 * RET_WIDTH + (j + 1) * LANES]
            kb = k.astype(BF16)
            vb = v.astype(BF16)
            o = _dot(q.astype(BF16), states[j].astype(BF16)) * xi_ref[:, sl]
            for hh in range(2):
                m = lo if hh == 0 else jnp.logical_not(lo)
                qm = jnp.where(m, q, 0.0).astype(BF16)
                s = _dot_nt(qm, kb) * dmask_ref[2 * j + hh]
                o = o + jnp.where(m, _dot(s.astype(BF16), vb), 0.0)
            kz = (k * zeta_ref[:, sl]).astype(BF16)
            kv = lax.dot_general(kz, vb, (((0,), (0,)), ((), ())), preferred_element_type=F32)
            states[j] = gc_ref[:, sl] * states[j] + jnp.where(same_head, kv, 0.0)
            s_lo = jnp.sum(jnp.where(lo, o, 0.0), axis=-1, keepdims=True)
            s_hi = jnp.sum(jnp.where(lo, 0.0, o), axis=-1, keepdims=True)
            d = o - jnp.where(lo, s_lo, s_hi) * (1.0 / HEAD_DIM)
            d2 = d * d
            v_lo = jnp.sum(jnp.where(lo, d2, 0.0), axis=-1, keepdims=True)
            v_hi = jnp.sum(jnp.where(lo, 0.0, d2), axis=-1, keepdims=True)
            var = jnp.where(lo, v_lo, v_hi) * (1.0 / HEAD_DIM)
            on = d * lax.rsqrt(var + GN_EPS) * gn_ref[:, sl]
            o_ref[0, rs, sl] = g * _sigmoid(g) * on
    for j, st in enumerate(states):
        state_ref[j] = st


def _retention(hret, gn_g):
    b, t, _ = hret.shape
    c = RET_CHUNK
    rows = min(t, RET_STEP_CHUNKS * c)
    dmask, xi, zeta, gc = _ret_consts()
    const = lambda shape: pl.BlockSpec(shape, lambda i, j: (0,) * len(shape))
    return pl.pallas_call(
        _ret_kernel,
        grid=(b, t // rows),
        in_specs=[pl.BlockSpec((1, rows, 4 * RET_WIDTH), lambda i, j: (i, j, 0)),
                  const((RET_HEADS, c, c)), const((c, RET_WIDTH)), const((c, RET_WIDTH)),
                  const((1, RET_WIDTH)), const((1, RET_WIDTH))],
        out_specs=pl.BlockSpec((1, rows, RET_WIDTH), lambda i, j: (i, j, 0)),
        out_shape=jax.ShapeDtypeStruct((b, t, RET_WIDTH), F32),
        scratch_shapes=[pltpu.VMEM((RET_WIDTH // LANES, LANES, LANES), F32)],
        compiler_params=_cparams(("parallel", "arbitrary")),
        name="retention",
    )(hret, dmask, xi, zeta, gc, gn_g)


def _compress_kernel(k_ref, v_ref, ptk, pbk, wtk, wbk, b1k, w2k, ptv, pbv, wtv, wbv, b1v, w2v,
                     cos_ref, sin_ref, kc_ref, vct_ref):
    def comp(x_ref, pt, pb, wt, wb, b1, w2):
        x = x_ref[0]
        top = _dot((x + pt[...]).astype(BF16), wt[...])
        bot = _dot((x + pb[...]).astype(BF16), wb[...])
        rows = bot.shape[0]
        pre = top + pltpu.roll(bot, rows - 1, 0) + b1[...]
        return _dot(_gelu_tanh(pre).astype(BF16), w2[...])

    kc = comp(k_ref, ptk, pbk, wtk, wbk, b1k, w2k)
    kc_ref[0] = _rope(kc, cos_ref[0], sin_ref[0]).astype(BF16)
    vct_ref[0] = jnp.transpose(comp(v_ref, ptv, pbv, wtv, wbv, b1v, w2v)).astype(BF16)


def _compress_weights(pos, w1, b1, w2):
    half = CMP_BLOCK // 2
    g = NSA_KV_HEADS
    w1r = w1.reshape(CMP_BLOCK, HEAD_DIM, CMP_HIDDEN)
    eye = jnp.eye(g, dtype=F32)

    def lay(wh):
        return jnp.einsum('ldh,ge->lgdeh', wh, eye).reshape(half * g * HEAD_DIM, g * CMP_HIDDEN)

    def tile_pos(p):
        return jnp.broadcast_to(p[:, None, :], (half, g, HEAD_DIM)).reshape(1, half * g * HEAD_DIM)

    w2bd = jnp.einsum('hd,ge->ghed', w2, eye).reshape(g * CMP_HIDDEN, g * HEAD_DIM)
    return (tile_pos(pos[:half]), tile_pos(pos[half:]), lay(w1r[:half]).astype(BF16), lay(w1r[half:]).astype(BF16),
            jnp.tile(b1, g)[None, :], w2bd.astype(BF16))


def _compress(kc_in, vc_in, wk, wv, cos_e, sin_e):
    b, nc, width = kc_in.shape
    const = lambda a: pl.BlockSpec(a.shape, lambda i: (0,) * a.ndim)
    blk = lambda w: pl.BlockSpec((1, nc, w), lambda i: (i, 0, 0))
    return pl.pallas_call(
        _compress_kernel,
        grid=(b,),
        in_specs=[blk(width), blk(width)] + [const(a) for a in wk] + [const(a) for a in wv]
                 + [blk(LANES), blk(LANES)],
        out_specs=[blk(NSA_KV_WIDTH), pl.BlockSpec((1, NSA_KV_WIDTH, nc), lambda i: (i, 0, 0))],
        out_shape=[jax.ShapeDtypeStruct((b, nc, NSA_KV_WIDTH), BF16),
                   jax.ShapeDtypeStruct((b, NSA_KV_WIDTH, nc), BF16)],
        compiler_params=_cparams(("parallel",)),
        name="nsa_compress",
    )(kc_in, vc_in, *wk, *wv, cos_e, sin_e)


def _lane_tiled(x):
    return jnp.concatenate([x] * NSA_REP, axis=1)


def _queries_t(qt, g, other, q_first):
    cols = []
    for r in range(NSA_REP):
        hq = g * NSA_REP + r
        piece = qt[hq * HEAD_DIM:(hq + 1) * HEAD_DIM]
        cols.append(jnp.concatenate([piece, other] if q_first else [other, piece], axis=0))
    return jnp.concatenate(cols, axis=1)


def _gated_heads_from_t(pieces, gate_ref, branch):
    sg = _sigmoid(jnp.transpose(gate_ref[0]))
    rows = [hq * N_BRANCH + branch for hq in range(NSA_HEADS)]
    gated = [p * sg[r:r + 1] for p, r in zip(pieces, rows)]
    slabs = [jnp.transpose(jnp.concatenate(gated[2 * j:2 * j + 2], axis=0)) for j in range(NSA_HEADS // 2)]
    return jnp.concatenate(slabs, axis=1)


def _normalised_heads(acc, tq):
    ot = acc[0:HEAD_DIM] * (1.0 / acc[HEAD_DIM:HEAD_DIM + 1])
    return [ot[:, r * tq:(r + 1) * tq] for r in range(NSA_REP)]


def _cmpattn_kernel(qt_ref, gate_ref, kc_ref, vct_ref, ovt_ref, o_ref, nsel_ref, *, tq, n_slc):
    i = pl.program_id(1)
    qt = qt_ref[0]
    kc = kc_ref[0]
    vct = vct_ref[0]
    nc = kc.shape[0]
    n = lax.broadcasted_iota(jnp.int32, (nc, tq), 0)
    t = i * tq + lax.broadcasted_iota(jnp.int32, (nc, tq), 1)
    bias = _lane_tiled(jnp.where(n * CMP_STRIDE + (CMP_BLOCK - 1) <= t, 0.0, NEG_INF))
    sees_any = _lane_tiled(jnp.where(i * tq + lax.broadcasted_iota(jnp.int32, (1, tq), 1) >= CMP_BLOCK - 1, 1.0, 0.0))
    zeros = jnp.zeros((HEAD_DIM, tq), BF16)
    blk = lax.broadcasted_iota(jnp.int32, (n_slc, tq), 0)
    tl = i * tq + lax.broadcasted_iota(jnp.int32, (n_slc, tq), 1)
    cur = tl // SLC_BLOCK
    forced = (blk == 0) | (blk == cur) | (blk == cur - 1)
    valid = blk * SLC_BLOCK <= tl
    ovt = ovt_ref[...]
    pieces = []
    scores = []
    for g in range(NSA_KV_HEADS):
        s = _dot(kc, _queries_t(qt, g, zeros, g == 0)) + bias
        e = jnp.exp2(s - jnp.max(s, axis=0, keepdims=True))
        p = e * (sees_any * (1.0 / jnp.sum(e, axis=0, keepdims=True)))
        ot = _dot(vct, p.astype(BF16))
        pieces.extend(ot[g * HEAD_DIM:(g + 1) * HEAD_DIM, r * tq:(r + 1) * tq] for r in range(NSA_REP))
        psum = p[:, 0:tq] + p[:, tq:2 * tq] + p[:, 2 * tq:3 * tq]
        hi = psum.astype(BF16)
        rest = psum - hi.astype(F32)
        mid = rest.astype(BF16)
        lo = (rest - mid.astype(F32)).astype(BF16)
        imp = _dot(ovt, hi) + _dot(ovt, mid) + _dot(ovt, lo)
        scores.append(jnp.where(valid, jnp.where(forced, FORCE_SCORE, imp), -1.0))
    o_ref[0] = _gated_heads_from_t(pieces, gate_ref, 0)

    n_live = ((i + 1) * tq) // SLC_BLOCK
    n_cls = n_slc // RANK_STEP
    cls = jnp.minimum((n_live - 1) // RANK_STEP, n_cls - 1)
    for c in range(n_cls):
        @pl.when(cls == c)
        def _(c=c):
            for g in range(NSA_KV_HEADS):
                nsel_ref[0, g] = _selection_bias(scores[g], (c + 1) * RANK_STEP, min(SLC_TOP, n_slc), tq)


def _selection_bias(score, nb, top, tq):
    sub = lax.broadcasted_iota(jnp.int32, (SUBLANES, tq), 0)
    ties = [jnp.where(sub > k, 1.0, 0.0) for k in range(SUBLANES)]
    groups = [score[SUBLANES * r:SUBLANES * (r + 1)] for r in range(nb // SUBLANES)]
    ranks = [jnp.zeros((SUBLANES, tq), F32) for _ in groups]
    for jp in range(nb):
        rowv = score[jp:jp + 1, :]
        for r, grp in enumerate(groups):
            if SUBLANES * r > jp:
                inc = jnp.where(rowv >= grp, 1.0, 0.0)
            elif SUBLANES * r + SUBLANES - 1 < jp:
                inc = jnp.where(rowv > grp, 1.0, 0.0)
            else:
                inc = jnp.where(rowv > grp, 1.0, jnp.where(rowv == grp, ties[jp - SUBLANES * r], 0.0))
            ranks[r] = ranks[r] + inc
    bias = jnp.where(jnp.concatenate(ranks, axis=0) < float(top), 0.0, NEG_INF)
    if nb < SEL_ROWS:
        bias = jnp.concatenate([bias, jnp.full((SEL_ROWS - nb, tq), NEG_INF, F32)], axis=0)
    return bias.astype(BF16)


def _cmpattn(qt, gate, kc, vct, ovt):
    b, _, t = qt.shape
    nc = kc.shape[1]
    tq = CMP_TQ
    n_slc = t // SLC_BLOCK
    assert n_slc <= SEL_ROWS and n_slc % RANK_STEP == 0
    blk = lambda w: pl.BlockSpec((1, tq, w), lambda i, j: (i, j, 0))
    return pl.pallas_call(
        functools.partial(_cmpattn_kernel, tq=tq, n_slc=n_slc),
        grid=(b, t // tq),
        in_specs=[pl.BlockSpec((1, NSA_WIDTH, tq), lambda i, j: (i, 0, j)), blk(LANES),
                  pl.BlockSpec((1, nc, NSA_KV_WIDTH), lambda i, j: (i, 0, 0)),
                  pl.BlockSpec((1, NSA_KV_WIDTH, nc), lambda i, j: (i, 0, 0)),
                  pl.BlockSpec(ovt.shape, lambda i, j: (0, 0))],
        out_specs=[blk(NSA_WIDTH),
                   pl.BlockSpec((1, NSA_KV_HEADS, SEL_ROWS, tq), lambda i, j: (i, 0, 0, j))],
        out_shape=[jax.ShapeDtypeStruct((b, t, NSA_WIDTH), F32),
                   jax.ShapeDtypeStruct((b, NSA_KV_HEADS, SEL_ROWS, t), BF16)],
        compiler_params=_cparams(("parallel", "parallel")),
        name="nsa_compressed_attn_select",
    )(qt, gate, kc, vct, ovt)


def _slc_kernel(qt_ref, nsel_ref, gate_ref, ksa_ref, vst_ref, o_ref, *, tq, kc):
    i = pl.program_id(1)
    qt = qt_ref[0]
    c_diag = (i * tq) // kc
    kpos = lax.broadcasted_iota(jnp.int32, (kc, tq), 0)
    tpos = i * tq + lax.broadcasted_iota(jnp.int32, (kc, tq), 1)
    qas = [_queries_t(qt, g, nsel_ref[0, g], True) for g in range(NSA_KV_HEADS)]

    def step(c, carry, causal):
        out = []
        for g, (m, acc) in enumerate(carry):
            k = ksa_ref[0, g, pl.ds(pl.multiple_of(c * kc, kc), kc), :]
            s = _dot(k, qas[g])
            if causal:
                s = s + _lane_tiled(jnp.where(c * kc + kpos <= tpos, 0.0, NEG_INF))
            m_new = jnp.maximum(m, jnp.max(s, axis=0, keepdims=True))
            p = jnp.exp2(s - m_new).astype(BF16)
            per = kc // SEL_CHUNK
            v = jnp.concatenate([vst_ref[0, g, c * per + u] for u in range(per)], axis=1)
            out.append((m_new, jnp.exp2(m - m_new) * acc + _dot(v, p)))
        return tuple(out)

    init = tuple((jnp.full((1, NSA_REP * tq), NEG_INF, F32), jnp.zeros((V_ROWS, NSA_REP * tq), F32))
                 for _ in range(NSA_KV_HEADS))
    carry = lax.fori_loop(0, c_diag, lambda c, carry: step(c, carry, False), init)
    carry = step(c_diag, carry, True)
    pieces = []
    for _, acc in carry:
        pieces.extend(_normalised_heads(acc, tq))
    o_ref[0] = _gated_heads_from_t(pieces, gate_ref, 1)


def _slc(qt, nsel, gate, ksa, vst):
    b, _, t = qt.shape
    tq = SEL_TQ
    kc = min(t, SEL_STEP)
    g = NSA_KV_HEADS
    return pl.pallas_call(
        functools.partial(_slc_kernel, tq=tq, kc=kc),
        grid=(b, t // tq),
        in_specs=[pl.BlockSpec((1, NSA_WIDTH, tq), lambda i, j: (i, 0, j)),
                  pl.BlockSpec((1, g, SEL_ROWS, tq), lambda i, j: (i, 0, 0, j)),
                  pl.BlockSpec((1, tq, LANES), lambda i, j: (i, j, 0)),
                  pl.BlockSpec((1, g, t, LANES), lambda i, j: (i, 0, 0, 0)),
                  pl.BlockSpec((1, g, t // SEL_CHUNK, V_ROWS, SEL_CHUNK), lambda i, j: (i, 0, 0, 0, 0))],
        out_specs=pl.BlockSpec((1, tq, NSA_WIDTH), lambda i, j: (i, j, 0)),
        out_shape=jax.ShapeDtypeStruct((b, t, NSA_WIDTH), F32),
        compiler_params=_cparams(("parallel", "parallel")),
        name="nsa_selected_attn",
    )(qt, nsel, gate, ksa, vst)


def _win_kernel(qt_ref, gate_ref, kw_ref, vwt_ref, o_ref, *, tq, nblk, t_total):
    i = pl.program_id(1)
    qt = qt_ref[0]
    kb = ATT_TQ
    span = nblk * kb
    first_blk = jnp.clip(i * (tq // kb) - WINDOW // kb, 0, t_total // kb - nblk)
    start = pl.multiple_of(first_blk * kb, kb)
    k = kw_ref[0, pl.ds(start, span), :]
    d = (i * tq + lax.broadcasted_iota(jnp.int32, (span, tq), 1)
         - (start + lax.broadcasted_iota(jnp.int32, (span, tq), 0)))
    bias = _lane_tiled(jnp.where((d >= 0) & (d < WINDOW), 0.0, NEG_INF))
    zeros = jnp.zeros((HEAD_DIM, tq), BF16)
    pieces = []
    for g in range(NSA_KV_HEADS):
        s = _dot(k, _queries_t(qt, g, zeros, g == 0)) + bias
        p = jnp.exp2(s - jnp.max(s, axis=0, keepdims=True)).astype(BF16)
        vt = jnp.concatenate([vwt_ref[0, g, first_blk + u] for u in range(nblk)], axis=1)
        pieces.extend(_normalised_heads(_dot(vt, p), tq))
    o_ref[0] = _gated_heads_from_t(pieces, gate_ref, 2)


def _win(qt, gate, kw, vwt):
    b, _, t = qt.shape
    tq = min(t, WIN_TQ)
    kb = ATT_TQ
    nblk = min(t // kb, (WINDOW + tq) // kb)
    g = NSA_KV_HEADS
    return pl.pallas_call(
        functools.partial(_win_kernel, tq=tq, nblk=nblk, t_total=t),
        grid=(b, t // tq),
        in_specs=[pl.BlockSpec((1, NSA_WIDTH, tq), lambda i, j: (i, 0, j)),
                  pl.BlockSpec((1, tq, LANES), lambda i, j: (i, j, 0)),
                  pl.BlockSpec((1, t, NSA_KV_WIDTH), lambda i, j: (i, 0, 0)),
                  pl.BlockSpec((1, g, t // kb, V_ROWS, kb), lambda i, j: (i, 0, 0, 0, 0))],
        out_specs=pl.BlockSpec((1, tq, NSA_WIDTH), lambda i, j: (i, j, 0)),
        out_shape=jax.ShapeDtypeStruct((b, t, NSA_WIDTH), F32),
        compiler_params=_cparams(("parallel", "parallel")),
        name="nsa_window_attn",
    )(qt, gate, kw, vwt)


N_MIX_IN = 6


def _post_kernel(*refs, alpha, fc):
    tiles, halos = refs[0:N_MIX_IN], refs[N_MIX_IN:2 * N_MIX_IN]
    (wo_ref, g1_ref, b1_ref, wg_ref, wu_ref, cw_ref, cb_ref, wd_ref, g2_ref, b2_ref,
     o_ref, x1_ref, xb_ref, act_ref) = refs[2 * N_MIX_IN:]
    j = pl.program_id(1)
    h = CONV_HALO

    def mixed(x_ref, ya_ref, yb_ref, oc_ref, os_ref, ow_ref):
        yc = oc_ref[0] + os_ref[0] + ow_ref[0]
        y = jnp.concatenate([ya_ref[0], yb_ref[0], yc], axis=1).astype(BF16)
        return _layernorm(alpha * x_ref[0] + _dot(y, wo_ref[...]), g1_ref[...], b1_ref[...])

    x1_ref[...] = mixed(*tiles)
    xb_ref[0:h, :] = jnp.where(j > 0, mixed(*halos), 0.0).astype(BF16)
    xb_ref[h:, :] = x1_ref[...].astype(BF16)
    for c in range(0, wg_ref.shape[1], fc):
        hg = _dot(xb_ref[...], wg_ref[:, c:c + fc])
        up = _dot(xb_ref[h:, :], wu_ref[:, c:c + fc])
        hc = (cb_ref[:, c:c + fc] + pltpu.roll(hg, 2, 0)[h:] * cw_ref[0:1, c:c + fc]
              + pltpu.roll(hg, 1, 0)[h:] * cw_ref[1:2, c:c + fc] + hg[h:] * cw_ref[2:3, c:c + fc])
        act_ref[:, c:c + fc] = (_gelu_tanh(hc) * up).astype(BF16)
    o_ref[0] = _layernorm(alpha * x1_ref[...] + _dot(act_ref[...], wd_ref[...]), g2_ref[...], b2_ref[...])


def _post(mix_in, wo, g1, b1, wg, wu, cw, cb, wd, g2, b2, alpha):
    b, t, d = mix_in[0].shape
    dff = wg.shape[1]
    tm = min(t, 512)
    fc = 2 * LANES
    r = tm // CONV_HALO
    const = lambda a: pl.BlockSpec(a.shape, lambda i, j: (0, 0), pipeline_mode=pl.Buffered(1))
    tile = lambda a: pl.BlockSpec((1, tm, a.shape[2]), lambda i, j: (i, j, 0))
    halo = lambda a: pl.BlockSpec((1, CONV_HALO, a.shape[2]), lambda i, j: (i, jnp.maximum(j * r - 1, 0), 0))
    weights = (wo, g1, b1, wg, wu, cw, cb, wd, g2, b2)
    assert len(mix_in) == N_MIX_IN
    return pl.pallas_call(
        functools.partial(_post_kernel, alpha=alpha, fc=fc),
        grid=(b, t // tm),
        in_specs=[tile(a) for a in mix_in] + [halo(a) for a in mix_in] + [const(a) for a in weights],
        out_specs=pl.BlockSpec((1, tm, d), lambda i, j: (i, j, 0)),
        out_shape=jax.ShapeDtypeStruct((b, t, d), F32),
        scratch_shapes=[pltpu.VMEM((tm, d), F32), pltpu.VMEM((CONV_HALO + tm, d), BF16),
                        pltpu.VMEM((tm, dff), BF16)],
        compiler_params=_cparams(("parallel", "parallel")),
        name="out_proj_ffn",
    )(*mix_in, *mix_in, *weights)


def _overlap_t(nc, n_slc):
    ci = np.arange(nc)[None, :]
    sj = np.arange(n_slc)[:, None]
    ov = np.clip(np.minimum(ci * CMP_STRIDE + CMP_BLOCK, (sj + 1) * SLC_BLOCK)
                 - np.maximum(ci * CMP_STRIDE, sj * SLC_BLOCK), 0, None).astype(np.float32) / CMP_STRIDE
    ov[:, nc - 1] = 0.0
    return jnp.asarray(ov, BF16)


def kernel(x, positions, w_in, w_out, pool_w, pool_scale, ret_gn_g, cmp_pos_k, cmp_w1_k, cmp_b1_k, cmp_w2_k,
           cmp_pos_v, cmp_w1_v, cmp_b1_v, cmp_w2_v, ffn_w_gate, ffn_w_up, ffn_conv_w, ffn_conv_b, ffn_w_down,
           ln1_g, ln1_b, ln2_g, ln2_b):
    b, t, d = x.shape
    depth = w_in.shape[0]
    alpha = float((2 * depth) ** 0.25)
    nc = t // CMP_STRIDE
    n_slc = t // SLC_BLOCK

    inv = ROPE_THETA ** (-jnp.arange(0, HEAD_DIM, 2, dtype=F32) / HEAD_DIM)
    inv = jnp.tile(inv, LANES // (HEAD_DIM // 2))[None, :]
    cos, sin = _rope_tables(positions, inv)
    ends = jnp.minimum(jnp.arange(nc) * CMP_STRIDE + CMP_BLOCK - 1, t - 1)
    cos_e, sin_e = _rope_tables(positions[:, ends], inv)
    ovt = _overlap_t(nc, n_slc)
    eye_g = jnp.eye(len(POOL_WINDOWS), dtype=F32)

    for l in range(depth):
        w_in_p = jnp.pad(w_in[l], ((0, 0), (0, IN_PAD - IN_WIDTH))).astype(BF16)
        wbd = jnp.einsum('gcd,ge->gced', pool_w[l], eye_g).reshape(POOL_WIDTH, POOL_WIDTH).astype(BF16)
        y_a, h_ret, qt, k_cmp, v_cmp, gate3, ksa, vst, kw, vwt = _inproj(
            x, cos, sin, w_in_p, wbd, pool_scale[l][None, :])

        y_b = _retention(h_ret, ret_gn_g[l][None, :])

        wk = _compress_weights(cmp_pos_k[l], cmp_w1_k[l], cmp_b1_k[l], cmp_w2_k[l])
        wv = _compress_weights(cmp_pos_v[l], cmp_w1_v[l], cmp_b1_v[l], cmp_w2_v[l])
        kc, vct = _compress(k_cmp, v_cmp, wk, wv, cos_e, sin_e)
        o_cmp, nsel = _cmpattn(qt, gate3, kc, vct, ovt)
        o_slc = _slc(qt, nsel, gate3, ksa, vst)
        o_win = _win(qt, gate3, kw, vwt)

        x = _post((x, y_a, y_b, o_cmp, o_slc, o_win), w_out[l].astype(BF16), ln1_g[l][None, :], ln1_b[l][None, :],
                  ffn_w_gate[l].astype(BF16), ffn_w_up[l].astype(BF16), ffn_conv_w[l], ffn_conv_b[l][None, :],
                  ffn_w_down[l].astype(BF16), ln2_g[l][None, :], ln2_b[l][None, :], alpha)
    return x
```

```python
import functools
import math

import jax
import jax.numpy as jnp
import numpy as np
from jax import lax
from jax.experimental import pallas as pl
from jax.experimental.pallas import tpu as pltpu

F32 = jnp.float32
BF16 = jnp.bfloat16

HEAD_DIM = 64
POOL_WINDOWS = (2, 4, 8, 16)
POOL_GROUP_WIDTH = 64
POOL_WIDTH = POOL_GROUP_WIDTH * len(POOL_WINDOWS)
POOL_HALO = 16
RET_HEADS = 6
RET_WIDTH = RET_HEADS * HEAD_DIM
RET_CHUNK = 256
NSA_HEADS = 6
NSA_WIDTH = NSA_HEADS * HEAD_DIM
NSA_KV_HEADS = 2
NSA_REP = NSA_HEADS // NSA_KV_HEADS
NSA_KV_WIDTH = NSA_KV_HEADS * HEAD_DIM
CMP_BLOCK = 32
CMP_STRIDE = 16
CMP_HIDDEN = 128
SLC_BLOCK = 64
SLC_TOP = 16
WINDOW = 512
N_BRANCH = 3
FORCE_SCORE = 1e6
CONV_WIDTH = 3
CONV_HALO = 16
ROPE_THETA = 10000.0
LN_EPS = 1e-5
GN_EPS = 1e-5
NEG_INF = -1e30
Q_SCALE = HEAD_DIM ** -0.5 * math.log2(math.e)

LANES = 128
SUBLANES = 8
VMEM_LIMIT = 56 * 1024 * 1024

ATT_TQ = 128
SEL_TQ = 512
SEL_CHUNK = 512
SEL_STEP = 512
SEL_ROWS = LANES - HEAD_DIM
V_ROWS = HEAD_DIM + 16
RANK_STEP = 16
WIN_TQ = 256
RET_STEP_CHUNKS = 4
CMP_TQ = 256

OFF_POOL = 0
OFF_RET = OFF_POOL + POOL_WIDTH
OFF_QNSA = OFF_RET + 4 * RET_WIDTH
OFF_KCMP = OFF_QNSA + NSA_WIDTH
OFF_VCMP = OFF_KCMP + NSA_KV_WIDTH
OFF_KV4 = OFF_VCMP + NSA_KV_WIDTH
OFF_GATE = OFF_KV4 + 4 * NSA_KV_WIDTH
IN_WIDTH = OFF_GATE + NSA_HEADS * N_BRANCH
IN_PAD = OFF_GATE + LANES


def _cparams(sem, flags=None):
    return pltpu.CompilerParams(dimension_semantics=sem, vmem_limit_bytes=VMEM_LIMIT, flags=flags)


def _rope(x, cos, sin_signed):
    lane = lax.broadcasted_iota(jnp.int32, x.shape, 1)
    first = (lane & (HEAD_DIM - 1)) < HEAD_DIM // 2
    partner = jnp.where(first, pltpu.roll(x, LANES - HEAD_DIM // 2, 1), pltpu.roll(x, HEAD_DIM // 2, 1))
    return x * cos + partner * sin_signed


def _gelu_tanh(x):
    return 0.5 * x * (1.0 + jnp.tanh(math.sqrt(2.0 / math.pi) * (x + 0.044715 * (x * x * x))))


def _sigmoid(x):
    return 1.0 / (1.0 + jnp.exp(-x))


def _layernorm(y, g, b):
    mu = jnp.mean(y, axis=-1, keepdims=True)
    d = y - mu
    var = jnp.mean(d * d, axis=-1, keepdims=True)
    return d * lax.rsqrt(var + LN_EPS) * g + b


def _dot(a, b):
    return jnp.dot(a, b, preferred_element_type=F32)


def _dot_nt(a, b):
    return lax.dot_general(a, b, (((1,), (1,)), ((), ())), preferred_element_type=F32)


def _rope_table_kernel(pos_ref, inv_ref, cos_ref, sin_ref):
    lane = lax.broadcasted_iota(jnp.int32, (LANES, LANES), 1)
    first = (lane & (HEAD_DIM - 1)) < HEAD_DIM // 2
    inv = inv_ref[...]
    copies = LANES // inv.shape[0]
    for r in range(pos_ref.shape[1]):
        ang = pos_ref[0, r:r + 1, :].astype(F32) * inv
        c = jnp.transpose(jnp.concatenate([jnp.cos(ang)] * copies, axis=0))
        s = jnp.transpose(jnp.concatenate([jnp.sin(ang)] * copies, axis=0))
        cos_ref[0, r * LANES:(r + 1) * LANES, :] = c
        sin_ref[0, r * LANES:(r + 1) * LANES, :] = jnp.where(first, -s, s)


def _rope_tables(pos, inv):
    b, n = pos.shape
    shp = jax.ShapeDtypeStruct((b, n, LANES), F32)
    return pl.pallas_call(
        _rope_table_kernel,
        grid=(b,),
        in_specs=[pl.BlockSpec((1, n // LANES, LANES), lambda i: (i, 0, 0)),
                  pl.BlockSpec(inv.shape, lambda i: (0, 0))],
        out_specs=[pl.BlockSpec((1, n, LANES), lambda i: (i, 0, 0))] * 2,
        out_shape=[shp, shp],
        compiler_params=_cparams(("parallel",)),
        name="rope_tables",
    )(pos.reshape(b, n // LANES, LANES), inv)


def _pool_mix(cur, prev, t0, w, scale):
    cat = jnp.concatenate([prev, cur], axis=0)
    s2 = cat + pltpu.roll(cat, 1, 0)
    s4 = s2 + pltpu.roll(s2, 2, 0)
    s8 = s4 + pltpu.roll(s4, 4, 0)
    s16 = s8 + pltpu.roll(s8, 8, 0)
    grp = lax.broadcasted_iota(jnp.int32, cur.shape, 1) // POOL_GROUP_WIDTH
    t = t0 + lax.broadcasted_iota(jnp.int32, cur.shape, 0)

    def pick(a, b, c, d):
        return jnp.where(grp == 0, a, jnp.where(grp == 1, b, jnp.where(grp == 2, c, d)))

    h = POOL_HALO
    wsum = pick(s2[h:], s4[h:], s8[h:], s16[h:])
    width = pick(*[float(wd) for wd in POOL_WINDOWS])
    cnt = jnp.minimum((t + 1).astype(F32), width)
    mixed = wsum / cnt - cur
    return _dot(mixed.astype(BF16), w) * scale


def _inproj_kernel(x_ref, xh_ref, cos_ref, sin_ref, w_ref, wpool_ref, pscale_ref,
                   ya_ref, ret_ref, qt_ref, kcmp_ref, vcmp_ref, gate_ref, ksa_ref, vst_ref, kw_ref, vwt_ref,
                   stage_ref, *, tm):
    j = pl.program_id(1)
    cos = cos_ref[0]
    sin = sin_ref[0]
    xb = x_ref[0].astype(BF16)
    lane = lax.broadcasted_iota(jnp.int32, (tm, LANES), 1)
    lo = lane < HEAD_DIM
    tpos = j * tm + lax.broadcasted_iota(jnp.int32, (tm, LANES), 0)
    onehot = jnp.where(lane - HEAD_DIM == tpos // SLC_BLOCK, 1.0, 0.0)
    top = lax.broadcasted_iota(jnp.int32, (LANES, tm), 0) < HEAD_DIM

    def store_t(v, ref, chunk):
        vt = jnp.transpose(v)
        per_head = (jnp.where(top, vt, 1.0), jnp.where(top, pltpu.roll(vt, HEAD_DIM, 0), 1.0))
        for g, vg in enumerate(per_head):
            for u in range(tm // chunk):
                ref[0, g, u] = vg[0:V_ROWS, u * chunk:(u + 1) * chunk].astype(BF16)

    def emit(col, y):
        if col < OFF_RET:
            return
        if col < OFF_QNSA:
            part, k = divmod(col - OFF_RET, RET_WIDTH)
            if part == 0:
                y = _rope(y, cos, sin)
            elif part == 1:
                y = _rope(y, cos, sin) * (HEAD_DIM ** -0.5)
            ret_ref[0, :, col - OFF_RET:col - OFF_RET + LANES] = y
        elif col < OFF_KCMP:
            k = col - OFF_QNSA
            qt_ref[0, k:k + LANES, :] = jnp.transpose(_rope(y, cos, sin) * Q_SCALE).astype(BF16)
        elif col in (OFF_KCMP, OFF_VCMP):
            out_ref = kcmp_ref if col == OFF_KCMP else vcmp_ref
            stage_ref[...] = y
            for l in range(CMP_STRIDE):
                out_ref[0, :, l * LANES:(l + 1) * LANES] = stage_ref[pl.ds(l, tm // CMP_STRIDE, stride=CMP_STRIDE), :]
        elif col == OFF_KV4:
            ks = _rope(y, cos, sin)
            ksa_ref[0, 0] = jnp.where(lo, ks, onehot).astype(BF16)
            ksa_ref[0, 1] = jnp.where(lo, pltpu.roll(ks, HEAD_DIM, 1), onehot).astype(BF16)
        elif col == OFF_KV4 + NSA_KV_WIDTH:
            store_t(y, vst_ref, SEL_CHUNK)
        elif col == OFF_KV4 + 2 * NSA_KV_WIDTH:
            kw_ref[0] = _rope(y, cos, sin).astype(BF16)
        elif col == OFF_KV4 + 3 * NSA_KV_WIDTH:
            store_t(y, vwt_ref, ATT_TQ)
        else:
            gate_ref[0] = y

    for c in range(0, IN_PAD, 2 * LANES):
        y = _dot(xb, w_ref[:, c:c + 2 * LANES])
        if c == OFF_POOL:
            prev = jnp.where(j > 0, _dot(xh_ref[0].astype(BF16), w_ref[:, 0:POOL_WIDTH]), 0.0)
            ya_ref[0] = _pool_mix(y, prev, j * tm, wpool_ref[...], pscale_ref[...])
        for half in range(2):
            emit(c + half * LANES, y[:, half * LANES:(half + 1) * LANES])


def _inproj(x, cos, sin, w, wpool, pscale):
    b, t, d = x.shape
    tm = min(t, 512)
    g = NSA_KV_HEADS
    r = tm // POOL_HALO
    row = lambda wd: pl.BlockSpec((1, tm, wd), lambda i, j: (i, j, 0))
    const = lambda a: pl.BlockSpec(a.shape, lambda i, j: (0, 0), pipeline_mode=pl.Buffered(1))
    strided = pl.BlockSpec((1, tm // CMP_STRIDE, CMP_STRIDE * NSA_KV_WIDTH), lambda i, j: (i, j, 0))
    outs = [
        (row(POOL_WIDTH), (b, t, POOL_WIDTH), F32),
        (row(4 * RET_WIDTH), (b, t, 4 * RET_WIDTH), F32),
        (pl.BlockSpec((1, NSA_WIDTH, tm), lambda i, j: (i, 0, j)), (b, NSA_WIDTH, t), BF16),
        (strided, (b, t // CMP_STRIDE, CMP_STRIDE * NSA_KV_WIDTH), F32),
        (strided, (b, t // CMP_STRIDE, CMP_STRIDE * NSA_KV_WIDTH), F32),
        (row(LANES), (b, t, LANES), F32),
        (pl.BlockSpec((1, g, tm, LANES), lambda i, j: (i, 0, j, 0)), (b, g, t, LANES), BF16),
        (pl.BlockSpec((1, g, tm // SEL_CHUNK, V_ROWS, SEL_CHUNK), lambda i, j: (i, 0, j, 0, 0)),
         (b, g, t // SEL_CHUNK, V_ROWS, SEL_CHUNK), BF16),
        (row(NSA_KV_WIDTH), (b, t, NSA_KV_WIDTH), BF16),
        (pl.BlockSpec((1, g, tm // ATT_TQ, V_ROWS, ATT_TQ), lambda i, j: (i, 0, j, 0, 0)),
         (b, g, t // ATT_TQ, V_ROWS, ATT_TQ), BF16),
    ]
    return pl.pallas_call(
        functools.partial(_inproj_kernel, tm=tm),
        grid=(b, t // tm),
        in_specs=[row(d), pl.BlockSpec((1, POOL_HALO, d), lambda i, j: (i, jnp.maximum(j * r - 1, 0), 0)),
                  row(LANES), row(LANES), const(w), const(wpool), const(pscale)],
        out_specs=[o[0] for o in outs],
        out_shape=[jax.ShapeDtypeStruct(o[1], o[2]) for o in outs],
        scratch_shapes=[pltpu.VMEM((tm, NSA_KV_WIDTH), F32)],
        compiler_params=_cparams(("parallel", "parallel")),
        name="in_proj",
    )(x, x, cos, sin, w, wpool, pscale)


def _ret_consts():
    h, c = RET_HEADS, RET_CHUNK
    lg = np.log1p(-np.power(2.0, -5.0 - np.arange(h, dtype=np.float64)))
    i = np.arange(c, dtype=np.float64)
    diff = i[:, None] - i[None, :]
    dmask = np.where(diff >= 0, np.exp(lg[:, None, None] * np.maximum(diff, 0.0)), 0.0)
    xi = np.repeat(np.exp(lg[:, None] * (i + 1.0)).T, HEAD_DIM, axis=1)
    zeta = np.repeat(np.exp(lg[:, None] * (c - 1.0 - i)).T, HEAD_DIM, axis=1)
    gc = np.repeat(np.exp(lg * c), HEAD_DIM)[None, :]
    return (jnp.asarray(dmask, F32), jnp.asarray(xi, F32), jnp.asarray(zeta, F32), jnp.asarray(gc, F32))


def _ret_kernel(h_ref, dmask_ref, xi_ref, zeta_ref, gc_ref, gn_ref, o_ref, state_ref):
    @pl.when(pl.program_id(1) == 0)
    def _():
        state_ref[...] = jnp.zeros_like(state_ref)

    c = RET_CHUNK
    lane = lax.broadcasted_iota(jnp.int32, (c, LANES), 1)
    row = lax.broadcasted_iota(jnp.int32, (LANES, LANES), 0)
    col = lax.broadcasted_iota(jnp.int32, (LANES, LANES), 1)
    lo = lane < HEAD_DIM
    same_head = (row < HEAD_DIM) == (col < HEAD_DIM)
    states = [state_ref[j] for j in range(RET_WIDTH // LANES)]
    for u in range(h_ref.shape[1] // c):
        rs = slice(u * c, (u + 1) * c)
        for j in range(RET_WIDTH // LANES):
            sl = slice(j * LANES, (j + 1) * LANES)
            q = h_ref[0, rs, j * LANES:(j + 1) * LANES]
            k = h_ref[0, rs, RET_WIDTH + j * LANES:RET_WIDTH + (j + 1) * LANES]
            v = h_ref[0, rs, 2 * RET_WIDTH + j * LANES:2 * RET_WIDTH + (j + 1) * LANES]
            g = h_ref[0, rs, 3 * RET_WIDTH + j * LANES:3 * RET_WIDTH + (j + 1) * LANES]
            kb = k.astype(BF16)
            vb = v.astype(BF16)
            o = _dot(q.astype(BF16), states[j].astype(BF16)) * xi_ref[:, sl]
            for hh in range(2):
                m = lo if hh == 0 else jnp.logical_not(lo)
                qm = jnp.where(m, q, 0.0).astype(BF16)
                s = _dot_nt(qm, kb) * dmask_ref[2 * j + hh]
                o = o + jnp.where(m, _dot(s.astype(BF16), vb), 0.0)
            kz = (k * zeta_ref[:, sl]).astype(BF16)
            kv = lax.dot_general(kz, vb, (((0,), (0,)), ((), ())), preferred_element_type=F32)
            states[j] = gc_ref[:, sl] * states[j] + jnp.where(same_head, kv, 0.0)
            s_lo = jnp.sum(jnp.where(lo, o, 0.0), axis=-1, keepdims=True)
            s_hi = jnp.sum(jnp.where(lo, 0.0, o), axis=-1, keepdims=True)
            d = o - jnp.where(lo, s_lo, s_hi) * (1.0 / HEAD_DIM)
            d2 = d * d
            v_lo = jnp.sum(jnp.where(lo, d2, 0.0), axis=-1, keepdims=True)
            v_hi = jnp.sum(jnp.where(lo, 0.0, d2), axis=-1, keepdims=True)
            var = jnp.where(lo, v_lo, v_hi) * (1.0 / HEAD_DIM)
            on = d * lax.rsqrt(var + GN_EPS) * gn_ref[:, sl]
            o_ref[0, rs, sl] = g * _sigmoid(g) * on
    for j, st in enumerate(states):
        state_ref[j] = st


def _retention(hret, gn_g):
    b, t, _ = hret.shape
    c = RET_CHUNK
    rows = min(t, RET_STEP_CHUNKS * c)
    dmask, xi, zeta, gc = _ret_consts()
    const = lambda shape: pl.BlockSpec(shape, lambda i, j: (0,) * len(shape))
    return pl.pallas_call(
        _ret_kernel,
        grid=(b, t // rows),
        in_specs=[pl.BlockSpec((1, rows, 4 * RET_WIDTH), lambda i, j: (i, j, 0)),
                  const((RET_HEADS, c, c)), const((c, RET_WIDTH)), const((c, RET_WIDTH)),
                  const((1, RET_WIDTH)), const((1, RET_WIDTH))],
        out_specs=pl.BlockSpec((1, rows, RET_WIDTH), lambda i, j: (i, j, 0)),
        out_shape=jax.ShapeDtypeStruct((b, t, RET_WIDTH), F32),
        scratch_shapes=[pltpu.VMEM((RET_WIDTH // LANES, LANES, LANES), F32)],
        compiler_params=_cparams(("parallel", "arbitrary")),
        name="retention",
    )(hret, dmask, xi, zeta, gc, gn_g)


def _compress_kernel(k_ref, v_ref, ptk, pbk, wtk, wbk, b1k, w2k, ptv, pbv, wtv, wbv, b1v, w2v,
                     cos_ref, sin_ref, kc_ref, vct_ref):
    def comp(x_ref, pt, pb, wt, wb, b1, w2):
        x = x_ref[0]
        top = _dot((x + pt[...]).astype(BF16), wt[...])
        bot = _dot((x + pb[...]).astype(BF16), wb[...])
        rows = bot.shape[0]
        pre = top + pltpu.roll(bot, rows - 1, 0) + b1[...]
        return _dot(_gelu_tanh(pre).astype(BF16), w2[...])

    kc = comp(k_ref, ptk, pbk, wtk, wbk, b1k, w2k)
    kc_ref[0] = _rope(kc, cos_ref[0], sin_ref[0]).astype(BF16)
    vct_ref[0] = jnp.transpose(comp(v_ref, ptv, pbv, wtv, wbv, b1v, w2v)).astype(BF16)


def _compress_weights(pos, w1, b1, w2):
    half = CMP_BLOCK // 2
    g = NSA_KV_HEADS
    w1r = w1.reshape(CMP_BLOCK, HEAD_DIM, CMP_HIDDEN)
    eye = jnp.eye(g, dtype=F32)

    def lay(wh):
        return jnp.einsum('ldh,ge->lgdeh', wh, eye).reshape(half * g * HEAD_DIM, g * CMP_HIDDEN)

    def tile_pos(p):
        return jnp.broadcast_to(p[:, None, :], (half, g, HEAD_DIM)).reshape(1, half * g * HEAD_DIM)

    w2bd = jnp.einsum('hd,ge->ghed', w2, eye).reshape(g * CMP_HIDDEN, g * HEAD_DIM)
    return (tile_pos(pos[:half]), tile_pos(pos[half:]), lay(w1r[:half]).astype(BF16), lay(w1r[half:]).astype(BF16),
            jnp.tile(b1, g)[None, :], w2bd.astype(BF16))


def _compress(kc_in, vc_in, wk, wv, cos_e, sin_e):
    b, nc, width = kc_in.shape
    const = lambda a: pl.BlockSpec(a.shape, lambda i: (0,) * a.ndim)
    blk = lambda w: pl.BlockSpec((1, nc, w), lambda i: (i, 0, 0))
    return pl.pallas_call(
        _compress_kernel,
        grid=(b,),
        in_specs=[blk(width), blk(width)] + [const(a) for a in wk] + [const(a) for a in wv]
                 + [blk(LANES), blk(LANES)],
        out_specs=[blk(NSA_KV_WIDTH), pl.BlockSpec((1, NSA_KV_WIDTH, nc), lambda i: (i, 0, 0))],
        out_shape=[jax.ShapeDtypeStruct((b, nc, NSA_KV_WIDTH), BF16),
                   jax.ShapeDtypeStruct((b, NSA_KV_WIDTH, nc), BF16)],
        compiler_params=_cparams(("parallel",)),
        name="nsa_compress",
    )(kc_in, vc_in, *wk, *wv, cos_e, sin_e)


def _lane_tiled(x):
    return jnp.concatenate([x] * NSA_REP, axis=1)


def _queries_t(qt, g, other, q_first):
    cols = []
    for r in range(NSA_REP):
        hq = g * NSA_REP + r
        piece = qt[hq * HEAD_DIM:(hq + 1) * HEAD_DIM]
        cols.append(jnp.concatenate([piece, other] if q_first else [other, piece], axis=0))
    return jnp.concatenate(cols, axis=1)


def _gated_heads_from_t(pieces, gate_ref, branch):
    sg = _sigmoid(jnp.transpose(gate_ref[0]))
    rows = [hq * N_BRANCH + branch for hq in range(NSA_HEADS)]
    gated = [p * sg[r:r + 1] for p, r in zip(pieces, rows)]
    slabs = [jnp.transpose(jnp.concatenate(gated[2 * j:2 * j + 2], axis=0)) for j in range(NSA_HEADS // 2)]
    return jnp.concatenate(slabs, axis=1)


def _normalised_heads(acc, tq):
    ot = acc[0:HEAD_DIM] * (1.0 / acc[HEAD_DIM:HEAD_DIM + 1])
    return [ot[:, r * tq:(r + 1) * tq] for r in range(NSA_REP)]


def _cmpattn_kernel(qt_ref, gate_ref, kc_ref, vct_ref, ovt_ref, o_ref, nsel_ref, *, tq, n_slc):
    i = pl.program_id(1)
    qt = qt_ref[0]
    kc = kc_ref[0]
    vct = vct_ref[0]
    nc = kc.shape[0]
    n = lax.broadcasted_iota(jnp.int32, (nc, tq), 0)
    t = i * tq + lax.broadcasted_iota(jnp.int32, (nc, tq), 1)
    bias = _lane_tiled(jnp.where(n * CMP_STRIDE + (CMP_BLOCK - 1) <= t, 0.0, NEG_INF))
    sees_any = _lane_tiled(jnp.where(i * tq + lax.broadcasted_iota(jnp.int32, (1, tq), 1) >= CMP_BLOCK - 1, 1.0, 0.0))
    zeros = jnp.zeros((HEAD_DIM, tq), BF16)
    blk = lax.broadcasted_iota(jnp.int32, (n_slc, tq), 0)
    tl = i * tq + lax.broadcasted_iota(jnp.int32, (n_slc, tq), 1)
    cur = tl // SLC_BLOCK
    forced = (blk == 0) | (blk == cur) | (blk == cur - 1)
    valid = blk * SLC_BLOCK <= tl
    ovt = ovt_ref[...]
    pieces = []
    scores = []
    for g in range(NSA_KV_HEADS):
        s = _dot(kc, _queries_t(qt, g, zeros, g == 0)) + bias
        e = jnp.exp2(s - jnp.max(s, axis=0, keepdims=True))
        p = e * (sees_any * (1.0 / jnp.sum(e, axis=0, keepdims=True)))
        ot = _dot(vct, p.astype(BF16))
        pieces.extend(ot[g * HEAD_DIM:(g + 1) * HEAD_DIM, r * tq:(r + 1) * tq] for r in range(NSA_REP))
        psum = p[:, 0:tq] + p[:, tq:2 * tq] + p[:, 2 * tq:3 * tq]
        hi = psum.astype(BF16)
        rest = psum - hi.astype(F32)
        mid = rest.astype(BF16)
        lo = (rest - mid.astype(F32)).astype(BF16)
        imp = _dot(ovt, hi) + _dot(ovt, mid) + _dot(ovt, lo)
        scores.append(jnp.where(valid, jnp.where(forced, FORCE_SCORE, imp), -1.0))
    o_ref[0] = _gated_heads_from_t(pieces, gate_ref, 0)

    n_live = ((i + 1) * tq) // SLC_BLOCK
    n_cls = n_slc // RANK_STEP
    cls = jnp.minimum((n_live - 1) // RANK_STEP, n_cls - 1)
    for c in range(n_cls):
        @pl.when(cls == c)
        def _(c=c):
            for g in range(NSA_KV_HEADS):
                nsel_ref[0, g] = _selection_bias(scores[g], (c + 1) * RANK_STEP, min(SLC_TOP, n_slc), tq)


def _selection_bias(score, nb, top, tq):
    sub = lax.broadcasted_iota(jnp.int32, (SUBLANES, tq), 0)
    ties = [jnp.where(sub > k, 1.0, 0.0) for k in range(SUBLANES)]
    groups = [score[SUBLANES * r:SUBLANES * (r + 1)] for r in range(nb // SUBLANES)]
    ranks = [jnp.zeros((SUBLANES, tq), F32) for _ in groups]
    for jp in range(nb):
        rowv = score[jp:jp + 1, :]
        for r, grp in enumerate(groups):
            if SUBLANES * r > jp:
                inc = jnp.where(rowv >= grp, 1.0, 0.0)
            elif SUBLANES * r + SUBLANES - 1 < jp:
                inc = jnp.where(rowv > grp, 1.0, 0.0)
            else:
                inc = jnp.where(rowv > grp, 1.0, jnp.where(rowv == grp, ties[jp - SUBLANES * r], 0.0))
            ranks[r] = ranks[r] + inc
    bias = jnp.where(jnp.concatenate(ranks, axis=0) < float(top), 0.0, NEG_INF)
    if nb < SEL_ROWS:
        bias = jnp.concatenate([bias, jnp.full((SEL_ROWS - nb, tq), NEG_INF, F32)], axis=0)
    return bias.astype(BF16)


def _cmpattn(qt, gate, kc, vct, ovt):
    b, _, t = qt.shape
    nc = kc.shape[1]
    tq = CMP_TQ
    n_slc = t // SLC_BLOCK
    assert n_slc <= SEL_ROWS and n_slc % RANK_STEP == 0
    blk = lambda w: pl.BlockSpec((1, tq, w), lambda i, j: (i, j, 0))
    return pl.pallas_call(
        functools.partial(_cmpattn_kernel, tq=tq, n_slc=n_slc),
        grid=(b, t // tq),
        in_specs=[pl.BlockSpec((1, NSA_WIDTH, tq), lambda i, j: (i, 0, j)), blk(LANES),
                  pl.BlockSpec((1, nc, NSA_KV_WIDTH), lambda i, j: (i, 0, 0)),
                  pl.BlockSpec((1, NSA_KV_WIDTH, nc), lambda i, j: (i, 0, 0)),
                  pl.BlockSpec(ovt.shape, lambda i, j: (0, 0))],
        out_specs=[blk(NSA_WIDTH),
                   pl.BlockSpec((1, NSA_KV_HEADS, SEL_ROWS, tq), lambda i, j: (i, 0, 0, j))],
        out_shape=[jax.ShapeDtypeStruct((b, t, NSA_WIDTH), F32),
                   jax.ShapeDtypeStruct((b, NSA_KV_HEADS, SEL_ROWS, t), BF16)],
        compiler_params=_cparams(("parallel", "parallel")),
        name="nsa_compressed_attn_select",
    )(qt, gate, kc, vct, ovt)


def _slc_kernel(qt_ref, nsel_ref, gate_ref, ksa_ref, vst_ref, o_ref, *, tq, kc):
    i = pl.program_id(1)
    qt = qt_ref[0]
    c_diag = (i * tq) // kc
    kpos = lax.broadcasted_iota(jnp.int32, (kc, tq), 0)
    tpos = i * tq + lax.broadcasted_iota(jnp.int32, (kc, tq), 1)
    qas = [_queries_t(qt, g, nsel_ref[0, g], True) for g in range(NSA_KV_HEADS)]

    def step(c, carry, causal):
        out = []
        for g, (m, acc) in enumerate(carry):
            k = ksa_ref[0, g, pl.ds(pl.multiple_of(c * kc, kc), kc), :]
            s = _dot(k, qas[g])
            if causal:
                s = s + _lane_tiled(jnp.where(c * kc + kpos <= tpos, 0.0, NEG_INF))
            m_new = jnp.maximum(m, jnp.max(s, axis=0, keepdims=True))
            p = jnp.exp2(s - m_new).astype(BF16)
            per = kc // SEL_CHUNK
            v = jnp.concatenate([vst_ref[0, g, c * per + u] for u in range(per)], axis=1)
            out.append((m_new, jnp.exp2(m - m_new) * acc + _dot(v, p)))
        return tuple(out)

    init = tuple((jnp.full((1, NSA_REP * tq), NEG_INF, F32), jnp.zeros((V_ROWS, NSA_REP * tq), F32))
                 for _ in range(NSA_KV_HEADS))
    carry = lax.fori_loop(0, c_diag, lambda c, carry: step(c, carry, False), init)
    for u in range(max(1, tq // kc)):
        carry = step(c_diag + u, carry, True)
    pieces = []
    for _, acc in carry:
        pieces.extend(_normalised_heads(acc, tq))
    o_ref[0] = _gated_heads_from_t(pieces, gate_ref, 1)


def _slc(qt, nsel, gate, ksa, vst):
    b, _, t = qt.shape
    tq = SEL_TQ
    kc = min(t, SEL_STEP)
    g = NSA_KV_HEADS
    return pl.pallas_call(
        functools.partial(_slc_kernel, tq=tq, kc=kc),
        grid=(b, t // tq),
        in_specs=[pl.BlockSpec((1, NSA_WIDTH, tq), lambda i, j: (i, 0, j)),
                  pl.BlockSpec((1, g, SEL_ROWS, tq), lambda i, j: (i, 0, 0, j)),
                  pl.BlockSpec((1, tq, LANES), lambda i, j: (i, j, 0)),
                  pl.BlockSpec((1, g, t, LANES), lambda i, j: (i, 0, 0, 0)),
                  pl.BlockSpec((1, g, t // SEL_CHUNK, V_ROWS, SEL_CHUNK), lambda i, j: (i, 0, 0, 0, 0))],
        out_specs=pl.BlockSpec((1, tq, NSA_WIDTH), lambda i, j: (i, j, 0)),
        out_shape=jax.ShapeDtypeStruct((b, t, NSA_WIDTH), F32),
        compiler_params=_cparams(("parallel", "parallel")),
        name="nsa_selected_attn",
    )(qt, nsel, gate, ksa, vst)


def _win_kernel(qt_ref, gate_ref, kw_ref, vwt_ref, o_ref, *, tq, nblk, t_total):
    i = pl.program_id(1)
    qt = qt_ref[0]
    kb = ATT_TQ
    span = nblk * kb
    first_blk = jnp.clip(i * (tq // kb) - WINDOW // kb, 0, t_total // kb - nblk)
    start = pl.multiple_of(first_blk * kb, kb)
    k = kw_ref[0, pl.ds(start, span), :]
    d = (i * tq + lax.broadcasted_iota(jnp.int32, (span, tq), 1)
         - (start + lax.broadcasted_iota(jnp.int32, (span, tq), 0)))
    bias = _lane_tiled(jnp.where((d >= 0) & (d < WINDOW), 0.0, NEG_INF))
    zeros = jnp.zeros((HEAD_DIM, tq), BF16)
    pieces = []
    for g in range(NSA_KV_HEADS):
        s = _dot(k, _queries_t(qt, g, zeros, g == 0)) + bias
        p = jnp.exp2(s - jnp.max(s, axis=0, keepdims=True)).astype(BF16)
        vt = jnp.concatenate([vwt_ref[0, g, first_blk + u] for u in range(nblk)], axis=1)
        pieces.extend(_normalised_heads(_dot(vt, p), tq))
    o_ref[0] = _gated_heads_from_t(pieces, gate_ref, 2)


def _win(qt, gate, kw, vwt):
    b, _, t = qt.shape
    tq = min(t, WIN_TQ)
    kb = ATT_TQ
    nblk = min(t // kb, (WINDOW + tq) // kb)
    g = NSA_KV_HEADS
    return pl.pallas_call(
        functools.partial(_win_kernel, tq=tq, nblk=nblk, t_total=t),
        grid=(b, t // tq),
        in_specs=[pl.BlockSpec((1, NSA_WIDTH, tq), lambda i, j: (i, 0, j)),
                  pl.BlockSpec((1, tq, LANES), lambda i, j: (i, j, 0)),
                  pl.BlockSpec((1, t, NSA_KV_WIDTH), lambda i, j: (i, 0, 0)),
                  pl.BlockSpec((1, g, t // kb, V_ROWS, kb), lambda i, j: (i, 0, 0, 0, 0))],
        out_specs=pl.BlockSpec((1, tq, NSA_WIDTH), lambda i, j: (i, j, 0)),
        out_shape=jax.ShapeDtypeStruct((b, t, NSA_WIDTH), F32),
        compiler_params=_cparams(("parallel", "parallel")),
        name="nsa_window_attn",
    )(qt, gate, kw, vwt)


N_MIX_IN = 6


def _post_kernel(*refs, alpha, fc):
    tiles, halos = refs[0:N_MIX_IN], refs[N_MIX_IN:2 * N_MIX_IN]
    (wo_ref, g1_ref, b1_ref, wg_ref, wu_ref, cw_ref, cb_ref, wd_ref, g2_ref, b2_ref,
     o_ref, x1_ref, xb_ref, act_ref) = refs[2 * N_MIX_IN:]
    j = pl.program_id(1)
    h = CONV_HALO

    def mixed(x_ref, ya_ref, yb_ref, oc_ref, os_ref, ow_ref):
        yc = oc_ref[0] + os_ref[0] + ow_ref[0]
        y = jnp.concatenate([ya_ref[0], yb_ref[0], yc], axis=1).astype(BF16)
        return _layernorm(alpha * x_ref[0] + _dot(y, wo_ref[...]), g1_ref[...], b1_ref[...])

    x1_ref[...] = mixed(*tiles)
    xb_ref[0:h, :] = jnp.where(j > 0, mixed(*halos), 0.0).astype(BF16)
    xb_ref[h:, :] = x1_ref[...].astype(BF16)
    for c in range(0, wg_ref.shape[1], fc):
        hg = _dot(xb_ref[...], wg_ref[:, c:c + fc])
        up = _dot(xb_ref[h:, :], wu_ref[:, c:c + fc])
        hc = (cb_ref[:, c:c + fc] + pltpu.roll(hg, 2, 0)[h:] * cw_ref[0:1, c:c + fc]
              + pltpu.roll(hg, 1, 0)[h:] * cw_ref[1:2, c:c + fc] + hg[h:] * cw_ref[2:3, c:c + fc])
        act_ref[:, c:c + fc] = (_gelu_tanh(hc) * up).astype(BF16)
    o_ref[0] = _layernorm(alpha * x1_ref[...] + _dot(act_ref[...], wd_ref[...]), g2_ref[...], b2_ref[...])


def _post(mix_in, wo, g1, b1, wg, wu, cw, cb, wd, g2, b2, alpha):
    b, t, d = mix_in[0].shape
    dff = wg.shape[1]
    tm = min(t, 512)
    fc = 2 * LANES
    r = tm // CONV_HALO
    const = lambda a: pl.BlockSpec(a.shape, lambda i, j: (0, 0), pipeline_mode=pl.Buffered(1))
    tile = lambda a: pl.BlockSpec((1, tm, a.shape[2]), lambda i, j: (i, j, 0))
    halo = lambda a: pl.BlockSpec((1, CONV_HALO, a.shape[2]), lambda i, j: (i, jnp.maximum(j * r - 1, 0), 0))
    weights = (wo, g1, b1, wg, wu, cw, cb, wd, g2, b2)
    assert len(mix_in) == N_MIX_IN
    return pl.pallas_call(
        functools.partial(_post_kernel, alpha=alpha, fc=fc),
        grid=(b, t // tm),
        in_specs=[tile(a) for a in mix_in] + [halo(a) for a in mix_in] + [const(a) for a in weights],
        out_specs=pl.BlockSpec((1, tm, d), lambda i, j: (i, j, 0)),
        out_shape=jax.ShapeDtypeStruct((b, t, d), F32),
        scratch_shapes=[pltpu.VMEM((tm, d), F32), pltpu.VMEM((CONV_HALO + tm, d), BF16),
                        pltpu.VMEM((tm, dff), BF16)],
        compiler_params=_cparams(("parallel", "parallel")),
        name="out_proj_ffn",
    )(*mix_in, *mix_in, *weights)


def _overlap_t(nc, n_slc):
    ci = np.arange(nc)[None, :]
    sj = np.arange(n_slc)[:, None]
    ov = np.clip(np.minimum(ci * CMP_STRIDE + CMP_BLOCK, (sj + 1) * SLC_BLOCK)
                 - np.maximum(ci * CMP_STRIDE, sj * SLC_BLOCK), 0, None).astype(np.float32) / CMP_STRIDE
    ov[:, nc - 1] = 0.0
    return jnp.asarray(ov, BF16)


def kernel(x, positions, w_in, w_out, pool_w, pool_scale, ret_gn_g, cmp_pos_k, cmp_w1_k, cmp_b1_k, cmp_w2_k,
           cmp_pos_v, cmp_w1_v, cmp_b1_v, cmp_w2_v, ffn_w_gate, ffn_w_up, ffn_conv_w, ffn_conv_b, ffn_w_down,
           ln1_g, ln1_b, ln2_g, ln2_b):
    b, t, d = x.shape
    depth = w_in.shape[0]
    alpha = float((2 * depth) ** 0.25)
    nc = t // CMP_STRIDE
    n_slc = t // SLC_BLOCK

    inv = ROPE_THETA ** (-jnp.arange(0, HEAD_DIM, 2, dtype=F32) / HEAD_DIM)
    inv = jnp.broadcast_to(inv[:, None], (HEAD_DIM // 2, LANES))
    cos, sin = _rope_tables(positions, inv)
    ends = jnp.minimum(jnp.arange(nc) * CMP_STRIDE + CMP_BLOCK - 1, t - 1)
    cos_e, sin_e = _rope_tables(positions[:, ends], inv)
    ovt = _overlap_t(nc, n_slc)
    eye_g = jnp.eye(len(POOL_WINDOWS), dtype=F32)

    for l in range(depth):
        w_in_p = jnp.pad(w_in[l], ((0, 0), (0, IN_PAD - IN_WIDTH))).astype(BF16)
        wbd = jnp.einsum('gcd,ge->gced', pool_w[l], eye_g).reshape(POOL_WIDTH, POOL_WIDTH).astype(BF16)
        y_a, h_ret, qt, k_cmp, v_cmp, gate3, ksa, vst, kw, vwt = _inproj(
            x, cos, sin, w_in_p, wbd, pool_scale[l][None, :])

        y_b = _retention(h_ret, ret_gn_g[l][None, :])

        wk = _compress_weights(cmp_pos_k[l], cmp_w1_k[l], cmp_b1_k[l], cmp_w2_k[l])
        wv = _compress_weights(cmp_pos_v[l], cmp_w1_v[l], cmp_b1_v[l], cmp_w2_v[l])
        kc, vct = _compress(k_cmp, v_cmp, wk, wv, cos_e, sin_e)
        o_cmp, nsel = _cmpattn(qt, gate3, kc, vct, ovt)
        o_slc = _slc(qt, nsel, gate3, ksa, vst)
        o_win = _win(qt, gate3, kw, vwt)

        x = _post((x, y_a, y_b, o_cmp, o_slc, o_win), w_out[l].astype(BF16), ln1_g[l][None, :], ln1_b[l][None, :],
                  ffn_w_gate[l].astype(BF16), ffn_w_up[l].astype(BF16), ffn_conv_w[l], ffn_conv_b[l][None, :],
                  ffn_w_down[l].astype(BF16), ln2_g[l][None, :], ln2_b[l][None, :], alpha)
    return x
```

```python
import functools
import math

import jax
import jax.numpy as jnp
import numpy as np
from jax import lax
from jax.experimental import pallas as pl
from jax.experimental.pallas import tpu as pltpu

F32 = jnp.float32
BF16 = jnp.bfloat16

HEAD_DIM = 64
POOL_WINDOWS = (2, 4, 8, 16)
POOL_GROUP_WIDTH = 64
POOL_WIDTH = POOL_GROUP_WIDTH * len(POOL_WINDOWS)
POOL_HALO = 16
RET_HEADS = 6
RET_WIDTH = RET_HEADS * HEAD_DIM
RET_CHUNK = 256
NSA_HEADS = 6
NSA_WIDTH = NSA_HEADS * HEAD_DIM
NSA_KV_HEADS = 2
NSA_REP = NSA_HEADS // NSA_KV_HEADS
NSA_KV_WIDTH = NSA_KV_HEADS * HEAD_DIM
CMP_BLOCK = 32
CMP_STRIDE = 16
CMP_HIDDEN = 128
SLC_BLOCK = 64
SLC_TOP = 16
WINDOW = 512
N_BRANCH = 3
FORCE_SCORE = 1e6
CONV_WIDTH = 3
CONV_HALO = 16
ROPE_THETA = 10000.0
LN_EPS = 1e-5
GN_EPS = 1e-5
NEG_INF = -1e30
Q_SCALE = HEAD_DIM ** -0.5 * math.log2(math.e)

LANES = 128
SUBLANES = 8
VMEM_LIMIT = 56 * 1024 * 1024

ATT_TQ = 128
SEL_TQ = 512
SEL_CHUNK = 512
SEL_STEP = 512
SEL_ROWS = LANES - HEAD_DIM
V_ROWS = HEAD_DIM + 16
RANK_STEP = 16
WIN_TQ = 256
RET_STEP_CHUNKS = 4
CMP_TQ = 512

OFF_POOL = 0
OFF_RET = OFF_POOL + POOL_WIDTH
OFF_QNSA = OFF_RET + 4 * RET_WIDTH
OFF_KCMP = OFF_QNSA + NSA_WIDTH
OFF_VCMP = OFF_KCMP + NSA_KV_WIDTH
OFF_KV4 = OFF_VCMP + NSA_KV_WIDTH
OFF_GATE = OFF_KV4 + 4 * NSA_KV_WIDTH
IN_WIDTH = OFF_GATE + NSA_HEADS * N_BRANCH
IN_PAD = OFF_GATE + LANES


def _cparams(sem, flags=None):
    return pltpu.CompilerParams(dimension_semantics=sem, vmem_limit_bytes=VMEM_LIMIT, flags=flags)


def _rope(x, cos, sin_signed):
    lane = lax.broadcasted_iota(jnp.int32, x.shape, 1)
    first = (lane & (HEAD_DIM - 1)) < HEAD_DIM // 2
    partner = jnp.where(first, pltpu.roll(x, LANES - HEAD_DIM // 2, 1), pltpu.roll(x, HEAD_DIM // 2, 1))
    return x * cos + partner * sin_signed


def _gelu_tanh(x):
    return 0.5 * x * (1.0 + jnp.tanh(math.sqrt(2.0 / math.pi) * (x + 0.044715 * (x * x * x))))


def _sigmoid(x):
    return 1.0 / (1.0 + jnp.exp(-x))


def _layernorm(y, g, b):
    mu = jnp.mean(y, axis=-1, keepdims=True)
    d = y - mu
    var = jnp.mean(d * d, axis=-1, keepdims=True)
    return d * lax.rsqrt(var + LN_EPS) * g + b


def _dot(a, b):
    return jnp.dot(a, b, preferred_element_type=F32)


def _dot_nt(a, b):
    return lax.dot_general(a, b, (((1,), (1,)), ((), ())), preferred_element_type=F32)


def _rope_table_kernel(pos_ref, inv_ref, cos_ref, sin_ref):
    lane = lax.broadcasted_iota(jnp.int32, (LANES, LANES), 1)
    first = (lane & (HEAD_DIM - 1)) < HEAD_DIM // 2
    inv = inv_ref[...]
    copies = LANES // inv.shape[0]
    for r in range(pos_ref.shape[1]):
        ang = pos_ref[0, r:r + 1, :].astype(F32) * inv
        c = jnp.transpose(jnp.concatenate([jnp.cos(ang)] * copies, axis=0))
        s = jnp.transpose(jnp.concatenate([jnp.sin(ang)] * copies, axis=0))
        cos_ref[0, r * LANES:(r + 1) * LANES, :] = c
        sin_ref[0, r * LANES:(r + 1) * LANES, :] = jnp.where(first, -s, s)


def _rope_tables(pos, inv):
    b, n = pos.shape
    shp = jax.ShapeDtypeStruct((b, n, LANES), F32)
    return pl.pallas_call(
        _rope_table_kernel,
        grid=(b,),
        in_specs=[pl.BlockSpec((1, n // LANES, LANES), lambda i: (i, 0, 0)),
                  pl.BlockSpec(inv.shape, lambda i: (0, 0))],
        out_specs=[pl.BlockSpec((1, n, LANES), lambda i: (i, 0, 0))] * 2,
        out_shape=[shp, shp],
        compiler_params=_cparams(("parallel",)),
        name="rope_tables",
    )(pos.reshape(b, n // LANES, LANES), inv)


def _pool_mix(cur, prev, t0, w, scale):
    cat = jnp.concatenate([prev, cur], axis=0)
    s2 = cat + pltpu.roll(cat, 1, 0)
    s4 = s2 + pltpu.roll(s2, 2, 0)
    s8 = s4 + pltpu.roll(s4, 4, 0)
    s16 = s8 + pltpu.roll(s8, 8, 0)
    grp = lax.broadcasted_iota(jnp.int32, cur.shape, 1) // POOL_GROUP_WIDTH
    t = t0 + lax.broadcasted_iota(jnp.int32, cur.shape, 0)

    def pick(a, b, c, d):
        return jnp.where(grp == 0, a, jnp.where(grp == 1, b, jnp.where(grp == 2, c, d)))

    h = POOL_HALO
    wsum = pick(s2[h:], s4[h:], s8[h:], s16[h:])
    width = pick(*[float(wd) for wd in POOL_WINDOWS])
    cnt = jnp.minimum((t + 1).astype(F32), width)
    mixed = wsum / cnt - cur
    return _dot(mixed.astype(BF16), w) * scale


def _inproj_kernel(x_ref, xh_ref, cos_ref, sin_ref, w_ref, wpool_ref, pscale_ref,
                   ya_ref, ret_ref, qt_ref, kcmp_ref, vcmp_ref, gate_ref, ksa_ref, vst_ref, kw_ref, vwt_ref,
                   stage_ref, *, tm):
    j = pl.program_id(1)
    cos = cos_ref[0]
    sin = sin_ref[0]
    xb = x_ref[0].astype(BF16)
    lane = lax.broadcasted_iota(jnp.int32, (tm, LANES), 1)
    lo = lane < HEAD_DIM
    tpos = j * tm + lax.broadcasted_iota(jnp.int32, (tm, LANES), 0)
    onehot = jnp.where(lane - HEAD_DIM == tpos // SLC_BLOCK, 1.0, 0.0)
    top = lax.broadcasted_iota(jnp.int32, (LANES, tm), 0) < HEAD_DIM

    def store_t(v, ref, chunk):
        vt = jnp.transpose(v)
        per_head = (jnp.where(top, vt, 1.0), jnp.where(top, pltpu.roll(vt, HEAD_DIM, 0), 1.0))
        for g, vg in enumerate(per_head):
            for u in range(tm // chunk):
                ref[0, g, u] = vg[0:V_ROWS, u * chunk:(u + 1) * chunk].astype(BF16)

    def emit(col, y):
        if col < OFF_RET:
            return
        if col < OFF_QNSA:
            part, k = divmod(col - OFF_RET, RET_WIDTH)
            if part == 0:
                y = _rope(y, cos, sin)
            elif part == 1:
                y = _rope(y, cos, sin) * (HEAD_DIM ** -0.5)
            ret_ref[0, :, col - OFF_RET:col - OFF_RET + LANES] = y
        elif col < OFF_KCMP:
            k = col - OFF_QNSA
            qt_ref[0, k:k + LANES, :] = jnp.transpose(_rope(y, cos, sin) * Q_SCALE).astype(BF16)
        elif col in (OFF_KCMP, OFF_VCMP):
            out_ref = kcmp_ref if col == OFF_KCMP else vcmp_ref
            stage_ref[...] = y
            for l in range(CMP_STRIDE):
                out_ref[0, :, l * LANES:(l + 1) * LANES] = stage_ref[pl.ds(l, tm // CMP_STRIDE, stride=CMP_STRIDE), :]
        elif col == OFF_KV4:
            ks = _rope(y, cos, sin)
            ksa_ref[0, 0] = jnp.where(lo, ks, onehot).astype(BF16)
            ksa_ref[0, 1] = jnp.where(lo, pltpu.roll(ks, HEAD_DIM, 1), onehot).astype(BF16)
        elif col == OFF_KV4 + NSA_KV_WIDTH:
            store_t(y, vst_ref, SEL_CHUNK)
        elif col == OFF_KV4 + 2 * NSA_KV_WIDTH:
            kw_ref[0] = _rope(y, cos, sin).astype(BF16)
        elif col == OFF_KV4 + 3 * NSA_KV_WIDTH:
            store_t(y, vwt_ref, ATT_TQ)
        else:
            gate_ref[0] = y

    for c in range(0, IN_PAD, 2 * LANES):
        y = _dot(xb, w_ref[:, c:c + 2 * LANES])
        if c == OFF_POOL:
            prev = jnp.where(j > 0, _dot(xh_ref[0].astype(BF16), w_ref[:, 0:POOL_WIDTH]), 0.0)
            ya_ref[0] = _pool_mix(y, prev, j * tm, wpool_ref[...], pscale_ref[...])
        for half in range(2):
            emit(c + half * LANES, y[:, half * LANES:(half + 1) * LANES])


def _inproj(x, cos, sin, w, wpool, pscale):
    b, t, d = x.shape
    tm = min(t, 512)
    g = NSA_KV_HEADS
    r = tm // POOL_HALO
    row = lambda wd: pl.BlockSpec((1, tm, wd), lambda i, j: (i, j, 0))
    const = lambda a: pl.BlockSpec(a.shape, lambda i, j: (0, 0), pipeline_mode=pl.Buffered(1))
    strided = pl.BlockSpec((1, tm // CMP_STRIDE, CMP_STRIDE * NSA_KV_WIDTH), lambda i, j: (i, j, 0))
    outs = [
        (row(POOL_WIDTH), (b, t, POOL_WIDTH), F32),
        (row(4 * RET_WIDTH), (b, t, 4 * RET_WIDTH), F32),
        (pl.BlockSpec((1, NSA_WIDTH, tm), lambda i, j: (i, 0, j)), (b, NSA_WIDTH, t), BF16),
        (strided, (b, t // CMP_STRIDE, CMP_STRIDE * NSA_KV_WIDTH), F32),
        (strided, (b, t // CMP_STRIDE, CMP_STRIDE * NSA_KV_WIDTH), F32),
        (row(LANES), (b, t, LANES), F32),
        (pl.BlockSpec((1, g, tm, LANES), lambda i, j: (i, 0, j, 0)), (b, g, t, LANES), BF16),
        (pl.BlockSpec((1, g, tm // SEL_CHUNK, V_ROWS, SEL_CHUNK), lambda i, j: (i, 0, j, 0, 0)),
         (b, g, t // SEL_CHUNK, V_ROWS, SEL_CHUNK), BF16),
        (row(NSA_KV_WIDTH), (b, t, NSA_KV_WIDTH), BF16),
        (pl.BlockSpec((1, g, tm // ATT_TQ, V_ROWS, ATT_TQ), lambda i, j: (i, 0, j, 0, 0)),
         (b, g, t // ATT_TQ, V_ROWS, ATT_TQ), BF16),
    ]
    return pl.pallas_call(
        functools.partial(_inproj_kernel, tm=tm),
        grid=(b, t // tm),
        in_specs=[row(d), pl.BlockSpec((1, POOL_HALO, d), lambda i, j: (i, jnp.maximum(j * r - 1, 0), 0)),
                  row(LANES), row(LANES), const(w), const(wpool), const(pscale)],
        out_specs=[o[0] for o in outs],
        out_shape=[jax.ShapeDtypeStruct(o[1], o[2]) for o in outs],
        scratch_shapes=[pltpu.VMEM((tm, NSA_KV_WIDTH), F32)],
        compiler_params=_cparams(("parallel", "parallel")),
        name="in_proj",
    )(x, x, cos, sin, w, wpool, pscale)


def _ret_consts():
    h, c = RET_HEADS, RET_CHUNK
    lg = np.log1p(-np.power(2.0, -5.0 - np.arange(h, dtype=np.float64)))
    i = np.arange(c, dtype=np.float64)
    diff = i[:, None] - i[None, :]
    dmask = np.where(diff >= 0, np.exp(lg[:, None, None] * np.maximum(diff, 0.0)), 0.0)
    xi = np.repeat(np.exp(lg[:, None] * (i + 1.0)).T, HEAD_DIM, axis=1)
    zeta = np.repeat(np.exp(lg[:, None] * (c - 1.0 - i)).T, HEAD_DIM, axis=1)
    gc = np.repeat(np.exp(lg * c), HEAD_DIM)[None, :]
    return (jnp.asarray(dmask, F32), jnp.asarray(xi, F32), jnp.asarray(zeta, F32), jnp.asarray(gc, F32))


def _ret_kernel(h_ref, dmask_ref, xi_ref, zeta_ref, gc_ref, gn_ref, o_ref, state_ref):
    @pl.when(pl.program_id(1) == 0)
    def _():
        state_ref[...] = jnp.zeros_like(state_ref)

    c = RET_CHUNK
    lane = lax.broadcasted_iota(jnp.int32, (c, LANES), 1)
    row = lax.broadcasted_iota(jnp.int32, (LANES, LANES), 0)
    col = lax.broadcasted_iota(jnp.int32, (LANES, LANES), 1)
    lo = lane < HEAD_DIM
    same_head = (row < HEAD_DIM) == (col < HEAD_DIM)
    states = [state_ref[j] for j in range(RET_WIDTH // LANES)]
    for u in range(h_ref.shape[1] // c):
        rs = slice(u * c, (u + 1) * c)
        for j in range(RET_WIDTH // LANES):
            sl = slice(j * LANES, (j + 1) * LANES)
            q = h_ref[0, rs, j * LANES:(j + 1) * LANES]
            k = h_ref[0, rs, RET_WIDTH + j * LANES:RET_WIDTH + (j + 1) * LANES]
            v = h_ref[0, rs, 2 * RET_WIDTH + j * LANES:2 * RET_WIDTH + (j + 1) * LANES]
            g = h_ref[0, rs, 3 * RET_WIDTH + j * LANES:3 * RET_WIDTH + (j + 1) * LANES]
            kb = k.astype(BF16)
            vb = v.astype(BF16)
            o = _dot(q.astype(BF16), states[j].astype(BF16)) * xi_ref[:, sl]
            for hh in range(2):
                m = lo if hh == 0 else jnp.logical_not(lo)
                qm = jnp.where(m, q, 0.0).astype(BF16)
                s = _dot_nt(qm, kb) * dmask_ref[2 * j + hh]
                o = o + jnp.where(m, _dot(s.astype(BF16), vb), 0.0)
            kz = (k * zeta_ref[:, sl]).astype(BF16)
            kv = lax.dot_general(kz, vb, (((0,), (0,)), ((), ())), preferred_element_type=F32)
            states[j] = gc_ref[:, sl] * states[j] + jnp.where(same_head, kv, 0.0)
            s_lo = jnp.sum(jnp.where(lo, o, 0.0), axis=-1, keepdims=True)
            s_hi = jnp.sum(jnp.where(lo, 0.0, o), axis=-1, keepdims=True)
            d = o - jnp.where(lo, s_lo, s_hi) * (1.0 / HEAD_DIM)
            d2 = d * d
            v_lo = jnp.sum(jnp.where(lo, d2, 0.0), axis=-1, keepdims=True)
            v_hi = jnp.sum(jnp.where(lo, 0.0, d2), axis=-1, keepdims=True)
            var = jnp.where(lo, v_lo, v_hi) * (1.0 / HEAD_DIM)
            on = d * lax.rsqrt(var + GN_EPS) * gn_ref[:, sl]
            o_ref[0, rs, sl] = g * _sigmoid(g) * on
    for j, st in enumerate(states):
        state_ref[j] = st


def _retention(hret, gn_g):
    b, t, _ = hret.shape
    c = RET_CHUNK
    rows = min(t, RET_STEP_CHUNKS * c)
    dmask, xi, zeta, gc = _ret_consts()
    const = lambda shape: pl.BlockSpec(shape, lambda i, j: (0,) * len(shape))
    return pl.pallas_call(
        _ret_kernel,
        grid=(b, t // rows),
        in_specs=[pl.BlockSpec((1, rows, 4 * RET_WIDTH), lambda i, j: (i, j, 0)),
                  const((RET_HEADS, c, c)), const((c, RET_WIDTH)), const((c, RET_WIDTH)),
                  const((1, RET_WIDTH)), const((1, RET_WIDTH))],
        out_specs=pl.BlockSpec((1, rows, RET_WIDTH), lambda i, j: (i, j, 0)),
        out_shape=jax.ShapeDtypeStruct((b, t, RET_WIDTH), F32),
        scratch_shapes=[pltpu.VMEM((RET_WIDTH // LANES, LANES, LANES), F32)],
        compiler_params=_cparams(("parallel", "arbitrary")),
        name="retention",
    )(hret, dmask, xi, zeta, gc, gn_g)


def _compress_kernel(k_ref, v_ref, ptk, pbk, wtk, wbk, b1k, w2k, ptv, pbv, wtv, wbv, b1v, w2v,
                     cos_ref, sin_ref, kc_ref, vct_ref):
    def comp(x_ref, pt, pb, wt, wb, b1, w2):
        x = x_ref[0]
        top = _dot((x + pt[...]).astype(BF16), wt[...])
        bot = _dot((x + pb[...]).astype(BF16), wb[...])
        rows = bot.shape[0]
        pre = top + pltpu.roll(bot, rows - 1, 0) + b1[...]
        return _dot(_gelu_tanh(pre).astype(BF16), w2[...])

    kc = comp(k_ref, ptk, pbk, wtk, wbk, b1k, w2k)
    kc_ref[0] = _rope(kc, cos_ref[0], sin_ref[0]).astype(BF16)
    vct_ref[0] = jnp.transpose(comp(v_ref, ptv, pbv, wtv, wbv, b1v, w2v)).astype(BF16)


def _compress_weights(pos, w1, b1, w2):
    half = CMP_BLOCK // 2
    g = NSA_KV_HEADS
    w1r = w1.reshape(CMP_BLOCK, HEAD_DIM, CMP_HIDDEN)
    eye = jnp.eye(g, dtype=F32)

    def lay(wh):
        return jnp.einsum('ldh,ge->lgdeh', wh, eye).reshape(half * g * HEAD_DIM, g * CMP_HIDDEN)

    def tile_pos(p):
        return jnp.broadcast_to(p[:, None, :], (half, g, HEAD_DIM)).reshape(1, half * g * HEAD_DIM)

    w2bd = jnp.einsum('hd,ge->ghed', w2, eye).reshape(g * CMP_HIDDEN, g * HEAD_DIM)
    return (tile_pos(pos[:half]), tile_pos(pos[half:]), lay(w1r[:half]).astype(BF16), lay(w1r[half:]).astype(BF16),
            jnp.tile(b1, g)[None, :], w2bd.astype(BF16))


def _compress(kc_in, vc_in, wk, wv, cos_e, sin_e):
    b, nc, width = kc_in.shape
    const = lambda a: pl.BlockSpec(a.shape, lambda i: (0,) * a.ndim)
    blk = lambda w: pl.BlockSpec((1, nc, w), lambda i: (i, 0, 0))
    return pl.pallas_call(
        _compress_kernel,
        grid=(b,),
        in_specs=[blk(width), blk(width)] + [const(a) for a in wk] + [const(a) for a in wv]
                 + [blk(LANES), blk(LANES)],
        out_specs=[blk(NSA_KV_WIDTH), pl.BlockSpec((1, NSA_KV_WIDTH, nc), lambda i: (i, 0, 0))],
        out_shape=[jax.ShapeDtypeStruct((b, nc, NSA_KV_WIDTH), BF16),
                   jax.ShapeDtypeStruct((b, NSA_KV_WIDTH, nc), BF16)],
        compiler_params=_cparams(("parallel",)),
        name="nsa_compress",
    )(kc_in, vc_in, *wk, *wv, cos_e, sin_e)


def _lane_tiled(x):
    return jnp.concatenate([x] * NSA_REP, axis=1)


def _queries_t(qt, g, other, q_first):
    cols = []
    for r in range(NSA_REP):
        hq = g * NSA_REP + r
        piece = qt[hq * HEAD_DIM:(hq + 1) * HEAD_DIM]
        cols.append(jnp.concatenate([piece, other] if q_first else [other, piece], axis=0))
    return jnp.concatenate(cols, axis=1)


def _gated_heads_from_t(pieces, gate_ref, branch):
    sg = _sigmoid(jnp.transpose(gate_ref[0]))
    rows = [hq * N_BRANCH + branch for hq in range(NSA_HEADS)]
    gated = [p * sg[r:r + 1] for p, r in zip(pieces, rows)]
    slabs = [jnp.transpose(jnp.concatenate(gated[2 * j:2 * j + 2], axis=0)) for j in range(NSA_HEADS // 2)]
    return jnp.concatenate(slabs, axis=1)


def _normalised_heads(acc, tq):
    ot = acc[0:HEAD_DIM] * (1.0 / acc[HEAD_DIM:HEAD_DIM + 1])
    return [ot[:, r * tq:(r + 1) * tq] for r in range(NSA_REP)]


def _cmpattn_kernel(qt_ref, gate_ref, kc_ref, vct_ref, ovt_ref, o_ref, nsel_ref, *, tq, n_slc):
    i = pl.program_id(1)
    qt = qt_ref[0]
    kc = kc_ref[0]
    vct = vct_ref[0]
    nc = kc.shape[0]
    n = lax.broadcasted_iota(jnp.int32, (nc, tq), 0)
    t = i * tq + lax.broadcasted_iota(jnp.int32, (nc, tq), 1)
    bias = _lane_tiled(jnp.where(n * CMP_STRIDE + (CMP_BLOCK - 1) <= t, 0.0, NEG_INF))
    sees_any = _lane_tiled(jnp.where(i * tq + lax.broadcasted_iota(jnp.int32, (1, tq), 1) >= CMP_BLOCK - 1, 1.0, 0.0))
    zeros = jnp.zeros((HEAD_DIM, tq), BF16)
    blk = lax.broadcasted_iota(jnp.int32, (n_slc, tq), 0)
    tl = i * tq + lax.broadcasted_iota(jnp.int32, (n_slc, tq), 1)
    cur = tl // SLC_BLOCK
    forced = (blk == 0) | (blk == cur) | (blk == cur - 1)
    valid = blk * SLC_BLOCK <= tl
    ovt = ovt_ref[...]
    pieces = []
    scores = []
    for g in range(NSA_KV_HEADS):
        s = _dot(kc, _queries_t(qt, g, zeros, g == 0)) + bias
        e = jnp.exp2(s - jnp.max(s, axis=0, keepdims=True))
        p = e * (sees_any * (1.0 / jnp.sum(e, axis=0, keepdims=True)))
        ot = _dot(vct, p.astype(BF16))
        pieces.extend(ot[g * HEAD_DIM:(g + 1) * HEAD_DIM, r * tq:(r + 1) * tq] for r in range(NSA_REP))
        psum = p[:, 0:tq] + p[:, tq:2 * tq] + p[:, 2 * tq:3 * tq]
        hi = psum.astype(BF16)
        rest = psum - hi.astype(F32)
        mid = rest.astype(BF16)
        lo = (rest - mid.astype(F32)).astype(BF16)
        imp = _dot(ovt, hi) + _dot(ovt, mid) + _dot(ovt, lo)
        scores.append(jnp.where(valid, jnp.where(forced, FORCE_SCORE, imp), -1.0))
    o_ref[0] = _gated_heads_from_t(pieces, gate_ref, 0)

    n_live = ((i + 1) * tq) // SLC_BLOCK
    n_cls = n_slc // RANK_STEP
    cls = jnp.minimum((n_live - 1) // RANK_STEP, n_cls - 1)
    for c in range(n_cls):
        @pl.when(cls == c)
        def _(c=c):
            for g in range(NSA_KV_HEADS):
                nsel_ref[0, g] = _selection_bias(scores[g], (c + 1) * RANK_STEP, min(SLC_TOP, n_slc), tq)


def _selection_bias(score, nb, top, tq):
    sub = lax.broadcasted_iota(jnp.int32, (SUBLANES, tq), 0)
    ties = [jnp.where(sub > k, 1.0, 0.0) for k in range(SUBLANES)]
    groups = [score[SUBLANES * r:SUBLANES * (r + 1)] for r in range(nb // SUBLANES)]
    ranks = [jnp.zeros((SUBLANES, tq), F32) for _ in groups]
    for jp in range(nb):
        rowv = score[jp:jp + 1, :]
        for r, grp in enumerate(groups):
            if SUBLANES * r > jp:
                inc = jnp.where(rowv >= grp, 1.0, 0.0)
            elif SUBLANES * r + SUBLANES - 1 < jp:
                inc = jnp.where(rowv > grp, 1.0, 0.0)
            else:
                inc = jnp.where(rowv > grp, 1.0, jnp.where(rowv == grp, ties[jp - SUBLANES * r], 0.0))
            ranks[r] = ranks[r] + inc
    bias = jnp.where(jnp.concatenate(ranks, axis=0) < float(top), 0.0, NEG_INF)
    if nb < SEL_ROWS:
        bias = jnp.concatenate([bias, jnp.full((SEL_ROWS - nb, tq), NEG_INF, F32)], axis=0)
    return bias.astype(BF16)


def _cmpattn(qt, gate, kc, vct, ovt):
    b, _, t = qt.shape
    nc = kc.shape[1]
    tq = CMP_TQ
    n_slc = t // SLC_BLOCK
    assert n_slc <= SEL_ROWS and n_slc % RANK_STEP == 0
    blk = lambda w: pl.BlockSpec((1, tq, w), lambda i, j: (i, j, 0))
    return pl.pallas_call(
        functools.partial(_cmpattn_kernel, tq=tq, n_slc=n_slc),
        grid=(b, t // tq),
        in_specs=[pl.BlockSpec((1, NSA_WIDTH, tq), lambda i, j: (i, 0, j)), blk(LANES),
                  pl.BlockSpec((1, nc, NSA_KV_WIDTH), lambda i, j: (i, 0, 0)),
                  pl.BlockSpec((1, NSA_KV_WIDTH, nc), lambda i, j: (i, 0, 0)),
                  pl.BlockSpec(ovt.shape, lambda i, j: (0, 0))],
        out_specs=[blk(NSA_WIDTH),
                   pl.BlockSpec((1, NSA_KV_HEADS, SEL_ROWS, tq), lambda i, j: (i, 0, 0, j))],
        out_shape=[jax.ShapeDtypeStruct((b, t, NSA_WIDTH), F32),
                   jax.ShapeDtypeStruct((b, NSA_KV_HEADS, SEL_ROWS, t), BF16)],
        compiler_params=_cparams(("parallel", "parallel")),
        name="nsa_compressed_attn_select",
    )(qt, gate, kc, vct, ovt)


def _slc_kernel(qt_ref, nsel_ref, gate_ref, ksa_ref, vst_ref, o_ref, *, tq, kc):
    i = pl.program_id(1)
    qt = qt_ref[0]
    c_diag = (i * tq) // kc
    kpos = lax.broadcasted_iota(jnp.int32, (kc, tq), 0)
    tpos = i * tq + lax.broadcasted_iota(jnp.int32, (kc, tq), 1)
    qas = [_queries_t(qt, g, nsel_ref[0, g], True) for g in range(NSA_KV_HEADS)]

    def step(c, carry, causal):
        out = []
        for g, (m, acc) in enumerate(carry):
            k = ksa_ref[0, g, pl.ds(pl.multiple_of(c * kc, kc), kc), :]
            s = _dot(k, qas[g])
            if causal:
                s = s + _lane_tiled(jnp.where(c * kc + kpos <= tpos, 0.0, NEG_INF))
            m_new = jnp.maximum(m, jnp.max(s, axis=0, keepdims=True))
            p = jnp.exp2(s - m_new).astype(BF16)
            per = kc // SEL_CHUNK
            v = jnp.concatenate([vst_ref[0, g, c * per + u] for u in range(per)], axis=1)
            out.append((m_new, jnp.exp2(m - m_new) * acc + _dot(v, p)))
        return tuple(out)

    init = tuple((jnp.full((1, NSA_REP * tq), NEG_INF, F32), jnp.zeros((V_ROWS, NSA_REP * tq), F32))
                 for _ in range(NSA_KV_HEADS))
    carry = lax.fori_loop(0, c_diag, lambda c, carry: step(c, carry, False), init)
    for u in range(max(1, tq // kc)):
        carry = step(c_diag + u, carry, True)
    pieces = []
    for _, acc in carry:
        pieces.extend(_normalised_heads(acc, tq))
    o_ref[0] = _gated_heads_from_t(pieces, gate_ref, 1)


def _slc(qt, nsel, gate, ksa, vst):
    b, _, t = qt.shape
    tq = SEL_TQ
    kc = min(t, SEL_STEP)
    g = NSA_KV_HEADS
    return pl.pallas_call(
        functools.partial(_slc_kernel, tq=tq, kc=kc),
        grid=(b, t // tq),
        in_specs=[pl.BlockSpec((1, NSA_WIDTH, tq), lambda i, j: (i, 0, j)),
                  pl.BlockSpec((1, g, SEL_ROWS, tq), lambda i, j: (i, 0, 0, j)),
                  pl.BlockSpec((1, tq, LANES), lambda i, j: (i, j, 0)),
                  pl.BlockSpec((1, g, t, LANES), lambda i, j: (i, 0, 0, 0)),
                  pl.BlockSpec((1, g, t // SEL_CHUNK, V_ROWS, SEL_CHUNK), lambda i, j: (i, 0, 0, 0, 0))],
        out_specs=pl.BlockSpec((1, tq, NSA_WIDTH), lambda i, j: (i, j, 0)),
        out_shape=jax.ShapeDtypeStruct((b, t, NSA_WIDTH), F32),
        compiler_params=_cparams(("parallel", "parallel")),
        name="nsa_selected_attn",
    )(qt, nsel, gate, ksa, vst)


def _win_kernel(qt_ref, gate_ref, kw_ref, vwt_ref, o_ref, *, tq, nblk, t_total):
    i = pl.program_id(1)
    qt = qt_ref[0]
    kb = ATT_TQ
    span = nblk * kb
    first_blk = jnp.clip(i * (tq // kb) - WINDOW // kb, 0, t_total // kb - nblk)
    start = pl.multiple_of(first_blk * kb, kb)
    k = kw_ref[0, pl.ds(start, span), :]
    d = (i * tq + lax.broadcasted_iota(jnp.int32, (span, tq), 1)
         - (start + lax.broadcasted_iota(jnp.int32, (span, tq), 0)))
    bias = _lane_tiled(jnp.where((d >= 0) & (d < WINDOW), 0.0, NEG_INF))
    zeros = jnp.zeros((HEAD_DIM, tq), BF16)
    pieces = []
    for g in range(NSA_KV_HEADS):
        s = _dot(k, _queries_t(qt, g, zeros, g == 0)) + bias
        p = jnp.exp2(s - jnp.max(s, axis=0, keepdims=True)).astype(BF16)
        vt = jnp.concatenate([vwt_ref[0, g, first_blk + u] for u in range(nblk)], axis=1)
        pieces.extend(_normalised_heads(_dot(vt, p), tq))
    o_ref[0] = _gated_heads_from_t(pieces, gate_ref, 2)


def _win(qt, gate, kw, vwt):
    b, _, t = qt.shape
    tq = min(t, WIN_TQ)
    kb = ATT_TQ
    nblk = min(t // kb, (WINDOW + tq) // kb)
    g = NSA_KV_HEADS
    return pl.pallas_call(
        functools.partial(_win_kernel, tq=tq, nblk=nblk, t_total=t),
        grid=(b, t // tq),
        in_specs=[pl.BlockSpec((1, NSA_WIDTH, tq), lambda i, j: (i, 0, j)),
                  pl.BlockSpec((1, tq, LANES), lambda i, j: (i, j, 0)),
                  pl.BlockSpec((1, t, NSA_KV_WIDTH), lambda i, j: (i, 0, 0)),
                  pl.BlockSpec((1, g, t // kb, V_ROWS, kb), lambda i, j: (i, 0, 0, 0, 0))],
        out_specs=pl.BlockSpec((1, tq, NSA_WIDTH), lambda i, j: (i, j, 0)),
        out_shape=jax.ShapeDtypeStruct((b, t, NSA_WIDTH), F32),
        compiler_params=_cparams(("parallel", "parallel")),
        name="nsa_window_attn",
    )(qt, gate, kw, vwt)


N_MIX_IN = 6


def _post_kernel(*refs, alpha, fc):
    tiles, halos = refs[0:N_MIX_IN], refs[N_MIX_IN:2 * N_MIX_IN]
    (wo_ref, g1_ref, b1_ref, wg_ref, wu_ref, cw_ref, cb_ref, wd_ref, g2_ref, b2_ref,
     o_ref, x1_ref, xb_ref, act_ref) = refs[2 * N_MIX_IN:]
    j = pl.program_id(1)
    h = CONV_HALO

    def mixed(x_ref, ya_ref, yb_ref, oc_ref, os_ref, ow_ref):
        yc = oc_ref[0] + os_ref[0] + ow_ref[0]
        y = jnp.concatenate([ya_ref[0], yb_ref[0], yc], axis=1).astype(BF16)
        return _layernorm(alpha * x_ref[0] + _dot(y, wo_ref[...]), g1_ref[...], b1_ref[...])

    x1_ref[...] = mixed(*tiles)
    xb_ref[0:h, :] = jnp.where(j > 0, mixed(*halos), 0.0).astype(BF16)
    xb_ref[h:, :] = x1_ref[...].astype(BF16)
    for c in range(0, wg_ref.shape[1], fc):
        hg = _dot(xb_ref[...], wg_ref[:, c:c + fc])
        up = _dot(xb_ref[h:, :], wu_ref[:, c:c + fc])
        hc = (cb_ref[:, c:c + fc] + pltpu.roll(hg, 2, 0)[h:] * cw_ref[0:1, c:c + fc]
              + pltpu.roll(hg, 1, 0)[h:] * cw_ref[1:2, c:c + fc] + hg[h:] * cw_ref[2:3, c:c + fc])
        act_ref[:, c:c + fc] = (_gelu_tanh(hc) * up).astype(BF16)
    o_ref[0] = _layernorm(alpha * x1_ref[...] + _dot(act_ref[...], wd_ref[...]), g2_ref[...], b2_ref[...])


def _post(mix_in, wo, g1, b1, wg, wu, cw, cb, wd, g2, b2, alpha):
    b, t, d = mix_in[0].shape
    dff = wg.shape[1]
    tm = min(t, 512)
    fc = 2 * LANES
    r = tm // CONV_HALO
    const = lambda a: pl.BlockSpec(a.shape, lambda i, j: (0, 0), pipeline_mode=pl.Buffered(1))
    tile = lambda a: pl.BlockSpec((1, tm, a.shape[2]), lambda i, j: (i, j, 0))
    halo = lambda a: pl.BlockSpec((1, CONV_HALO, a.shape[2]), lambda i, j: (i, jnp.maximum(j * r - 1, 0), 0))
    weights = (wo, g1, b1, wg, wu, cw, cb, wd, g2, b2)
    assert len(mix_in) == N_MIX_IN
    return pl.pallas_call(
        functools.partial(_post_kernel, alpha=alpha, fc=fc),
        grid=(b, t // tm),
        in_specs=[tile(a) for a in mix_in] + [halo(a) for a in mix_in] + [const(a) for a in weights],
        out_specs=pl.BlockSpec((1, tm, d), lambda i, j: (i, j, 0)),
        out_shape=jax.ShapeDtypeStruct((b, t, d), F32),
        scratch_shapes=[pltpu.VMEM((tm, d), F32), pltpu.VMEM((CONV_HALO + tm, d), BF16),
                        pltpu.VMEM((tm, dff), BF16)],
        compiler_params=_cparams(("parallel", "parallel")),
        name="out_proj_ffn",
    )(*mix_in, *mix_in, *weights)


def _overlap_t(nc, n_slc):
    ci = np.arange(nc)[None, :]
    sj = np.arange(n_slc)[:, None]
    ov = np.clip(np.minimum(ci * CMP_STRIDE + CMP_BLOCK, (sj + 1) * SLC_BLOCK)
                 - np.maximum(ci * CMP_STRIDE, sj * SLC_BLOCK), 0, None).astype(np.float32) / CMP_STRIDE
    ov[:, nc - 1] = 0.0
    return jnp.asarray(ov, BF16)


def kernel(x, positions, w_in, w_out, pool_w, pool_scale, ret_gn_g, cmp_pos_k, cmp_w1_k, cmp_b1_k, cmp_w2_k,
           cmp_pos_v, cmp_w1_v, cmp_b1_v, cmp_w2_v, ffn_w_gate, ffn_w_up, ffn_conv_w, ffn_conv_b, ffn_w_down,
           ln1_g, ln1_b, ln2_g, ln2_b):
    b, t, d = x.shape
    depth = w_in.shape[0]
    alpha = float((2 * depth) ** 0.25)
    nc = t // CMP_STRIDE
    n_slc = t // SLC_BLOCK

    inv = ROPE_THETA ** (-jnp.arange(0, HEAD_DIM, 2, dtype=F32) / HEAD_DIM)
    inv = jnp.broadcast_to(inv[:, None], (HEAD_DIM // 2, LANES))
    cos, sin = _rope_tables(positions, inv)
    ends = jnp.minimum(jnp.arange(nc) * CMP_STRIDE + CMP_BLOCK - 1, t - 1)
    cos_e, sin_e = _rope_tables(positions[:, ends], inv)
    ovt = _overlap_t(nc, n_slc)
    eye_g = jnp.eye(len(POOL_WINDOWS), dtype=F32)

    for l in range(depth):
        w_in_p = jnp.pad(w_in[l], ((0, 0), (0, IN_PAD - IN_WIDTH))).astype(BF16)
        wbd = jnp.einsum('gcd,ge->gced', pool_w[l], eye_g).reshape(POOL_WIDTH, POOL_WIDTH).astype(BF16)
        y_a, h_ret, qt, k_cmp, v_cmp, gate3, ksa, vst, kw, vwt = _inproj(
            x, cos, sin, w_in_p, wbd, pool_scale[l][None, :])

        y_b = _retention(h_ret, ret_gn_g[l][None, :])

        wk = _compress_weights(cmp_pos_k[l], cmp_w1_k[l], cmp_b1_k[l], cmp_w2_k[l])
        wv = _compress_weights(cmp_pos_v[l], cmp_w1_v[l], cmp_b1_v[l], cmp_w2_v[l])
        kc, vct = _compress(k_cmp, v_cmp, wk, wv, cos_e, sin_e)
        o_cmp, nsel = _cmpattn(qt, gate3, kc, vct, ovt)
        o_slc = _slc(qt, nsel, gate3, ksa, vst)
        o_win = _win(qt, gate3, kw, vwt)

        x = _post((x, y_a, y_b, o_cmp, o_slc, o_win), w_out[l].astype(BF16), ln1_g[l][None, :], ln1_b[l][None, :],
                  ffn_w_gate[l].astype(BF16), ffn_w_up[l].astype(BF16), ffn_conv_w[l], ffn_conv_b[l][None, :],
                  ffn_w_down[l].astype(BF16), ln2_g[l][None, :], ln2_b[l][None, :], alpha)
    return x
```

```python
import functools
import math

import jax
import jax.numpy as jnp
import numpy as np
from jax import lax
from jax.experimental import pallas as pl
from jax.experimental.pallas import tpu as pltpu

F32 = jnp.float32
BF16 = jnp.bfloat16

HEAD_DIM = 64
POOL_WINDOWS = (2, 4, 8, 16)
POOL_GROUP_WIDTH = 64
POOL_WIDTH = POOL_GROUP_WIDTH * len(POOL_WINDOWS)
POOL_HALO = 16
RET_HEADS = 6
RET_WIDTH = RET_HEADS * HEAD_DIM
RET_CHUNK = 256
NSA_HEADS = 6
NSA_WIDTH = NSA_HEADS * HEAD_DIM
NSA_KV_HEADS = 2
NSA_REP = NSA_HEADS // NSA_KV_HEADS
NSA_KV_WIDTH = NSA_KV_HEADS * HEAD_DIM
CMP_BLOCK = 32
CMP_STRIDE = 16
CMP_HIDDEN = 128
SLC_BLOCK = 64
SLC_TOP = 16
WINDOW = 512
N_BRANCH = 3
FORCE_SCORE = 1e6
CONV_WIDTH = 3
CONV_HALO = 16
ROPE_THETA = 10000.0
LN_EPS = 1e-5
GN_EPS = 1e-5
NEG_INF = -1e30
Q_SCALE = HEAD_DIM ** -0.5 * math.log2(math.e)

LANES = 128
SUBLANES = 8
VMEM_LIMIT = 56 * 1024 * 1024

ATT_TQ = 128
SEL_TQ = 512
SEL_CHUNK = 512
SEL_STEP = 512
SEL_ROWS = LANES - HEAD_DIM
V_ROWS = HEAD_DIM + 16
RANK_STEP = 16
WIN_TQ = 256
RET_STEP_CHUNKS = 4
CMP_TQ = 1024

OFF_POOL = 0
OFF_RET = OFF_POOL + POOL_WIDTH
OFF_QNSA = OFF_RET + 4 * RET_WIDTH
OFF_KCMP = OFF_QNSA + NSA_WIDTH
OFF_VCMP = OFF_KCMP + NSA_KV_WIDTH
OFF_KV4 = OFF_VCMP + NSA_KV_WIDTH
OFF_GATE = OFF_KV4 + 4 * NSA_KV_WIDTH
IN_WIDTH = OFF_GATE + NSA_HEADS * N_BRANCH
IN_PAD = OFF_GATE + LANES


def _cparams(sem, flags=None):
    return pltpu.CompilerParams(dimension_semantics=sem, vmem_limit_bytes=VMEM_LIMIT, flags=flags)


def _rope(x, cos, sin_signed):
    lane = lax.broadcasted_iota(jnp.int32, x.shape, 1)
    first = (lane & (HEAD_DIM - 1)) < HEAD_DIM // 2
    partner = jnp.where(first, pltpu.roll(x, LANES - HEAD_DIM // 2, 1), pltpu.roll(x, HEAD_DIM // 2, 1))
    return x * cos + partner * sin_signed


def _gelu_tanh(x):
    return 0.5 * x * (1.0 + jnp.tanh(math.sqrt(2.0 / math.pi) * (x + 0.044715 * (x * x * x))))


def _sigmoid(x):
    return 1.0 / (1.0 + jnp.exp(-x))


def _layernorm(y, g, b):
    mu = jnp.mean(y, axis=-1, keepdims=True)
    d = y - mu
    var = jnp.mean(d * d, axis=-1, keepdims=True)
    return d * lax.rsqrt(var + LN_EPS) * g + b


def _dot(a, b):
    return jnp.dot(a, b, preferred_element_type=F32)


def _dot_nt(a, b):
    return lax.dot_general(a, b, (((1,), (1,)), ((), ())), preferred_element_type=F32)


def _rope_table_kernel(pos_ref, inv_ref, cos_ref, sin_ref):
    lane = lax.broadcasted_iota(jnp.int32, (LANES, LANES), 1)
    first = (lane & (HEAD_DIM - 1)) < HEAD_DIM // 2
    inv = inv_ref[...]
    copies = LANES // inv.shape[0]
    for r in range(pos_ref.shape[1]):
        ang = pos_ref[0, r:r + 1, :].astype(F32) * inv
        c = jnp.transpose(jnp.concatenate([jnp.cos(ang)] * copies, axis=0))
        s = jnp.transpose(jnp.concatenate([jnp.sin(ang)] * copies, axis=0))
        cos_ref[0, r * LANES:(r + 1) * LANES, :] = c
        sin_ref[0, r * LANES:(r + 1) * LANES, :] = jnp.where(first, -s, s)


def _rope_tables(pos, inv):
    b, n = pos.shape
    shp = jax.ShapeDtypeStruct((b, n, LANES), F32)
    return pl.pallas_call(
        _rope_table_kernel,
        grid=(b,),
        in_specs=[pl.BlockSpec((1, n // LANES, LANES), lambda i: (i, 0, 0)),
                  pl.BlockSpec(inv.shape, lambda i: (0, 0))],
        out_specs=[pl.BlockSpec((1, n, LANES), lambda i: (i, 0, 0))] * 2,
        out_shape=[shp, shp],
        compiler_params=_cparams(("parallel",)),
        name="rope_tables",
    )(pos.reshape(b, n // LANES, LANES), inv)


def _pool_mix(cur, prev, t0, w, scale):
    cat = jnp.concatenate([prev, cur], axis=0)
    s2 = cat + pltpu.roll(cat, 1, 0)
    s4 = s2 + pltpu.roll(s2, 2, 0)
    s8 = s4 + pltpu.roll(s4, 4, 0)
    s16 = s8 + pltpu.roll(s8, 8, 0)
    grp = lax.broadcasted_iota(jnp.int32, cur.shape, 1) // POOL_GROUP_WIDTH
    t = t0 + lax.broadcasted_iota(jnp.int32, cur.shape, 0)

    def pick(a, b, c, d):
        return jnp.where(grp == 0, a, jnp.where(grp == 1, b, jnp.where(grp == 2, c, d)))

    h = POOL_HALO
    wsum = pick(s2[h:], s4[h:], s8[h:], s16[h:])
    width = pick(*[float(wd) for wd in POOL_WINDOWS])
    cnt = jnp.minimum((t + 1).astype(F32), width)
    mixed = wsum / cnt - cur
    return _dot(mixed.astype(BF16), w) * scale


def _inproj_kernel(x_ref, xh_ref, cos_ref, sin_ref, w_ref, wpool_ref, pscale_ref,
                   ya_ref, ret_ref, qt_ref, kcmp_ref, vcmp_ref, gate_ref, ksa_ref, vst_ref, kw_ref, vwt_ref,
                   stage_ref, *, tm):
    j = pl.program_id(1)
    cos = cos_ref[0]
    sin = sin_ref[0]
    xb = x_ref[0].astype(BF16)
    lane = lax.broadcasted_iota(jnp.int32, (tm, LANES), 1)
    lo = lane < HEAD_DIM
    tpos = j * tm + lax.broadcasted_iota(jnp.int32, (tm, LANES), 0)
    onehot = jnp.where(lane - HEAD_DIM == tpos // SLC_BLOCK, 1.0, 0.0)
    top = lax.broadcasted_iota(jnp.int32, (LANES, tm), 0) < HEAD_DIM

    def store_t(v, ref, chunk):
        vt = jnp.transpose(v)
        per_head = (jnp.where(top, vt, 1.0), jnp.where(top, pltpu.roll(vt, HEAD_DIM, 0), 1.0))
        for g, vg in enumerate(per_head):
            for u in range(tm // chunk):
                ref[0, g, u] = vg[0:V_ROWS, u * chunk:(u + 1) * chunk].astype(BF16)

    def emit(col, y):
        if col < OFF_RET:
            return
        if col < OFF_QNSA:
            part, k = divmod(col - OFF_RET, RET_WIDTH)
            if part == 0:
                y = _rope(y, cos, sin)
            elif part == 1:
                y = _rope(y, cos, sin) * (HEAD_DIM ** -0.5)
            ret_ref[0, :, col - OFF_RET:col - OFF_RET + LANES] = y
        elif col < OFF_KCMP:
            k = col - OFF_QNSA
            qt_ref[0, k:k + LANES, :] = jnp.transpose(_rope(y, cos, sin) * Q_SCALE).astype(BF16)
        elif col in (OFF_KCMP, OFF_VCMP):
            out_ref = kcmp_ref if col == OFF_KCMP else vcmp_ref
            stage_ref[...] = y
            for l in range(CMP_STRIDE):
                out_ref[0, :, l * LANES:(l + 1) * LANES] = stage_ref[pl.ds(l, tm // CMP_STRIDE, stride=CMP_STRIDE), :]
        elif col == OFF_KV4:
            ks = _rope(y, cos, sin)
            ksa_ref[0, 0] = jnp.where(lo, ks, onehot).astype(BF16)
            ksa_ref[0, 1] = jnp.where(lo, pltpu.roll(ks, HEAD_DIM, 1), onehot).astype(BF16)
        elif col == OFF_KV4 + NSA_KV_WIDTH:
            store_t(y, vst_ref, SEL_CHUNK)
        elif col == OFF_KV4 + 2 * NSA_KV_WIDTH:
            kw_ref[0] = _rope(y, cos, sin).astype(BF16)
        elif col == OFF_KV4 + 3 * NSA_KV_WIDTH:
            store_t(y, vwt_ref, ATT_TQ)
        else:
            gate_ref[0] = y

    for c in range(0, IN_PAD, 2 * LANES):
        y = _dot(xb, w_ref[:, c:c + 2 * LANES])
        if c == OFF_POOL:
            prev = jnp.where(j > 0, _dot(xh_ref[0].astype(BF16), w_ref[:, 0:POOL_WIDTH]), 0.0)
            ya_ref[0] = _pool_mix(y, prev, j * tm, wpool_ref[...], pscale_ref[...])
        for half in range(2):
            emit(c + half * LANES, y[:, half * LANES:(half + 1) * LANES])


def _inproj(x, cos, sin, w, wpool, pscale):
    b, t, d = x.shape
    tm = min(t, 512)
    g = NSA_KV_HEADS
    r = tm // POOL_HALO
    row = lambda wd: pl.BlockSpec((1, tm, wd), lambda i, j: (i, j, 0))
    const = lambda a: pl.BlockSpec(a.shape, lambda i, j: (0, 0), pipeline_mode=pl.Buffered(1))
    strided = pl.BlockSpec((1, tm // CMP_STRIDE, CMP_STRIDE * NSA_KV_WIDTH), lambda i, j: (i, j, 0))
    outs = [
        (row(POOL_WIDTH), (b, t, POOL_WIDTH), F32),
        (row(4 * RET_WIDTH), (b, t, 4 * RET_WIDTH), F32),
        (pl.BlockSpec((1, NSA_WIDTH, tm), lambda i, j: (i, 0, j)), (b, NSA_WIDTH, t), BF16),
        (strided, (b, t // CMP_STRIDE, CMP_STRIDE * NSA_KV_WIDTH), F32),
        (strided, (b, t // CMP_STRIDE, CMP_STRIDE * NSA_KV_WIDTH), F32),
        (row(LANES), (b, t, LANES), F32),
        (pl.BlockSpec((1, g, tm, LANES), lambda i, j: (i, 0, j, 0)), (b, g, t, LANES), BF16),
        (pl.BlockSpec((1, g, tm // SEL_CHUNK, V_ROWS, SEL_CHUNK), lambda i, j: (i, 0, j, 0, 0)),
         (b, g, t // SEL_CHUNK, V_ROWS, SEL_CHUNK), BF16),
        (row(NSA_KV_WIDTH), (b, t, NSA_KV_WIDTH), BF16),
        (pl.BlockSpec((1, g, tm // ATT_TQ, V_ROWS, ATT_TQ), lambda i, j: (i, 0, j, 0, 0)),
         (b, g, t // ATT_TQ, V_ROWS, ATT_TQ), BF16),
    ]
    return pl.pallas_call(
        functools.partial(_inproj_kernel, tm=tm),
        grid=(b, t // tm),
        in_specs=[row(d), pl.BlockSpec((1, POOL_HALO, d), lambda i, j: (i, jnp.maximum(j * r - 1, 0), 0)),
                  row(LANES), row(LANES), const(w), const(wpool), const(pscale)],
        out_specs=[o[0] for o in outs],
        out_shape=[jax.ShapeDtypeStruct(o[1], o[2]) for o in outs],
        scratch_shapes=[pltpu.VMEM((tm, NSA_KV_WIDTH), F32)],
        compiler_params=_cparams(("parallel", "parallel")),
        name="in_proj",
    )(x, x, cos, sin, w, wpool, pscale)


def _ret_consts():
    h, c = RET_HEADS, RET_CHUNK
    lg = np.log1p(-np.power(2.0, -5.0 - np.arange(h, dtype=np.float64)))
    i = np.arange(c, dtype=np.float64)
    diff = i[:, None] - i[None, :]
    dmask = np.where(diff >= 0, np.exp(lg[:, None, None] * np.maximum(diff, 0.0)), 0.0)
    xi = np.repeat(np.exp(lg[:, None] * (i + 1.0)).T, HEAD_DIM, axis=1)
    zeta = np.repeat(np.exp(lg[:, None] * (c - 1.0 - i)).T, HEAD_DIM, axis=1)
    gc = np.repeat(np.exp(lg * c), HEAD_DIM)[None, :]
    return (jnp.asarray(dmask, F32), jnp.asarray(xi, F32), jnp.asarray(zeta, F32), jnp.asarray(gc, F32))


def _ret_kernel(h_ref, dmask_ref, xi_ref, zeta_ref, gc_ref, gn_ref, o_ref, state_ref):
    @pl.when(pl.program_id(1) == 0)
    def _():
        state_ref[...] = jnp.zeros_like(state_ref)

    c = RET_CHUNK
    lane = lax.broadcasted_iota(jnp.int32, (c, LANES), 1)
    row = lax.broadcasted_iota(jnp.int32, (LANES, LANES), 0)
    col = lax.broadcasted_iota(jnp.int32, (LANES, LANES), 1)
    lo = lane < HEAD_DIM
    same_head = (row < HEAD_DIM) == (col < HEAD_DIM)
    states = [state_ref[j] for j in range(RET_WIDTH // LANES)]
    for u in range(h_ref.shape[1] // c):
        rs = slice(u * c, (u + 1) * c)
        for j in range(RET_WIDTH // LANES):
            sl = slice(j * LANES, (j + 1) * LANES)
            q = h_ref[0, rs, j * LANES:(j + 1) * LANES]
            k = h_ref[0, rs, RET_WIDTH + j * LANES:RET_WIDTH + (j + 1) * LANES]
            v = h_ref[0, rs, 2 * RET_WIDTH + j * LANES:2 * RET_WIDTH + (j + 1) * LANES]
            g = h_ref[0, rs, 3 * RET_WIDTH + j * LANES:3 * RET_WIDTH + (j + 1) * LANES]
            kb = k.astype(BF16)
            vb = v.astype(BF16)
            o = _dot(q.astype(BF16), states[j].astype(BF16)) * xi_ref[:, sl]
            for hh in range(2):
                m = lo if hh == 0 else jnp.logical_not(lo)
                qm = jnp.where(m, q, 0.0).astype(BF16)
                s = _dot_nt(qm, kb) * dmask_ref[2 * j + hh]
                o = o + jnp.where(m, _dot(s.astype(BF16), vb), 0.0)
            kz = (k * zeta_ref[:, sl]).astype(BF16)
            kv = lax.dot_general(kz, vb, (((0,), (0,)), ((), ())), preferred_element_type=F32)
            states[j] = gc_ref[:, sl] * states[j] + jnp.where(same_head, kv, 0.0)
            s_lo = jnp.sum(jnp.where(lo, o, 0.0), axis=-1, keepdims=True)
            s_hi = jnp.sum(jnp.where(lo, 0.0, o), axis=-1, keepdims=True)
            d = o - jnp.where(lo, s_lo, s_hi) * (1.0 / HEAD_DIM)
            d2 = d * d
            v_lo = jnp.sum(jnp.where(lo, d2, 0.0), axis=-1, keepdims=True)
            v_hi = jnp.sum(jnp.where(lo, 0.0, d2), axis=-1, keepdims=True)
            var = jnp.where(lo, v_lo, v_hi) * (1.0 / HEAD_DIM)
            on = d * lax.rsqrt(var + GN_EPS) * gn_ref[:, sl]
            o_ref[0, rs, sl] = g * _sigmoid(g) * on
    for j, st in enumerate(states):
        state_ref[j] = st


def _retention(hret, gn_g):
    b, t, _ = hret.shape
    c = RET_CHUNK
    rows = min(t, RET_STEP_CHUNKS * c)
    dmask, xi, zeta, gc = _ret_consts()
    const = lambda shape: pl.BlockSpec(shape, lambda i, j: (0,) * len(shape))
    return pl.pallas_call(
        _ret_kernel,
        grid=(b, t // rows),
        in_specs=[pl.BlockSpec((1, rows, 4 * RET_WIDTH), lambda i, j: (i, j, 0)),
                  const((RET_HEADS, c, c)), const((c, RET_WIDTH)), const((c, RET_WIDTH)),
                  const((1, RET_WIDTH)), const((1, RET_WIDTH))],
        out_specs=pl.BlockSpec((1, rows, RET_WIDTH), lambda i, j: (i, j, 0)),
        out_shape=jax.ShapeDtypeStruct((b, t, RET_WIDTH), F32),
        scratch_shapes=[pltpu.VMEM((RET_WIDTH // LANES, LANES, LANES), F32)],
        compiler_params=_cparams(("parallel", "arbitrary")),
        name="retention",
    )(hret, dmask, xi, zeta, gc, gn_g)


def _compress_kernel(k_ref, v_ref, ptk, pbk, wtk, wbk, b1k, w2k, ptv, pbv, wtv, wbv, b1v, w2v,
                     cos_ref, sin_ref, kc_ref, vct_ref):
    def comp(x_ref, pt, pb, wt, wb, b1, w2):
        x = x_ref[0]
        top = _dot((x + pt[...]).astype(BF16), wt[...])
        bot = _dot((x + pb[...]).astype(BF16), wb[...])
        rows = bot.shape[0]
        pre = top + pltpu.roll(bot, rows - 1, 0) + b1[...]
        return _dot(_gelu_tanh(pre).astype(BF16), w2[...])

    kc = comp(k_ref, ptk, pbk, wtk, wbk, b1k, w2k)
    kc_ref[0] = _rope(kc, cos_ref[0], sin_ref[0]).astype(BF16)
    vct_ref[0] = jnp.transpose(comp(v_ref, ptv, pbv, wtv, wbv, b1v, w2v)).astype(BF16)


def _compress_weights(pos, w1, b1, w2):
    half = CMP_BLOCK // 2
    g = NSA_KV_HEADS
    w1r = w1.reshape(CMP_BLOCK, HEAD_DIM, CMP_HIDDEN)
    eye = jnp.eye(g, dtype=F32)

    def lay(wh):
        return jnp.einsum('ldh,ge->lgdeh', wh, eye).reshape(half * g * HEAD_DIM, g * CMP_HIDDEN)

    def tile_pos(p):
        return jnp.broadcast_to(p[:, None, :], (half, g, HEAD_DIM)).reshape(1, half * g * HEAD_DIM)

    w2bd = jnp.einsum('hd,ge->ghed', w2, eye).reshape(g * CMP_HIDDEN, g * HEAD_DIM)
    return (tile_pos(pos[:half]), tile_pos(pos[half:]), lay(w1r[:half]).astype(BF16), lay(w1r[half:]).astype(BF16),
            jnp.tile(b1, g)[None, :], w2bd.astype(BF16))


def _compress(kc_in, vc_in, wk, wv, cos_e, sin_e):
    b, nc, width = kc_in.shape
    const = lambda a: pl.BlockSpec(a.shape, lambda i: (0,) * a.ndim)
    blk = lambda w: pl.BlockSpec((1, nc, w), lambda i: (i, 0, 0))
    return pl.pallas_call(
        _compress_kernel,
        grid=(b,),
        in_specs=[blk(width), blk(width)] + [const(a) for a in wk] + [const(a) for a in wv]
                 + [blk(LANES), blk(LANES)],
        out_specs=[blk(NSA_KV_WIDTH), pl.BlockSpec((1, NSA_KV_WIDTH, nc), lambda i: (i, 0, 0))],
        out_shape=[jax.ShapeDtypeStruct((b, nc, NSA_KV_WIDTH), BF16),
                   jax.ShapeDtypeStruct((b, NSA_KV_WIDTH, nc), BF16)],
        compiler_params=_cparams(("parallel",)),
        name="nsa_compress",
    )(kc_in, vc_in, *wk, *wv, cos_e, sin_e)


def _lane_tiled(x):
    return jnp.concatenate([x] * NSA_REP, axis=1)


def _queries_t(qt, g, other, q_first):
    cols = []
    for r in range(NSA_REP):
        hq = g * NSA_REP + r
        piece = qt[hq * HEAD_DIM:(hq + 1) * HEAD_DIM]
        cols.append(jnp.concatenate([piece, other] if q_first else [other, piece], axis=0))
    return jnp.concatenate(cols, axis=1)


def _gated_heads_from_t(pieces, gate_ref, branch):
    sg = _sigmoid(jnp.transpose(gate_ref[0]))
    rows = [hq * N_BRANCH + branch for hq in range(NSA_HEADS)]
    gated = [p * sg[r:r + 1] for p, r in zip(pieces, rows)]
    slabs = [jnp.transpose(jnp.concatenate(gated[2 * j:2 * j + 2], axis=0)) for j in range(NSA_HEADS // 2)]
    return jnp.concatenate(slabs, axis=1)


def _normalised_heads(acc, tq):
    ot = acc[0:HEAD_DIM] * (1.0 / acc[HEAD_DIM:HEAD_DIM + 1])
    return [ot[:, r * tq:(r + 1) * tq] for r in range(NSA_REP)]


def _cmpattn_kernel(qt_ref, gate_ref, kc_ref, vct_ref, ovt_ref, o_ref, nsel_ref, *, tq, n_slc):
    i = pl.program_id(1)
    qt = qt_ref[0]
    kc = kc_ref[0]
    vct = vct_ref[0]
    nc = kc.shape[0]
    n = lax.broadcasted_iota(jnp.int32, (nc, tq), 0)
    t = i * tq + lax.broadcasted_iota(jnp.int32, (nc, tq), 1)
    bias = _lane_tiled(jnp.where(n * CMP_STRIDE + (CMP_BLOCK - 1) <= t, 0.0, NEG_INF))
    sees_any = _lane_tiled(jnp.where(i * tq + lax.broadcasted_iota(jnp.int32, (1, tq), 1) >= CMP_BLOCK - 1, 1.0, 0.0))
    zeros = jnp.zeros((HEAD_DIM, tq), BF16)
    blk = lax.broadcasted_iota(jnp.int32, (n_slc, tq), 0)
    tl = i * tq + lax.broadcasted_iota(jnp.int32, (n_slc, tq), 1)
    cur = tl // SLC_BLOCK
    forced = (blk == 0) | (blk == cur) | (blk == cur - 1)
    valid = blk * SLC_BLOCK <= tl
    ovt = ovt_ref[...]
    pieces = []
    scores = []
    for g in range(NSA_KV_HEADS):
        s = _dot(kc, _queries_t(qt, g, zeros, g == 0)) + bias
        e = jnp.exp2(s - jnp.max(s, axis=0, keepdims=True))
        p = e * (sees_any * (1.0 / jnp.sum(e, axis=0, keepdims=True)))
        ot = _dot(vct, p.astype(BF16))
        pieces.extend(ot[g * HEAD_DIM:(g + 1) * HEAD_DIM, r * tq:(r + 1) * tq] for r in range(NSA_REP))
        psum = p[:, 0:tq] + p[:, tq:2 * tq] + p[:, 2 * tq:3 * tq]
        hi = psum.astype(BF16)
        rest = psum - hi.astype(F32)
        mid = rest.astype(BF16)
        lo = (rest - mid.astype(F32)).astype(BF16)
        imp = _dot(ovt, hi) + _dot(ovt, mid) + _dot(ovt, lo)
        scores.append(jnp.where(valid, jnp.where(forced, FORCE_SCORE, imp), -1.0))
    o_ref[0] = _gated_heads_from_t(pieces, gate_ref, 0)

    n_live = ((i + 1) * tq) // SLC_BLOCK
    n_cls = n_slc // RANK_STEP
    cls = jnp.minimum((n_live - 1) // RANK_STEP, n_cls - 1)
    for c in range(n_cls):
        @pl.when(cls == c)
        def _(c=c):
            for g in range(NSA_KV_HEADS):
                nsel_ref[0, g] = _selection_bias(scores[g], (c + 1) * RANK_STEP, min(SLC_TOP, n_slc), tq)


def _selection_bias(score, nb, top, tq):
    sub = lax.broadcasted_iota(jnp.int32, (SUBLANES, tq), 0)
    ties = [jnp.where(sub > k, 1.0, 0.0) for k in range(SUBLANES)]
    groups = [score[SUBLANES * r:SUBLANES * (r + 1)] for r in range(nb // SUBLANES)]
    ranks = [jnp.zeros((SUBLANES, tq), F32) for _ in groups]
    for jp in range(nb):
        rowv = score[jp:jp + 1, :]
        for r, grp in enumerate(groups):
            if SUBLANES * r > jp:
                inc = jnp.where(rowv >= grp, 1.0, 0.0)
            elif SUBLANES * r + SUBLANES - 1 < jp:
                inc = jnp.where(rowv > grp, 1.0, 0.0)
            else:
                inc = jnp.where(rowv > grp, 1.0, jnp.where(rowv == grp, ties[jp - SUBLANES * r], 0.0))
            ranks[r] = ranks[r] + inc
    bias = jnp.where(jnp.concatenate(ranks, axis=0) < float(top), 0.0, NEG_INF)
    if nb < SEL_ROWS:
        bias = jnp.concatenate([bias, jnp.full((SEL_ROWS - nb, tq), NEG_INF, F32)], axis=0)
    return bias.astype(BF16)


def _cmpattn(qt, gate, kc, vct, ovt):
    b, _, t = qt.shape
    nc = kc.shape[1]
    tq = CMP_TQ
    n_slc = t // SLC_BLOCK
    assert n_slc <= SEL_ROWS and n_slc % RANK_STEP == 0
    blk = lambda w: pl.BlockSpec((1, tq, w), lambda i, j: (i, j, 0))
    return pl.pallas_call(
        functools.partial(_cmpattn_kernel, tq=tq, n_slc=n_slc),
        grid=(b, t // tq),
        in_specs=[pl.BlockSpec((1, NSA_WIDTH, tq), lambda i, j: (i, 0, j)), blk(LANES),
                  pl.BlockSpec((1, nc, NSA_KV_WIDTH), lambda i, j: (i, 0, 0)),
                  pl.BlockSpec((1, NSA_KV_WIDTH, nc), lambda i, j: (i, 0, 0)),
                  pl.BlockSpec(ovt.shape, lambda i, j: (0, 0))],
        out_specs=[blk(NSA_WIDTH),
                   pl.BlockSpec((1, NSA_KV_HEADS, SEL_ROWS, tq), lambda i, j: (i, 0, 0, j))],
        out_shape=[jax.ShapeDtypeStruct((b, t, NSA_WIDTH), F32),
                   jax.ShapeDtypeStruct((b, NSA_KV_HEADS, SEL_ROWS, t), BF16)],
        compiler_params=_cparams(("parallel", "parallel")),
        name="nsa_compressed_attn_select",
    )(qt, gate, kc, vct, ovt)


def _slc_kernel(qt_ref, nsel_ref, gate_ref, ksa_ref, vst_ref, o_ref, *, tq, kc):
    i = pl.program_id(1)
    qt = qt_ref[0]
    c_diag = (i * tq) // kc
    kpos = lax.broadcasted_iota(jnp.int32, (kc, tq), 0)
    tpos = i * tq + lax.broadcasted_iota(jnp.int32, (kc, tq), 1)
    qas = [_queries_t(qt, g, nsel_ref[0, g], True) for g in range(NSA_KV_HEADS)]

    def step(c, carry, causal):
        out = []
        for g, (m, acc) in enumerate(carry):
            k = ksa_ref[0, g, pl.ds(pl.multiple_of(c * kc, kc), kc), :]
            s = _dot(k, qas[g])
            if causal:
                s = s + _lane_tiled(jnp.where(c * kc + kpos <= tpos, 0.0, NEG_INF))
            m_new = jnp.maximum(m, jnp.max(s, axis=0, keepdims=True))
            p = jnp.exp2(s - m_new).astype(BF16)
            per = kc // SEL_CHUNK
            v = jnp.concatenate([vst_ref[0, g, c * per + u] for u in range(per)], axis=1)
            out.append((m_new, jnp.exp2(m - m_new) * acc + _dot(v, p)))
        return tuple(out)

    init = tuple((jnp.full((1, NSA_REP * tq), NEG_INF, F32), jnp.zeros((V_ROWS, NSA_REP * tq), F32))
                 for _ in range(NSA_KV_HEADS))
    carry = lax.fori_loop(0, c_diag, lambda c, carry: step(c, carry, False), init)
    for u in range(max(1, tq // kc)):
        carry = step(c_diag + u, carry, True)
    pieces = []
    for _, acc in carry:
        pieces.extend(_normalised_heads(acc, tq))
    o_ref[0] = _gated_heads_from_t(pieces, gate_ref, 1)


def _slc(qt, nsel, gate, ksa, vst):
    b, _, t = qt.shape
    tq = SEL_TQ
    kc = min(t, SEL_STEP)
    g = NSA_KV_HEADS
    return pl.pallas_call(
        functools.partial(_slc_kernel, tq=tq, kc=kc),
        grid=(b, t // tq),
        in_specs=[pl.BlockSpec((1, NSA_WIDTH, tq), lambda i, j: (i, 0, j)),
                  pl.BlockSpec((1, g, SEL_ROWS, tq), lambda i, j: (i, 0, 0, j)),
                  pl.BlockSpec((1, tq, LANES), lambda i, j: (i, j, 0)),
                  pl.BlockSpec((1, g, t, LANES), lambda i, j: (i, 0, 0, 0)),
                  pl.BlockSpec((1, g, t // SEL_CHUNK, V_ROWS, SEL_CHUNK), lambda i, j: (i, 0, 0, 0, 0))],
        out_specs=pl.BlockSpec((1, tq, NSA_WIDTH), lambda i, j: (i, j, 0)),
        out_shape=jax.ShapeDtypeStruct((b, t, NSA_WIDTH), F32),
        compiler_params=_cparams(("parallel", "parallel")),
        name="nsa_selected_attn",
    )(qt, nsel, gate, ksa, vst)


def _win_kernel(qt_ref, gate_ref, kw_ref, vwt_ref, o_ref, *, tq, nblk, t_total):
    i = pl.program_id(1)
    qt = qt_ref[0]
    kb = ATT_TQ
    span = nblk * kb
    first_blk = jnp.clip(i * (tq // kb) - WINDOW // kb, 0, t_total // kb - nblk)
    start = pl.multiple_of(first_blk * kb, kb)
    k = kw_ref[0, pl.ds(start, span), :]
    d = (i * tq + lax.broadcasted_iota(jnp.int32, (span, tq), 1)
         - (start + lax.broadcasted_iota(jnp.int32, (span, tq), 0)))
    bias = _lane_tiled(jnp.where((d >= 0) & (d < WINDOW), 0.0, NEG_INF))
    zeros = jnp.zeros((HEAD_DIM, tq), BF16)
    pieces = []
    for g in range(NSA_KV_HEADS):
        s = _dot(k, _queries_t(qt, g, zeros, g == 0)) + bias
        p = jnp.exp2(s - jnp.max(s, axis=0, keepdims=True)).astype(BF16)
        vt = jnp.concatenate([vwt_ref[0, g, first_blk + u] for u in range(nblk)], axis=1)
        pieces.extend(_normalised_heads(_dot(vt, p), tq))
    o_ref[0] = _gated_heads_from_t(pieces, gate_ref, 2)


def _win(qt, gate, kw, vwt):
    b, _, t = qt.shape
    tq = min(t, WIN_TQ)
    kb = ATT_TQ
    nblk = min(t // kb, (WINDOW + tq) // kb)
    g = NSA_KV_HEADS
    return pl.pallas_call(
        functools.partial(_win_kernel, tq=tq, nblk=nblk, t_total=t),
        grid=(b, t // tq),
        in_specs=[pl.BlockSpec((1, NSA_WIDTH, tq), lambda i, j: (i, 0, j)),
                  pl.BlockSpec((1, tq, LANES), lambda i, j: (i, j, 0)),
                  pl.BlockSpec((1, t, NSA_KV_WIDTH), lambda i, j: (i, 0, 0)),
                  pl.BlockSpec((1, g, t // kb, V_ROWS, kb), lambda i, j: (i, 0, 0, 0, 0))],
        out_specs=pl.BlockSpec((1, tq, NSA_WIDTH), lambda i, j: (i, j, 0)),
        out_shape=jax.ShapeDtypeStruct((b, t, NSA_WIDTH), F32),
        compiler_params=_cparams(("parallel", "parallel")),
        name="nsa_window_attn",
    )(qt, gate, kw, vwt)


N_MIX_IN = 6


def _post_kernel(*refs, alpha, fc):
    tiles, halos = refs[0:N_MIX_IN], refs[N_MIX_IN:2 * N_MIX_IN]
    (wo_ref, g1_ref, b1_ref, wg_ref, wu_ref, cw_ref, cb_ref, wd_ref, g2_ref, b2_ref,
     o_ref, x1_ref, xb_ref, act_ref) = refs[2 * N_MIX_IN:]
    j = pl.program_id(1)
    h = CONV_HALO

    def mixed(x_ref, ya_ref, yb_ref, oc_ref, os_ref, ow_ref):
        yc = oc_ref[0] + os_ref[0] + ow_ref[0]
        y = jnp.concatenate([ya_ref[0], yb_ref[0], yc], axis=1).astype(BF16)
        return _layernorm(alpha * x_ref[0] + _dot(y, wo_ref[...]), g1_ref[...], b1_ref[...])

    x1_ref[...] = mixed(*tiles)
    xb_ref[0:h, :] = jnp.where(j > 0, mixed(*halos), 0.0).astype(BF16)
    xb_ref[h:, :] = x1_ref[...].astype(BF16)
    for c in range(0, wg_ref.shape[1], fc):
        hg = _dot(xb_ref[...], wg_ref[:, c:c + fc])
        up = _dot(xb_ref[h:, :], wu_ref[:, c:c + fc])
        hc = (cb_ref[:, c:c + fc] + pltpu.roll(hg, 2, 0)[h:] * cw_ref[0:1, c:c + fc]
              + pltpu.roll(hg, 1, 0)[h:] * cw_ref[1:2, c:c + fc] + hg[h:] * cw_ref[2:3, c:c + fc])
        act_ref[:, c:c + fc] = (_gelu_tanh(hc) * up).astype(BF16)
    o_ref[0] = _layernorm(alpha * x1_ref[...] + _dot(act_ref[...], wd_ref[...]), g2_ref[...], b2_ref[...])


def _post(mix_in, wo, g1, b1, wg, wu, cw, cb, wd, g2, b2, alpha):
    b, t, d = mix_in[0].shape
    dff = wg.shape[1]
    tm = min(t, 512)
    fc = 2 * LANES
    r = tm // CONV_HALO
    const = lambda a: pl.BlockSpec(a.shape, lambda i, j: (0, 0), pipeline_mode=pl.Buffered(1))
    tile = lambda a: pl.BlockSpec((1, tm, a.shape[2]), lambda i, j: (i, j, 0))
    halo = lambda a: pl.BlockSpec((1, CONV_HALO, a.shape[2]), lambda i, j: (i, jnp.maximum(j * r - 1, 0), 0))
    weights = (wo, g1, b1, wg, wu, cw, cb, wd, g2, b2)
    assert len(mix_in) == N_MIX_IN
    return pl.pallas_call(
        functools.partial(_post_kernel, alpha=alpha, fc=fc),
        grid=(b, t // tm),
        in_specs=[tile(a) for a in mix_in] + [halo(a) for a in mix_in] + [const(a) for a in weights],
        out_specs=pl.BlockSpec((1, tm, d), lambda i, j: (i, j, 0)),
        out_shape=jax.ShapeDtypeStruct((b, t, d), F32),
        scratch_shapes=[pltpu.VMEM((tm, d), F32), pltpu.VMEM((CONV_HALO + tm, d), BF16),
                        pltpu.VMEM((tm, dff), BF16)],
        compiler_params=_cparams(("parallel", "parallel")),
        name="out_proj_ffn",
    )(*mix_in, *mix_in, *weights)


def _overlap_t(nc, n_slc):
    ci = np.arange(nc)[None, :]
    sj = np.arange(n_slc)[:, None]
    ov = np.clip(np.minimum(ci * CMP_STRIDE + CMP_BLOCK, (sj + 1) * SLC_BLOCK)
                 - np.maximum(ci * CMP_STRIDE, sj * SLC_BLOCK), 0, None).astype(np.float32) / CMP_STRIDE
    ov[:, nc - 1] = 0.0
    return jnp.asarray(ov, BF16)


def kernel(x, positions, w_in, w_out, pool_w, pool_scale, ret_gn_g, cmp_pos_k, cmp_w1_k, cmp_b1_k, cmp_w2_k,
           cmp_pos_v, cmp_w1_v, cmp_b1_v, cmp_w2_v, ffn_w_gate, ffn_w_up, ffn_conv_w, ffn_conv_b, ffn_w_down,
           ln1_g, ln1_b, ln2_g, ln2_b):
    b, t, d = x.shape
    depth = w_in.shape[0]
    alpha = float((2 * depth) ** 0.25)
    nc = t // CMP_STRIDE
    n_slc = t // SLC_BLOCK

    inv = ROPE_THETA ** (-jnp.arange(0, HEAD_DIM, 2, dtype=F32) / HEAD_DIM)
    inv = jnp.broadcast_to(inv[:, None], (HEAD_DIM // 2, LANES))
    cos, sin = _rope_tables(positions, inv)
    ends = jnp.minimum(jnp.arange(nc) * CMP_STRIDE + CMP_BLOCK - 1, t - 1)
    cos_e, sin_e = _rope_tables(positions[:, ends], inv)
    ovt = _overlap_t(nc, n_slc)
    eye_g = jnp.eye(len(POOL_WINDOWS), dtype=F32)

    for l in range(depth):
        w_in_p = jnp.pad(w_in[l], ((0, 0), (0, IN_PAD - IN_WIDTH))).astype(BF16)
        wbd = jnp.einsum('gcd,ge->gced', pool_w[l], eye_g).reshape(POOL_WIDTH, POOL_WIDTH).astype(BF16)
        y_a, h_ret, qt, k_cmp, v_cmp, gate3, ksa, vst, kw, vwt = _inproj(
            x, cos, sin, w_in_p, wbd, pool_scale[l][None, :])

        y_b = _retention(h_ret, ret_gn_g[l][None, :])

        wk = _compress_weights(cmp_pos_k[l], cmp_w1_k[l], cmp_b1_k[l], cmp_w2_k[l])
        wv = _compress_weights(cmp_pos_v[l], cmp_w1_v[l], cmp_b1_v[l], cmp_w2_v[l])
        kc, vct = _compress(k_cmp, v_cmp, wk, wv, cos_e, sin_e)
        o_cmp, nsel = _cmpattn(qt, gate3, kc, vct, ovt)
        o_slc = _slc(qt, nsel, gate3, ksa, vst)
        o_win = _win(qt, gate3, kw, vwt)

        x = _post((x, y_a, y_b, o_cmp, o_slc, o_win), w_out[l].astype(BF16), ln1_g[l][None, :], ln1_b[l][None, :],
                  ffn_w_gate[l].astype(BF16), ffn_w_up[l].astype(BF16), ffn_conv_w[l], ffn_conv_b[l][None, :],
                  ffn_w_down[l].astype(BF16), ln2_g[l][None, :], ln2_b[l][None, :], alpha)
    return x
```
